```python
import jax, jax.numpy as jnp
from jax import lax
import numpy as np

D_MODEL = 2048
BATCH = 8
SEQ = 4096
DEPTH = 4

D_FF = 5632
POOL_WINDOWS = (2, 4, 8, 16)
N_POOL_GROUPS = len(POOL_WINDOWS)
POOL_WIDTH = 1024
POOL_GROUP_DIM = POOL_WIDTH // N_POOL_GROUPS
SGU_HEADS = 8
SGU_HEAD_DIM = 128
SGU_WIDTH = SGU_HEADS * SGU_HEAD_DIM
CHUNK = 128
IN_PROJ_WIDTH = POOL_WIDTH + 2 * SGU_WIDTH + 2 * D_MODEL
MACARON_WEIGHT = 0.5
EPS = 1e-6

kernel_name = "hybrid_pool_sgu_macaron_block"


def rmsnorm(x, g):
    xf = x.astype(jnp.float32)
    var = jnp.mean(xf * xf, axis=-1, keepdims=True)
    return (xf * lax.rsqrt(var + EPS)).astype(x.dtype) * g


def swiglu(h, w_up, w_down):
    gate, up = jnp.split(h @ w_up, 2, axis=-1)
    return (jax.nn.silu(gate) * up) @ w_down


def pool_mixer(p, w_group, scale):
    B, S, _ = p.shape
    maxw = POOL_WINDOWS[-1]
    pf = p.astype(jnp.float32)
    cs = jnp.cumsum(pf, axis=1)
    cs_pad = jnp.pad(cs, ((0, 0), (maxw, 0), (0, 0)))
    pos = jnp.arange(1, S + 1, dtype=jnp.int32)
    outs = []
    for g, w in enumerate(POOL_WINDOWS):
        sl = slice(g * POOL_GROUP_DIM, (g + 1) * POOL_GROUP_DIM)
        prev = cs_pad[:, maxw - w: maxw - w + S, sl]
        cnt = jnp.minimum(pos, w).astype(jnp.float32)[None, :, None]
        outs.append((cs[:, :, sl] - prev) / cnt - pf[:, :, sl])
    d = jnp.stack(outs, axis=2).astype(p.dtype)
    y = jnp.einsum('bsgc,gcd->bsgd', d, w_group)
    return y.reshape(B, S, POOL_WIDTH) * scale


def spatial_gating(u, v, v_gain, w_s, b_s):
    B, S, _ = u.shape
    n_chunks = S // CHUNK
    v = rmsnorm(v, v_gain)
    vc = v.reshape(B, n_chunks, CHUNK, SGU_HEADS, SGU_HEAD_DIM)
    w = w_s * jnp.tril(jnp.ones((CHUNK, CHUNK), dtype=w_s.dtype))
    s = jnp.einsum('hts,bnshc->bnthc', w, vc) + b_s.T[None, None, :, :, None]
    return u * s.reshape(B, S, SGU_WIDTH)


def _fwd_setup_inputs(seed: int = 0) -> dict:
    key = jax.random.key(seed)
    ks = jax.random.split(key, 32)
    L, D = DEPTH, D_MODEL

    def dense(k, shape, fan_in):
        return jax.random.normal(k, shape, jnp.float32) * (fan_in ** -0.5)

    def gain(k, shape):
        return 1.0 + 0.05 * jax.random.normal(k, shape, jnp.float32)

    return {
        "x": jax.random.normal(ks[0], (BATCH, SEQ, D), jnp.float32),
        "g_ffn1_pre": gain(ks[1], (L, D)),
        "w_ffn1_up": dense(ks[2], (L, D, 2 * D_FF), D),
        "w_ffn1_down": dense(ks[3], (L, D_FF, D), D_FF),
        "g_ffn1_post": gain(ks[4], (L, D)),
        "g_mix_pre": gain(ks[5], (L, D)),
        "w_in": dense(ks[6], (L, D, IN_PROJ_WIDTH), D),
        "pool_group_w": dense(ks[7], (L, N_POOL_GROUPS, POOL_GROUP_DIM, POOL_GROUP_DIM), POOL_GROUP_DIM),
        "pool_scale": gain(ks[8], (L, POOL_WIDTH)),
        "w_pool_out": dense(ks[9], (L, POOL_WIDTH, D), POOL_WIDTH),
        "sgu_v_gain": gain(ks[10], (L, SGU_WIDTH)),
        "sgu_w_s": dense(ks[11], (L, SGU_HEADS, CHUNK, CHUNK), CHUNK),
        "sgu_b_s": gain(ks[12], (L, SGU_HEADS, CHUNK)),
        "w_sgu_out": dense(ks[13], (L, SGU_WIDTH, D), SGU_WIDTH),
        "w_out": dense(ks[14], (L, D, D), D),
        "g_mix_post": gain(ks[15], (L, D)),
        "g_ffn2_pre": gain(ks[16], (L, D)),
        "w_ffn2_up": dense(ks[17], (L, D, 2 * D_FF), D),
        "w_ffn2_down": dense(ks[18], (L, D_FF, D), D_FF),
        "g_ffn2_post": gain(ks[19], (L, D)),
    }


def _fwd_reference(x, g_ffn1_pre, w_ffn1_up, w_ffn1_down, g_ffn1_post, g_mix_pre, w_in,
              pool_group_w, pool_scale, w_pool_out, sgu_v_gain, sgu_w_s, sgu_b_s,
              w_sgu_out, w_out, g_mix_post, g_ffn2_pre, w_ffn2_up, w_ffn2_down, g_ffn2_post):
    splits = (POOL_WIDTH, POOL_WIDTH + SGU_WIDTH, POOL_WIDTH + 2 * SGU_WIDTH,
              POOL_WIDTH + 2 * SGU_WIDTH + D_MODEL)
    for i in range(DEPTH):
        f = swiglu(rmsnorm(x, g_ffn1_pre[i]), w_ffn1_up[i], w_ffn1_down[i])
        x = x + MACARON_WEIGHT * rmsnorm(f, g_ffn1_post[i])

        h = rmsnorm(x, g_mix_pre[i])
        p, u, v, ga, gb = jnp.split(h @ w_in[i], splits, axis=-1)
        y_a = pool_mixer(p, pool_group_w[i], pool_scale[i]) @ w_pool_out[i]
        y_b = spatial_gating(jax.nn.gelu(u), jax.nn.gelu(v), sgu_v_gain[i],
                             sgu_w_s[i], sgu_b_s[i]) @ w_sgu_out[i]
        m = jax.nn.sigmoid(ga) * y_a + jax.nn.sigmoid(gb) * y_b
        x = x + rmsnorm(m @ w_out[i], g_mix_post[i])

        f = swiglu(rmsnorm(x, g_ffn2_pre[i]), w_ffn2_up[i], w_ffn2_down[i])
        x = x + MACARON_WEIGHT * rmsnorm(f, g_ffn2_post[i])
    return x


import jax as _jax
import jax.numpy as _jnp

TWIN_FORMAT = 'train_step'
FWD_PARAMS = ['x', 'g_ffn1_pre', 'w_ffn1_up', 'w_ffn1_down', 'g_ffn1_post', 'g_mix_pre', 'w_in', 'pool_group_w', 'pool_scale', 'w_pool_out', 'sgu_v_gain', 'sgu_w_s', 'sgu_b_s', 'w_sgu_out', 'w_out', 'g_mix_post', 'g_ffn2_pre', 'w_ffn2_up', 'w_ffn2_down', 'g_ffn2_post']
TWIN_WEIGHTS = ['g_ffn1_pre', 'w_ffn1_up', 'w_ffn1_down', 'g_ffn1_post', 'g_mix_pre', 'w_in', 'pool_group_w', 'pool_scale', 'w_pool_out', 'sgu_v_gain', 'sgu_w_s', 'sgu_b_s', 'w_sgu_out', 'w_out', 'g_mix_post', 'g_ffn2_pre', 'w_ffn2_up', 'w_ffn2_down', 'g_ffn2_post']
TWIN_DIFF_INPUT = 'x'
TWIN_INPUTS = ['x', 'g_ffn1_pre', 'w_ffn1_up', 'w_ffn1_down', 'g_ffn1_post', 'g_mix_pre', 'w_in', 'pool_group_w', 'pool_scale', 'w_pool_out', 'sgu_v_gain', 'sgu_w_s', 'sgu_b_s', 'w_sgu_out', 'w_out', 'g_mix_post', 'g_ffn2_pre', 'w_ffn2_up', 'w_ffn2_down', 'g_ffn2_post', 'loss_target', 'm_g_ffn1_pre', 'm_w_ffn1_up', 'm_w_ffn1_down', 'm_g_ffn1_post', 'm_g_mix_pre', 'm_w_in', 'm_pool_group_w', 'm_pool_scale', 'm_w_pool_out', 'm_sgu_v_gain', 'm_sgu_w_s', 'm_sgu_b_s', 'm_w_sgu_out', 'm_w_out', 'm_g_mix_post', 'm_g_ffn2_pre', 'm_w_ffn2_up', 'm_w_ffn2_down', 'm_g_ffn2_post', 'v_g_ffn1_pre', 'v_w_ffn1_up', 'v_w_ffn1_down', 'v_g_ffn1_post', 'v_g_mix_pre', 'v_w_in', 'v_pool_group_w', 'v_pool_scale', 'v_w_pool_out', 'v_sgu_v_gain', 'v_sgu_w_s', 'v_sgu_b_s', 'v_w_sgu_out', 'v_w_out', 'v_g_mix_post', 'v_g_ffn2_pre', 'v_w_ffn2_up', 'v_w_ffn2_down', 'v_g_ffn2_post']
TWIN_OUTPUTS = ['loss', 'grad_x', 'grad_g_ffn1_pre', 'grad_w_ffn1_up', 'grad_w_ffn1_down', 'grad_g_ffn1_post', 'grad_g_mix_pre', 'grad_w_in', 'grad_pool_group_w', 'grad_pool_scale', 'grad_w_pool_out', 'grad_sgu_v_gain', 'grad_sgu_w_s', 'grad_sgu_b_s', 'grad_w_sgu_out', 'grad_w_out', 'grad_g_mix_post', 'grad_g_ffn2_pre', 'grad_w_ffn2_up', 'grad_w_ffn2_down', 'grad_g_ffn2_post', 'delta_g_ffn1_pre', 'delta_w_ffn1_up', 'delta_w_ffn1_down', 'delta_g_ffn1_post', 'delta_g_mix_pre', 'delta_w_in', 'delta_pool_group_w', 'delta_pool_scale', 'delta_w_pool_out', 'delta_sgu_v_gain', 'delta_sgu_w_s', 'delta_sgu_b_s', 'delta_w_sgu_out', 'delta_w_out', 'delta_g_mix_post', 'delta_g_ffn2_pre', 'delta_w_ffn2_up', 'delta_w_ffn2_down', 'delta_g_ffn2_post', 'new_m_g_ffn1_pre', 'new_m_w_ffn1_up', 'new_m_w_ffn1_down', 'new_m_g_ffn1_post', 'new_m_g_mix_pre', 'new_m_w_in', 'new_m_pool_group_w', 'new_m_pool_scale', 'new_m_w_pool_out', 'new_m_sgu_v_gain', 'new_m_sgu_w_s', 'new_m_sgu_b_s', 'new_m_w_sgu_out', 'new_m_w_out', 'new_m_g_mix_post', 'new_m_g_ffn2_pre', 'new_m_w_ffn2_up', 'new_m_w_ffn2_down', 'new_m_g_ffn2_post', 'new_v_g_ffn1_pre', 'new_v_w_ffn1_up', 'new_v_w_ffn1_down', 'new_v_g_ffn1_post', 'new_v_g_mix_pre', 'new_v_w_in', 'new_v_pool_group_w', 'new_v_pool_scale', 'new_v_w_pool_out', 'new_v_sgu_v_gain', 'new_v_sgu_w_s', 'new_v_sgu_b_s', 'new_v_w_sgu_out', 'new_v_w_out', 'new_v_g_mix_post', 'new_v_g_ffn2_pre', 'new_v_w_ffn2_up', 'new_v_w_ffn2_down', 'new_v_g_ffn2_post']
TWIN_LEAF_KINDS = {'loss': 'loss', 'grad_x': 'grad_x', 'grad_g_ffn1_pre': 'grad_w', 'grad_w_ffn1_up': 'grad_w', 'grad_w_ffn1_down': 'grad_w', 'grad_g_ffn1_post': 'grad_w', 'grad_g_mix_pre': 'grad_w', 'grad_w_in': 'grad_w', 'grad_pool_group_w': 'grad_w', 'grad_pool_scale': 'grad_w', 'grad_w_pool_out': 'grad_w', 'grad_sgu_v_gain': 'grad_w', 'grad_sgu_w_s': 'grad_w', 'grad_sgu_b_s': 'grad_w', 'grad_w_sgu_out': 'grad_w', 'grad_w_out': 'grad_w', 'grad_g_mix_post': 'grad_w', 'grad_g_ffn2_pre': 'grad_w', 'grad_w_ffn2_up': 'grad_w', 'grad_w_ffn2_down': 'grad_w', 'grad_g_ffn2_post': 'grad_w', 'delta_g_ffn1_pre': 'delta_w', 'delta_w_ffn1_up': 'delta_w', 'delta_w_ffn1_down': 'delta_w', 'delta_g_ffn1_post': 'delta_w', 'delta_g_mix_pre': 'delta_w', 'delta_w_in': 'delta_w', 'delta_pool_group_w': 'delta_w', 'delta_pool_scale': 'delta_w', 'delta_w_pool_out': 'delta_w', 'delta_sgu_v_gain': 'delta_w', 'delta_sgu_w_s': 'delta_w', 'delta_sgu_b_s': 'delta_w', 'delta_w_sgu_out': 'delta_w', 'delta_w_out': 'delta_w', 'delta_g_mix_post': 'delta_w', 'delta_g_ffn2_pre': 'delta_w', 'delta_w_ffn2_up': 'delta_w', 'delta_w_ffn2_down': 'delta_w', 'delta_g_ffn2_post': 'delta_w', 'new_m_g_ffn1_pre': 'new_m', 'new_m_w_ffn1_up': 'new_m', 'new_m_w_ffn1_down': 'new_m', 'new_m_g_ffn1_post': 'new_m', 'new_m_g_mix_pre': 'new_m', 'new_m_w_in': 'new_m', 'new_m_pool_group_w': 'new_m', 'new_m_pool_scale': 'new_m', 'new_m_w_pool_out': 'new_m', 'new_m_sgu_v_gain': 'new_m', 'new_m_sgu_w_s': 'new_m', 'new_m_sgu_b_s': 'new_m', 'new_m_w_sgu_out': 'new_m', 'new_m_w_out': 'new_m', 'new_m_g_mix_post': 'new_m', 'new_m_g_ffn2_pre': 'new_m', 'new_m_w_ffn2_up': 'new_m', 'new_m_w_ffn2_down': 'new_m', 'new_m_g_ffn2_post': 'new_m', 'new_v_g_ffn1_pre': 'new_v', 'new_v_w_ffn1_up': 'new_v', 'new_v_w_ffn1_down': 'new_v', 'new_v_g_ffn1_post': 'new_v', 'new_v_g_mix_pre': 'new_v', 'new_v_w_in': 'new_v', 'new_v_pool_group_w': 'new_v', 'new_v_pool_scale': 'new_v', 'new_v_w_pool_out': 'new_v', 'new_v_sgu_v_gain': 'new_v', 'new_v_sgu_w_s': 'new_v', 'new_v_sgu_b_s': 'new_v', 'new_v_w_sgu_out': 'new_v', 'new_v_w_out': 'new_v', 'new_v_g_mix_post': 'new_v', 'new_v_g_ffn2_pre': 'new_v', 'new_v_w_ffn2_up': 'new_v', 'new_v_w_ffn2_down': 'new_v', 'new_v_g_ffn2_post': 'new_v'}


def _forward(args):
    return _fwd_reference(*[args[k] for k in FWD_PARAMS])


def _output_shape():
    def fwd():
        inp = _fwd_setup_inputs(0)
        return _fwd_reference(*[inp[k] for k in FWD_PARAMS])
    out = _jax.eval_shape(fwd)
    return out.shape, out.dtype

N_MICROBATCH = 1
ADAM_LR = 0.001
ADAM_B1 = 0.9
ADAM_B2 = 0.999
ADAM_EPS = 1e-08
ADAM_WD = 0.01
ADAM_STEP = 10
PER_EXAMPLE_BATCH_AXIS = {'x': 0, 'loss_target': 0}
SHARED_INPUTS = []
_WEIGHT_DTYPES = {'g_ffn1_pre': _jnp.float32, 'w_ffn1_up': _jnp.float32, 'w_ffn1_down': _jnp.float32, 'g_ffn1_post': _jnp.float32, 'g_mix_pre': _jnp.float32, 'w_in': _jnp.float32, 'pool_group_w': _jnp.float32, 'pool_scale': _jnp.float32, 'w_pool_out': _jnp.float32, 'sgu_v_gain': _jnp.float32, 'sgu_w_s': _jnp.float32, 'sgu_b_s': _jnp.float32, 'w_sgu_out': _jnp.float32, 'w_out': _jnp.float32, 'g_mix_post': _jnp.float32, 'g_ffn2_pre': _jnp.float32, 'w_ffn2_up': _jnp.float32, 'w_ffn2_down': _jnp.float32, 'g_ffn2_post': _jnp.float32}
MOMENT_SCALE = {'g_ffn1_pre': 4.701017e-01, 'w_ffn1_up': 1.926269e-01, 'w_ffn1_down': 3.455249e-01, 'g_ffn1_post': 3.936478e+00, 'g_mix_pre': 7.774517e-01, 'w_in': 4.131424e-01, 'pool_group_w': 8.354734e-01, 'pool_scale': 8.493333e-01, 'w_pool_out': 6.121716e-01, 'sgu_v_gain': 2.255897e-01, 'sgu_w_s': 2.209092e-01, 'sgu_b_s': 3.353792e-01, 'w_sgu_out': 1.458939e+00, 'w_out': 1.560558e+00, 'g_mix_post': 1.612795e+01, 'g_ffn2_pre': 3.957072e-01, 'w_ffn2_up': 1.668895e-01, 'w_ffn2_down': 3.224065e-01, 'g_ffn2_post': 3.979748e+00}


def _to_microbatches(a, axis):
    t = _jnp.moveaxis(a, axis, 0)
    t = t.reshape((N_MICROBATCH, t.shape[0] // N_MICROBATCH) + t.shape[1:])
    return _jnp.moveaxis(t, 1, axis + 1)


def setup_inputs(seed: int = 0) -> dict:
    inp = _fwd_setup_inputs(seed)
    key = _jax.random.fold_in(_jax.random.key(seed), 7919)
    shape, _ = _output_shape()
    out = dict(inp)
    out["loss_target"] = _jax.random.normal(_jax.random.fold_in(key, 0), shape, _jnp.float32)
    for i, name in enumerate(TWIN_WEIGHTS):
        w = inp[name].astype(_jnp.float32)
        if MOMENT_SCALE is None:
            s = _jnp.sqrt(_jnp.mean(_jnp.square(w)) + 1e-30)
        else:
            s = MOMENT_SCALE[name]
        km, kv = _jax.random.split(_jax.random.fold_in(key, i + 1))
        out[name] = w
        out["m_" + name] = s * _jax.random.normal(km, w.shape, _jnp.float32)
        out["v_" + name] = (s * s) * _jax.random.uniform(kv, w.shape, _jnp.float32, 0.5, 1.5)
    if N_MICROBATCH > 1:
        for name, axis in PER_EXAMPLE_BATCH_AXIS.items():
            out[name] = _to_microbatches(out[name], axis)
    return {'x': out['x'], 'g_ffn1_pre': out['g_ffn1_pre'], 'w_ffn1_up': out['w_ffn1_up'], 'w_ffn1_down': out['w_ffn1_down'], 'g_ffn1_post': out['g_ffn1_post'], 'g_mix_pre': out['g_mix_pre'], 'w_in': out['w_in'], 'pool_group_w': out['pool_group_w'], 'pool_scale': out['pool_scale'], 'w_pool_out': out['w_pool_out'], 'sgu_v_gain': out['sgu_v_gain'], 'sgu_w_s': out['sgu_w_s'], 'sgu_b_s': out['sgu_b_s'], 'w_sgu_out': out['w_sgu_out'], 'w_out': out['w_out'], 'g_mix_post': out['g_mix_post'], 'g_ffn2_pre': out['g_ffn2_pre'], 'w_ffn2_up': out['w_ffn2_up'], 'w_ffn2_down': out['w_ffn2_down'], 'g_ffn2_post': out['g_ffn2_post'], 'loss_target': out['loss_target'], 'm_g_ffn1_pre': out['m_g_ffn1_pre'], 'm_w_ffn1_up': out['m_w_ffn1_up'], 'm_w_ffn1_down': out['m_w_ffn1_down'], 'm_g_ffn1_post': out['m_g_ffn1_post'], 'm_g_mix_pre': out['m_g_mix_pre'], 'm_w_in': out['m_w_in'], 'm_pool_group_w': out['m_pool_group_w'], 'm_pool_scale': out['m_pool_scale'], 'm_w_pool_out': out['m_w_pool_out'], 'm_sgu_v_gain': out['m_sgu_v_gain'], 'm_sgu_w_s': out['m_sgu_w_s'], 'm_sgu_b_s': out['m_sgu_b_s'], 'm_w_sgu_out': out['m_w_sgu_out'], 'm_w_out': out['m_w_out'], 'm_g_mix_post': out['m_g_mix_post'], 'm_g_ffn2_pre': out['m_g_ffn2_pre'], 'm_w_ffn2_up': out['m_w_ffn2_up'], 'm_w_ffn2_down': out['m_w_ffn2_down'], 'm_g_ffn2_post': out['m_g_ffn2_post'], 'v_g_ffn1_pre': out['v_g_ffn1_pre'], 'v_w_ffn1_up': out['v_w_ffn1_up'], 'v_w_ffn1_down': out['v_w_ffn1_down'], 'v_g_ffn1_post': out['v_g_ffn1_post'], 'v_g_mix_pre': out['v_g_mix_pre'], 'v_w_in': out['v_w_in'], 'v_pool_group_w': out['v_pool_group_w'], 'v_pool_scale': out['v_pool_scale'], 'v_w_pool_out': out['v_w_pool_out'], 'v_sgu_v_gain': out['v_sgu_v_gain'], 'v_sgu_w_s': out['v_sgu_w_s'], 'v_sgu_b_s': out['v_sgu_b_s'], 'v_w_sgu_out': out['v_w_sgu_out'], 'v_w_out': out['v_w_out'], 'v_g_mix_post': out['v_g_mix_post'], 'v_g_ffn2_pre': out['v_g_ffn2_pre'], 'v_w_ffn2_up': out['v_w_ffn2_up'], 'v_w_ffn2_down': out['v_w_ffn2_down'], 'v_g_ffn2_post': out['v_g_ffn2_post']}


def _loss(weights, diff, rest, loss_target):
    with _jax.named_scope("forward"):
        args = {**rest, TWIN_DIFF_INPUT: diff, **{k: w.astype(_WEIGHT_DTYPES[k]) for k, w in weights.items()}}
        y = _forward(args)
    with _jax.named_scope("loss_head"):
        err = _jnp.square(y.astype(_jnp.float32) - loss_target)
        return 0.5 * _jnp.sum(_jnp.mean(err, axis=-1)) if err.ndim else 0.5 * err


def _adamw(w, g, m, v):
    m = ADAM_B1 * m + (1.0 - ADAM_B1) * g
    v = ADAM_B2 * v + (1.0 - ADAM_B2) * _jnp.square(g)
    m_hat = m / (1.0 - ADAM_B1 ** ADAM_STEP)
    v_hat = v / (1.0 - ADAM_B2 ** ADAM_STEP)
    delta = -ADAM_LR * (m_hat / (_jnp.sqrt(v_hat) + ADAM_EPS) + ADAM_WD * w)
    return delta, m, v


def reference(x, g_ffn1_pre, w_ffn1_up, w_ffn1_down, g_ffn1_post, g_mix_pre, w_in, pool_group_w, pool_scale, w_pool_out, sgu_v_gain, sgu_w_s, sgu_b_s, w_sgu_out, w_out, g_mix_post, g_ffn2_pre, w_ffn2_up, w_ffn2_down, g_ffn2_post, loss_target, m_g_ffn1_pre, m_w_ffn1_up, m_w_ffn1_down, m_g_ffn1_post, m_g_mix_pre, m_w_in, m_pool_group_w, m_pool_scale, m_w_pool_out, m_sgu_v_gain, m_sgu_w_s, m_sgu_b_s, m_w_sgu_out, m_w_out, m_g_mix_post, m_g_ffn2_pre, m_w_ffn2_up, m_w_ffn2_down, m_g_ffn2_post, v_g_ffn1_pre, v_w_ffn1_up, v_w_ffn1_down, v_g_ffn1_post, v_g_mix_pre, v_w_in, v_pool_group_w, v_pool_scale, v_w_pool_out, v_sgu_v_gain, v_sgu_w_s, v_sgu_b_s, v_w_sgu_out, v_w_out, v_g_mix_post, v_g_ffn2_pre, v_w_ffn2_up, v_w_ffn2_down, v_g_ffn2_post):
    given = dict(x=x, g_ffn1_pre=g_ffn1_pre, w_ffn1_up=w_ffn1_up, w_ffn1_down=w_ffn1_down, g_ffn1_post=g_ffn1_post, g_mix_pre=g_mix_pre, w_in=w_in, pool_group_w=pool_group_w, pool_scale=pool_scale, w_pool_out=w_pool_out, sgu_v_gain=sgu_v_gain, sgu_w_s=sgu_w_s, sgu_b_s=sgu_b_s, w_sgu_out=w_sgu_out, w_out=w_out, g_mix_post=g_mix_post, g_ffn2_pre=g_ffn2_pre, w_ffn2_up=w_ffn2_up, w_ffn2_down=w_ffn2_down, g_ffn2_post=g_ffn2_post, loss_target=loss_target, m_g_ffn1_pre=m_g_ffn1_pre, m_w_ffn1_up=m_w_ffn1_up, m_w_ffn1_down=m_w_ffn1_down, m_g_ffn1_post=m_g_ffn1_post, m_g_mix_pre=m_g_mix_pre, m_w_in=m_w_in, m_pool_group_w=m_pool_group_w, m_pool_scale=m_pool_scale, m_w_pool_out=m_w_pool_out, m_sgu_v_gain=m_sgu_v_gain, m_sgu_w_s=m_sgu_w_s, m_sgu_b_s=m_sgu_b_s, m_w_sgu_out=m_w_sgu_out, m_w_out=m_w_out, m_g_mix_post=m_g_mix_post, m_g_ffn2_pre=m_g_ffn2_pre, m_w_ffn2_up=m_w_ffn2_up, m_w_ffn2_down=m_w_ffn2_down, m_g_ffn2_post=m_g_ffn2_post, v_g_ffn1_pre=v_g_ffn1_pre, v_w_ffn1_up=v_w_ffn1_up, v_w_ffn1_down=v_w_ffn1_down, v_g_ffn1_post=v_g_ffn1_post, v_g_mix_pre=v_g_mix_pre, v_w_in=v_w_in, v_pool_group_w=v_pool_group_w, v_pool_scale=v_pool_scale, v_w_pool_out=v_w_pool_out, v_sgu_v_gain=v_sgu_v_gain, v_sgu_w_s=v_sgu_w_s, v_sgu_b_s=v_sgu_b_s, v_w_sgu_out=v_w_sgu_out, v_w_out=v_w_out, v_g_mix_post=v_g_mix_post, v_g_ffn2_pre=v_g_ffn2_pre, v_w_ffn2_up=v_w_ffn2_up, v_w_ffn2_down=v_w_ffn2_down, v_g_ffn2_post=v_g_ffn2_post)
    weights = {n: given[n] for n in TWIN_WEIGHTS}
    shared = {n: given[n] for n in SHARED_INPUTS}
    per_example = {n: given[n] for n in ['x']}
    grad_fn = _jax.value_and_grad(_loss, argnums=(0, 1))

    def one_microbatch(ex, loss_target):
        ex = dict(ex)
        diff = ex.pop(TWIN_DIFF_INPUT)
        return grad_fn(weights, diff, {**shared, **ex}, loss_target)

    if N_MICROBATCH == 1:
        loss, (grad_w, grad_x) = one_microbatch(per_example, given["loss_target"])
    else:
        def body(carry, xs):
            loss_sum, grad_sum = carry
            l_k, (gw_k, gx_k) = one_microbatch(xs[0], xs[1])
            with _jax.named_scope("update"):
                return (loss_sum + l_k, _jax.tree.map(_jnp.add, grad_sum, gw_k)), gx_k

        init = (_jnp.zeros((), _jnp.float32), _jax.tree.map(_jnp.zeros_like, weights))
        (loss, grad_w), grad_x = _jax.lax.scan(body, init, (per_example, given["loss_target"]))
    with _jax.named_scope("update"):
        delta_w, new_m, new_v = {}, {}, {}
        for n in TWIN_WEIGHTS:
            delta_w[n], new_m[n], new_v[n] = _adamw(weights[n], grad_w[n], given["m_" + n], given["v_" + n])
    return (loss, grad_x, *[grad_w[n] for n in TWIN_WEIGHTS], *[delta_w[n] for n in TWIN_WEIGHTS],
            *[new_m[n] for n in TWIN_WEIGHTS], *[new_v[n] for n in TWIN_WEIGHTS])
```

```python
import functools

import jax
import jax.numpy as jnp
from jax import lax
from jax.experimental import pallas as pl
from jax.experimental.pallas import tpu as pltpu

F32 = jnp.float32
BF16 = jnp.bfloat16

N_DEV = 8
MESH_AXES = ("x", "y", "c")
N_PEERS = N_DEV - 1

EPS = 1e-6
MACARON_WEIGHT = 0.5
N_POOL_GROUPS = 4
POOL_HALO = 16
SGU_HEAD = 128
SGU_CHUNK = 128
GELU_C = 0.7978845608028654
GELU_A = 0.044715

ADAM_LR = 0.001
ADAM_B1 = 0.9
ADAM_B2 = 0.999
ADAM_EPS = 1e-08
ADAM_WD = 0.01
ADAM_STEP = 10

ROW_TILE = 512
VMEM_LIMIT_BYTES = 56 * 1024 * 1024

BIG = ("w_ffn1_up", "w_ffn1_down", "w_in", "pool_group_w", "w_pool_out", "w_sgu_out", "w_out",
       "w_ffn2_up", "w_ffn2_down")
SMALL = ("g_ffn1_pre", "g_ffn1_post", "g_mix_pre", "pool_scale", "sgu_v_gain", "sgu_w_s", "sgu_b_s",
         "g_mix_post", "g_ffn2_pre", "g_ffn2_post")
ORDER = ("g_ffn1_pre", "w_ffn1_up", "w_ffn1_down", "g_ffn1_post", "g_mix_pre", "w_in", "pool_group_w",
         "pool_scale", "w_pool_out", "sgu_v_gain", "sgu_w_s", "sgu_b_s", "w_sgu_out", "w_out",
         "g_mix_post", "g_ffn2_pre", "w_ffn2_up", "w_ffn2_down", "g_ffn2_post")


def _call(body, *, name, out_shape, grid=(), in_specs=None, out_specs=None, scratch_shapes=(),
          aliases=None):
    extra = {}
    if grid:
        extra["grid"] = grid
    if aliases:
        extra["input_output_aliases"] = aliases
    return pl.pallas_call(
        body, name=name, out_shape=out_shape, in_specs=in_specs, out_specs=out_specs,
        scratch_shapes=list(scratch_shapes),
        compiler_params=pltpu.CompilerParams(vmem_limit_bytes=VMEM_LIMIT_BYTES), **extra)


def _sds(shape, dtype):
    return jax.ShapeDtypeStruct(tuple(shape), dtype)


def _dot(a, b):
    return jnp.dot(a, b, preferred_element_type=F32)


def _dot_nt(a, b):
    return lax.dot_general(a, b, (((1,), (1,)), ((), ())), preferred_element_type=F32)


def _dot_tn(a, b):
    return lax.dot_general(a, b, (((0,), (0,)), ((), ())), preferred_element_type=F32)


def _sigmoid(x):
    return 1.0 / (1.0 + jnp.exp(-x))


def _gelu(x):
    return 0.5 * x * (1.0 + jnp.tanh(GELU_C * (x + GELU_A * x * x * x)))


def _gelu_grad(x):
    t = jnp.tanh(GELU_C * (x + GELU_A * x * x * x))
    return 0.5 * (1.0 + t) + 0.5 * x * (1.0 - t * t) * (GELU_C * (1.0 + 3.0 * GELU_A * x * x))


def _rstd(x):
    return lax.rsqrt(jnp.mean(x * x, axis=-1, keepdims=True) + EPS)


def _row_tile(t):
    return min(t, ROW_TILE)


def _slab_rows(r):
    for cand in (256, 128, 64, 32, 16):
        if r % cand == 0:
            return cand
    return r


def _rms_fwd_body(x_ref, g_ref, h_ref):
    x = x_ref[...]
    h_ref[...] = ((x * _rstd(x)) * g_ref[...]).astype(h_ref.dtype)


def _rms_fwd(x, g):
    t, d = x.shape
    tm = _row_tile(t)
    return _call(
        functools.partial(_rms_fwd_body), name="rms_fwd", out_shape=_sds((t, d), BF16),
        grid=(t // tm,),
        in_specs=[pl.BlockSpec((tm, d), lambda i: (i, 0)), pl.BlockSpec((1, d), lambda i: (0, 0))],
        out_specs=pl.BlockSpec((tm, d), lambda i: (i, 0)))(x, g)


def _mm_cs_body(h_ref, w_ref, o_ref):
    o_ref[...] = _dot(h_ref[...], w_ref[...]).astype(o_ref.dtype)


def _mm_cs(h, wg):
    t, k = h.shape
    n = wg.shape[2]
    tm = _row_tile(t)
    return _call(
        functools.partial(_mm_cs_body), name="mm_cs", out_shape=_sds((t, N_DEV * n), BF16),
        grid=(N_DEV, t // tm),
        in_specs=[pl.BlockSpec((tm, k), lambda j, i: (i, 0)),
                  pl.BlockSpec((None, k, n), lambda j, i: (j, 0, 0))],
        out_specs=pl.BlockSpec((tm, n), lambda j, i: (i, j)))(h, wg)


def _ffn_up_body(h_ref, wg_ref, wu_ref, gu_ref, a_ref):
    h = h_ref[...]
    g = _dot(h, wg_ref[...])
    u = _dot(h, wu_ref[...])
    gu_ref[0] = g.astype(gu_ref.dtype)
    gu_ref[1] = u.astype(gu_ref.dtype)
    a_ref[...] = (g * _sigmoid(g) * u).astype(a_ref.dtype)


def _ffn_up(h, wg):
    t, k = h.shape
    n = wg.shape[2]
    half = N_DEV // 2
    f = half * n
    tm = _row_tile(t)
    return _call(
        functools.partial(_ffn_up_body), name="ffn_up",
        out_shape=(_sds((2, t, f), BF16), _sds((t, f), BF16)),
        grid=(half, t // tm),
        in_specs=[pl.BlockSpec((tm, k), lambda j, i: (i, 0)),
                  pl.BlockSpec((None, k, n), lambda j, i: (j, 0, 0)),
                  pl.BlockSpec((None, k, n), lambda j, i: (j + N_DEV // 2, 0, 0))],
        out_specs=(pl.BlockSpec((2, tm, n), lambda j, i: (0, i, j)),
                   pl.BlockSpec((tm, n), lambda j, i: (i, j))))(h, wg, wg)


def _down_body(a_ref, w_ref, x_ref, g_ref, xo_ref, f_ref, acc_ref, *, scale, nk):
    k = pl.program_id(1)

    @pl.when(k == 0)
    def _():
        acc_ref[...] = jnp.zeros_like(acc_ref)

    acc_ref[...] += _dot(a_ref[...], w_ref[...])

    @pl.when(k == nk - 1)
    def _():
        f = acc_ref[...]
        f_ref[...] = f.astype(f_ref.dtype)
        xo_ref[...] = x_ref[...] + scale * ((f * _rstd(f)) * g_ref[...])


def _down_norm_res(a, w, x, g, scale):
    t, kk = a.shape
    d = w.shape[1]
    tm = _row_tile(t)
    tk = 512 if kk % 512 == 0 else 256
    nk = kk // tk
    return _call(
        functools.partial(_down_body, scale=scale, nk=nk), name="down_norm_res",
        out_shape=(_sds((t, d), F32), _sds((t, d), BF16)),
        grid=(t // tm, nk),
        in_specs=[pl.BlockSpec((tm, tk), lambda i, k: (i, k)),
                  pl.BlockSpec((tk, d), lambda i, k: (k, 0)),
                  pl.BlockSpec((tm, d), lambda i, k: (i, 0)),
                  pl.BlockSpec((1, d), lambda i, k: (0, 0))],
        out_specs=(pl.BlockSpec((tm, d), lambda i, k: (i, 0)),
                   pl.BlockSpec((tm, d), lambda i, k: (i, 0))),
        scratch_shapes=[pltpu.VMEM((tm, d), F32)])(a, w, x, g)


def _pool_inv_count(g, i, tm):
    t1 = i * tm + lax.broadcasted_iota(jnp.int32, (tm, 1), 0) + 1
    return 1.0 / jnp.minimum(t1, 2 << g).astype(F32)


def _pool_windows(e, g):
    s2 = e + pltpu.roll(e, 1, 0)
    s4 = s2 + pltpu.roll(s2, 2, 0)
    s8 = s4 + pltpu.roll(s4, 4, 0)
    s16 = s8 + pltpu.roll(s8, 8, 0)
    return jnp.where(g == 0, s2, jnp.where(g == 1, s4, jnp.where(g == 2, s8, s16)))


def _pool_delta(p_ref, ph_ref, g, i, tm):
    pm = p_ref[...].astype(F32)
    ph = jnp.where(i == 0, 0.0, ph_ref[...].astype(F32))
    e = jnp.concatenate([ph, pm], axis=0)
    s = _pool_windows(e, g)[POOL_HALO:]
    return s * _pool_inv_count(g, i, tm) - pm


def _pool_fwd_body(p_ref, ph_ref, w_ref, sc_ref, o_ref, *, tm):
    g = pl.program_id(0)
    i = pl.program_id(1)
    d = _pool_delta(p_ref, ph_ref, g, i, tm)
    o_ref[...] = (_dot(d.astype(BF16), w_ref[...]) * sc_ref[...]).astype(o_ref.dtype)


def _pool_specs(t, c, tm):
    hb = tm // POOL_HALO
    main = pl.BlockSpec((tm, c), lambda g, i: (i, g))
    halo_before = pl.BlockSpec((POOL_HALO, c), lambda g, i: (jnp.maximum(i * hb - 1, 0), g))
    return main, halo_before


def _pool_fwd(proj, wgrp, scale):
    t = proj.shape[0]
    c = wgrp.shape[1]
    tm = _row_tile(t)
    main, halo = _pool_specs(t, c, tm)
    return _call(
        functools.partial(_pool_fwd_body, tm=tm), name="pool_fwd",
        out_shape=_sds((t, N_POOL_GROUPS * c), BF16), grid=(N_POOL_GROUPS, t // tm),
        in_specs=[main, halo, pl.BlockSpec((None, c, c), lambda g, i: (g, 0, 0)),
                  pl.BlockSpec((1, c), lambda g, i: (0, g))],
        out_specs=pl.BlockSpec((tm, c), lambda g, i: (i, g)))(proj, proj, wgrp, scale)


def _tril_mask():
    r = lax.broadcasted_iota(jnp.int32, (SGU_CHUNK, SGU_CHUNK), 0)
    c = lax.broadcasted_iota(jnp.int32, (SGU_CHUNK, SGU_CHUNK), 1)
    return r >= c


def _tril_weights(ws_ref, h):
    return jnp.where(_tril_mask(), ws_ref[h], 0.0)


def _sgu_fwd_body(u_ref, v_ref, gain_ref, ws_ref, b_ref, q_ref, *, tm, heads):
    ug = _gelu(u_ref[...].astype(F32))
    vg = _gelu(v_ref[...].astype(F32))
    vn = ((vg * _rstd(vg)) * gain_ref[...]).astype(BF16)
    for h in range(heads):
        wt = _tril_weights(ws_ref, h).astype(BF16)
        cols = slice(h * SGU_HEAD, (h + 1) * SGU_HEAD)
        for c in range(tm // SGU_CHUNK):
            rows = slice(c * SGU_CHUNK, (c + 1) * SGU_CHUNK)
            sg = _dot(wt, vn[rows, cols]) + b_ref[h]
            q_ref[rows, cols] = (ug[rows, cols] * sg).astype(q_ref.dtype)


def _sgu_fwd(proj, pw, gain, ws, b3):
    t = proj.shape[0]
    sw = gain.shape[1]
    heads = sw // SGU_HEAD
    tm = _row_tile(t)
    ub = pw // sw
    return _call(
        functools.partial(_sgu_fwd_body, tm=tm, heads=heads), name="sgu_fwd",
        out_shape=_sds((t, sw), BF16), grid=(t // tm,),
        in_specs=[pl.BlockSpec((tm, sw), lambda i: (i, ub)),
                  pl.BlockSpec((tm, sw), lambda i: (i, ub + 1)),
                  pl.BlockSpec((1, sw), lambda i: (0, 0)),
                  pl.BlockSpec((heads, SGU_CHUNK, SGU_CHUNK), lambda i: (0, 0, 0)),
                  pl.BlockSpec((heads, SGU_CHUNK, 1), lambda i: (0, 0, 0))],
        out_specs=pl.BlockSpec((tm, sw), lambda i: (i, 0)))(proj, proj, gain, ws, b3)


def _mix_gate_body(yps_ref, q_ref, ga_ref, gb_ref, wp_ref, ws_ref, m_ref, ya_ref, yb_ref):
    ya = _dot(yps_ref[...], wp_ref[...])
    yb = _dot(q_ref[...], ws_ref[...])
    ya_ref[...] = ya.astype(ya_ref.dtype)
    yb_ref[...] = yb.astype(yb_ref.dtype)
    m = _sigmoid(ga_ref[...].astype(F32)) * ya + _sigmoid(gb_ref[...].astype(F32)) * yb
    m_ref[...] = m.astype(m_ref.dtype)


def _mix_gate(yps, q, proj, gate_off, wpo, wso):
    t, pw = yps.shape
    sw = q.shape[1]
    n = wpo.shape[2]
    d = N_DEV * n
    tm = _row_tile(t)
    ga0 = gate_off // n
    gb0 = (gate_off + d) // n
    out = _sds((t, d), BF16)
    blk = pl.BlockSpec((tm, n), lambda j, i: (i, j))
    return _call(
        functools.partial(_mix_gate_body), name="mix_gate", out_shape=(out, out, out),
        grid=(N_DEV, t // tm),
        in_specs=[pl.BlockSpec((tm, pw), lambda j, i: (i, 0)),
                  pl.BlockSpec((tm, sw), lambda j, i: (i, 0)),
                  pl.BlockSpec((tm, n), lambda j, i: (i, ga0 + j)),
                  pl.BlockSpec((tm, n), lambda j, i: (i, gb0 + j)),
                  pl.BlockSpec((None, pw, n), lambda j, i: (j, 0, 0)),
                  pl.BlockSpec((None, sw, n), lambda j, i: (j, 0, 0))],
        out_specs=(blk, blk, blk))(yps, q, proj, proj, wpo, wso)


def _loss_body(y_ref, t_ref, dy_ref, l_ref, *, inv_d):
    i = pl.program_id(0)

    @pl.when(i == 0)
    def _():
        l_ref[...] = jnp.zeros_like(l_ref)

    e = y_ref[...] - t_ref[...]
    dy_ref[...] = e * inv_d
    l_ref[...] += jnp.sum(e * e)


def _loss_grad(y, target):
    t, d = y.shape
    tm = _row_tile(t)
    blk = pl.BlockSpec((tm, d), lambda i: (i, 0))
    return _call(
        functools.partial(_loss_body, inv_d=1.0 / d), name="loss_grad",
        out_shape=(_sds((t, d), F32), _sds((8, 128), F32)), grid=(t // tm,),
        in_specs=[blk, blk],
        out_specs=(blk, pl.BlockSpec((8, 128), lambda i: (0, 0))))(y, target)


def _norm_bwd_body(dy_ref, f_ref, g_ref, df_ref, dg_ref, *, scale):
    i = pl.program_id(0)

    @pl.when(i == 0)
    def _():
        dg_ref[...] = jnp.zeros_like(dg_ref)

    f = f_ref[...].astype(F32)
    rstd = _rstd(f)
    fn = f * rstd
    dr = scale * dy_ref[...]
    dg_ref[...] += jnp.sum(dr * fn, axis=0, keepdims=True)
    dfn = dr * g_ref[...]
    df = rstd * (dfn - fn * jnp.mean(dfn * fn, axis=-1, keepdims=True))
    df_ref[...] = df.astype(df_ref.dtype)


def _norm_res_bwd(dy, f, g, scale):
    t, d = dy.shape
    tm = _row_tile(t)
    blk = pl.BlockSpec((tm, d), lambda i: (i, 0))
    vec = pl.BlockSpec((1, d), lambda i: (0, 0))
    return _call(
        functools.partial(_norm_bwd_body, scale=scale), name="norm_res_bwd",
        out_shape=(_sds((t, d), BF16), _sds((1, d), F32)), grid=(t // tm,),
        in_specs=[blk, blk, vec], out_specs=(blk, vec))(dy, f, g)


def _dgrad_ffn_body(df_ref, w_ref, gu_ref, dz_ref):
    da = _dot_nt(df_ref[...], w_ref[...])
    g = gu_ref[0].astype(F32)
    u = gu_ref[1].astype(F32)
    sig = _sigmoid(g)
    dz_ref[0] = (da * u * (sig * (1.0 + g * (1.0 - sig)))).astype(dz_ref.dtype)
    dz_ref[1] = (da * (g * sig)).astype(dz_ref.dtype)


def _dgrad_ffn(df, wd, gu):
    t, d = df.shape
    f = wd.shape[0]
    tm = _row_tile(t)
    tn = 512 if f % 512 == 0 else 256
    blk = pl.BlockSpec((2, tm, tn), lambda j, i: (0, i, j))
    return _call(
        functools.partial(_dgrad_ffn_body), name="dgrad_ffn", out_shape=_sds((2, t, f), BF16),
        grid=(f // tn, t // tm),
        in_specs=[pl.BlockSpec((tm, d), lambda j, i: (i, 0)),
                  pl.BlockSpec((tn, d), lambda j, i: (j, 0)), blk],
        out_specs=blk)(df, wd, gu)


def _dgrad_gate_body(do_ref, w_ref, ya_ref, yb_ref, ga_ref, gb_ref, dya_ref, dyb_ref, dga_ref, dgb_ref):
    dm = _dot_nt(do_ref[...], w_ref[...])
    sa = _sigmoid(ga_ref[...].astype(F32))
    sb = _sigmoid(gb_ref[...].astype(F32))
    dya_ref[...] = (dm * sa).astype(dya_ref.dtype)
    dyb_ref[...] = (dm * sb).astype(dyb_ref.dtype)
    dga_ref[...] = (dm * ya_ref[...].astype(F32) * (sa * (1.0 - sa))).astype(dga_ref.dtype)
    dgb_ref[...] = (dm * yb_ref[...].astype(F32) * (sb * (1.0 - sb))).astype(dgb_ref.dtype)


def _dgrad_gate(do, wout, ya, yb, proj, gate_off):
    t, d = do.shape
    tm = _row_tile(t)
    tn = 512 if d % 512 == 0 else 256
    ga0 = gate_off // tn
    gb0 = (gate_off + d) // tn
    blk = pl.BlockSpec((tm, tn), lambda j, i: (i, j))
    out = _sds((t, d), BF16)
    return _call(
        functools.partial(_dgrad_gate_body), name="dgrad_gate", out_shape=(out, out, out, out),
        grid=(d // tn, t // tm),
        in_specs=[pl.BlockSpec((tm, d), lambda j, i: (i, 0)),
                  pl.BlockSpec((tn, d), lambda j, i: (j, 0)), blk, blk,
                  pl.BlockSpec((tm, tn), lambda j, i: (i, ga0 + j)),
                  pl.BlockSpec((tm, tn), lambda j, i: (i, gb0 + j))],
        out_specs=(blk, blk, blk, blk))(do, wout, ya, yb, proj, proj)


def _wgrad_rows_body(a_ref, b_ref, o_ref, acc_ref, *, nt):
    t = pl.program_id(1)

    @pl.when(t == 0)
    def _():
        acc_ref[...] = jnp.zeros_like(acc_ref)

    acc_ref[...] += _dot_tn(a_ref[...], b_ref[...])

    @pl.when(t == nt - 1)
    def _():
        o_ref[...] = acc_ref[...].astype(o_ref.dtype)


def _wgrad_rows(a, b):
    t, k = a.shape
    n = b.shape[1]
    tt = _row_tile(t)
    tk = k // 4
    nt = t // tt
    return _call(
        functools.partial(_wgrad_rows_body, nt=nt), name="wgrad_rows", out_shape=_sds((k, n), BF16),
        grid=(k // tk, nt),
        in_specs=[pl.BlockSpec((tt, tk), lambda r, s: (s, r)),
                  pl.BlockSpec((tt, n), lambda r, s: (s, 0))],
        out_specs=pl.BlockSpec((tk, n), lambda r, s: (r, 0)),
        scratch_shapes=[pltpu.VMEM((tk, n), F32)])(a, b)


def _wgrad_cs(a, b3):
    t, k = a.shape
    parts = b3.shape[0]
    per = N_DEV // parts
    n = b3.shape[2] // per
    tt = _row_tile(t)
    nt = t // tt
    return _call(
        functools.partial(_wgrad_rows_body, nt=nt), name="wgrad_cs", out_shape=_sds((N_DEV, k, n), BF16),
        grid=(N_DEV, nt),
        in_specs=[pl.BlockSpec((tt, k), lambda r, s: (s, 0)),
                  pl.BlockSpec((None, tt, n), lambda r, s: (r // per, s, r % per))],
        out_specs=pl.BlockSpec((None, k, n), lambda r, s: (r, 0, 0)),
        scratch_shapes=[pltpu.VMEM((k, n), F32)])(a, b3)


def _dgrad_cs_body(dz_ref, w_ref, o_ref, acc_ref):
    s = pl.program_id(1)

    @pl.when(s == 0)
    def _():
        acc_ref[...] = jnp.zeros_like(acc_ref)

    acc_ref[...] += _dot_nt(dz_ref[...], w_ref[...])

    @pl.when(s == N_DEV - 1)
    def _():
        o_ref[...] = acc_ref[...].astype(o_ref.dtype)


def _dgrad_cs(dz3, wg):
    parts, t, _ = dz3.shape
    per = N_DEV // parts
    k, n = wg.shape[1], wg.shape[2]
    tm = _row_tile(t)
    return _call(
        functools.partial(_dgrad_cs_body), name="dgrad_cs", out_shape=_sds((t, k), BF16),
        grid=(t // tm, N_DEV),
        in_specs=[pl.BlockSpec((None, tm, n), lambda i, s: (s // per, i, s % per)),
                  pl.BlockSpec((None, k, n), lambda i, s: (s, 0, 0))],
        out_specs=pl.BlockSpec((tm, k), lambda i, s: (i, 0)),
        scratch_shapes=[pltpu.VMEM((tm, k), F32)])(dz3, wg)


def _dgrad_cs_norm_body(dz_ref, w_ref, x_ref, g_ref, dy_ref, o_ref, dg_ref, acc_ref):
    i = pl.program_id(0)
    s = pl.program_id(1)

    @pl.when(jnp.logical_and(i == 0, s == 0))
    def _():
        dg_ref[...] = jnp.zeros_like(dg_ref)

    @pl.when(s == 0)
    def _():
        acc_ref[...] = jnp.zeros_like(acc_ref)

    acc_ref[...] += _dot_nt(dz_ref[...], w_ref[...])

    @pl.when(s == N_DEV - 1)
    def _():
        dh = acc_ref[...]
        x = x_ref[...]
        rstd = _rstd(x)
        xn = x * rstd
        dg_ref[...] += jnp.sum(dh * xn, axis=0, keepdims=True)
        dxn = dh * g_ref[...]
        o_ref[...] = dy_ref[...] + rstd * (dxn - xn * jnp.mean(dxn * xn, axis=-1, keepdims=True))


def _dgrad_cs_norm(dz3, wg, x, g, dy):
    parts, t, _ = dz3.shape
    per = N_DEV // parts
    d, n = wg.shape[1], wg.shape[2]
    tm = _row_tile(t)
    blk = pl.BlockSpec((tm, d), lambda i, s: (i, 0))
    vec = pl.BlockSpec((1, d), lambda i, s: (0, 0))
    return _call(
        functools.partial(_dgrad_cs_norm_body), name="dgrad_cs_norm",
        out_shape=(_sds((t, d), F32), _sds((1, d), F32)), grid=(t // tm, N_DEV),
        in_specs=[pl.BlockSpec((None, tm, n), lambda i, s: (s // per, i, s % per)),
                  pl.BlockSpec((None, d, n), lambda i, s: (s, 0, 0)), blk, vec, blk],
        out_specs=(blk, vec),
        scratch_shapes=[pltpu.VMEM((tm, d), F32)])(dz3, wg, x, g, dy)


def _pool_bwd_body(p_ref, ph_ref, dy_ref, dyh_ref, w_ref, sc_ref, dp_ref, dw_ref, dsc_ref, *, tm, nt):
    g = pl.program_id(0)
    i = pl.program_id(1)

    @pl.when(i == 0)
    def _():
        dw_ref[...] = jnp.zeros_like(dw_ref)
        dsc_ref[...] = jnp.zeros_like(dsc_ref)

    w = w_ref[...]
    sc = sc_ref[...]
    d = _pool_delta(p_ref, ph_ref, g, i, tm).astype(BF16)
    dyps = dy_ref[...].astype(F32)
    dsc_ref[...] += jnp.sum(dyps * _dot(d, w), axis=0, keepdims=True)
    dyp = (dyps * sc).astype(BF16)
    dw_ref[...] += _dot_tn(d, dyp)
    dyp_after = (jnp.where(i == nt - 1, 0.0, dyh_ref[...].astype(F32)) * sc).astype(BF16)
    dd = jnp.concatenate([_dot_nt(dyp, w), _dot_nt(dyp_after, w)], axis=0)
    rows = tm + POOL_HALO
    t1 = i * tm + lax.broadcasted_iota(jnp.int32, (rows, 1), 0) + 1
    e = dd / jnp.minimum(t1, 2 << g).astype(F32)
    r2 = e + pltpu.roll(e, rows - 1, 0)
    r4 = r2 + pltpu.roll(r2, rows - 2, 0)
    r8 = r4 + pltpu.roll(r4, rows - 4, 0)
    r16 = r8 + pltpu.roll(r8, rows - 8, 0)
    r = jnp.where(g == 0, r2, jnp.where(g == 1, r4, jnp.where(g == 2, r8, r16)))
    dp_ref[...] = (r[:tm] - dd[:tm]).astype(dp_ref.dtype)


def _pool_bwd(proj, dyps, wgrp, scale):
    t = proj.shape[0]
    c = wgrp.shape[1]
    tm = _row_tile(t)
    nt = t // tm
    hb = tm // POOL_HALO
    last = t // POOL_HALO - 1
    main, halo = _pool_specs(t, c, tm)
    halo_after = pl.BlockSpec((POOL_HALO, c), lambda g, i: (jnp.minimum((i + 1) * hb, last), g))
    return _call(
        functools.partial(_pool_bwd_body, tm=tm, nt=nt), name="pool_bwd",
        out_shape=(_sds((t, N_POOL_GROUPS * c), BF16), _sds((N_POOL_GROUPS, c, c), F32),
                   _sds((1, N_POOL_GROUPS * c), F32)),
        grid=(N_POOL_GROUPS, nt),
        in_specs=[main, halo, main, halo_after, pl.BlockSpec((None, c, c), lambda g, i: (g, 0, 0)),
                  pl.BlockSpec((1, c), lambda g, i: (0, g))],
        out_specs=(main, pl.BlockSpec((None, c, c), lambda g, i: (g, 0, 0)),
                   pl.BlockSpec((1, c), lambda g, i: (0, g))))(proj, proj, dyps, dyps, wgrp, scale)


def _sgu_bwd_body(u_ref, v_ref, dq_ref, gain_ref, ws_ref, b_ref, duv_ref, dws_ref, db_ref, dgain_ref,
                  dvn_ref, *, tm, heads):
    i = pl.program_id(0)

    @pl.when(i == 0)
    def _():
        dws_ref[...] = jnp.zeros_like(dws_ref)
        db_ref[...] = jnp.zeros_like(db_ref)
        dgain_ref[...] = jnp.zeros_like(dgain_ref)

    sw = heads * SGU_HEAD
    u = u_ref[...].astype(F32)
    v = v_ref[...].astype(F32)
    ug = _gelu(u)
    vg = _gelu(v)
    rstd = _rstd(vg)
    vgn = vg * rstd
    gain = gain_ref[...]
    vn = (vgn * gain).astype(BF16)
    dq = dq_ref[...].astype(F32)
    for h in range(heads):
        wt = _tril_weights(ws_ref, h).astype(BF16)
        cols = slice(h * SGU_HEAD, (h + 1) * SGU_HEAD)
        dws = jnp.zeros((SGU_CHUNK, SGU_CHUNK), F32)
        dbh = jnp.zeros((SGU_CHUNK, 1), F32)
        for c in range(tm // SGU_CHUNK):
            rows = slice(c * SGU_CHUNK, (c + 1) * SGU_CHUNK)
            vn_c = vn[rows, cols]
            sg = _dot(wt, vn_c) + b_ref[h]
            dq_c = dq[rows, cols]
            dsg = dq_c * ug[rows, cols]
            duv_ref[rows, cols] = (dq_c * sg * _gelu_grad(u[rows, cols])).astype(duv_ref.dtype)
            dsg_b = dsg.astype(BF16)
            dws = dws + _dot_nt(dsg_b, vn_c)
            dbh = dbh + jnp.sum(dsg, axis=1, keepdims=True)
            dvn_ref[rows, cols] = _dot_tn(wt, dsg_b)
        dws_ref[h] += jnp.where(_tril_mask(), dws, 0.0)
        db_ref[h] += dbh
    dvn = dvn_ref[...]
    dgain_ref[...] += jnp.sum(dvn * vgn, axis=0, keepdims=True)
    dvgn = dvn * gain
    dvg = rstd * (dvgn - vgn * jnp.mean(dvgn * vgn, axis=-1, keepdims=True))
    duv_ref[:, sw:] = (dvg * _gelu_grad(v)).astype(duv_ref.dtype)


def _sgu_bwd(proj, pw, dq, gain, ws, b3):
    t = proj.shape[0]
    sw = gain.shape[1]
    heads = sw // SGU_HEAD
    tm = _row_tile(t)
    ub = pw // sw
    wblk = pl.BlockSpec((heads, SGU_CHUNK, SGU_CHUNK), lambda i: (0, 0, 0))
    bblk = pl.BlockSpec((heads, SGU_CHUNK, 1), lambda i: (0, 0, 0))
    vec = pl.BlockSpec((1, sw), lambda i: (0, 0))
    return _call(
        functools.partial(_sgu_bwd_body, tm=tm, heads=heads), name="sgu_bwd",
        out_shape=(_sds((t, 2 * sw), BF16), _sds((heads, SGU_CHUNK, SGU_CHUNK), F32),
                   _sds((heads, SGU_CHUNK, 1), F32), _sds((1, sw), F32)),
        grid=(t // tm,),
        in_specs=[pl.BlockSpec((tm, sw), lambda i: (i, ub)),
                  pl.BlockSpec((tm, sw), lambda i: (i, ub + 1)),
                  pl.BlockSpec((tm, sw), lambda i: (i, 0)), vec, wblk, bblk],
        out_specs=(pl.BlockSpec((tm, 2 * sw), lambda i: (i, 0)), wblk, bblk, vec),
        scratch_shapes=[pltpu.VMEM((tm, sw), F32)])(proj, proj, dq, gain, ws, b3)


def _cast_body(x_ref, o_ref):
    o_ref[...] = x_ref[...].astype(o_ref.dtype)


def _cast_bf16(w):
    l, r, c = w.shape
    tr = _slab_rows(r)
    blk = pl.BlockSpec((None, tr, c), lambda a, b: (a, b, 0))
    return _call(functools.partial(_cast_body), name="cast_bf16", out_shape=_sds(w.shape, BF16),
                 grid=(l, r // tr), in_specs=[blk], out_specs=blk)(w)


def _mesh_position():
    x, y, c = (lax.axis_index(a) for a in MESH_AXES)
    return x, y, c


def _peers(x, y, c):
    out = []
    for m in range(1, N_DEV):
        px = 1 - x if m & 4 else x
        py = 1 - y if m & 2 else y
        pc = 1 - c if m & 1 else c
        out.append(((px, py, pc), 4 * px + 2 * py + pc))
    return out


def _all_gather_body(*refs, n_arr, layer):
    srcs, outs = refs[:n_arr], refs[n_arr:2 * n_arr]
    send, recv, loc = refs[2 * n_arr:]
    x, y, c = _mesh_position()
    me = 4 * x + 2 * y + c
    peers = _peers(x, y, c)
    local, remote = [], []
    for k in range(n_arr):
        src = srcs[k] if layer is None else srcs[k].at[layer]
        own = pltpu.make_async_copy(src, outs[k].at[me], loc.at[k])
        own.start()
        local.append(own)
        for j, (peer, _) in enumerate(peers):
            cp = pltpu.make_async_remote_copy(
                src_ref=src, dst_ref=outs[k].at[me], send_sem=send.at[k, j], recv_sem=recv.at[k, j],
                device_id=peer, device_id_type=pl.DeviceIdType.MESH)
            cp.start()
            remote.append(cp)
    for k in range(n_arr):
        src = srcs[k] if layer is None else srcs[k].at[layer]
        for j, (peer, peer_idx) in enumerate(peers):
            pltpu.make_async_remote_copy(
                src_ref=src, dst_ref=outs[k].at[peer_idx], send_sem=send.at[k, j], recv_sem=recv.at[k, j],
                device_id=peer, device_id_type=pl.DeviceIdType.MESH).wait_recv()
    for cp in remote:
        cp.wait_send()
    for cp in local:
        cp.wait()


def _all_gather(shards, layer, name):
    n_arr = len(shards)
    outs = tuple(_sds((N_DEV,) + tuple(s.shape[-2:]), s.dtype) for s in shards)
    any_spec = pl.BlockSpec(memory_space=pl.ANY)
    res = _call(
        functools.partial(_all_gather_body, n_arr=n_arr, layer=layer), name=name, out_shape=outs,
        in_specs=[any_spec] * n_arr, out_specs=tuple([any_spec] * n_arr),
        scratch_shapes=[pltpu.SemaphoreType.DMA((n_arr, N_PEERS)), pltpu.SemaphoreType.DMA((n_arr, N_PEERS)),
                        pltpu.SemaphoreType.DMA((n_arr,))])(*shards)
    return list(res)


def _exchange_body(*refs, n_arr):
    srcs, outs = refs[:n_arr], refs[n_arr:2 * n_arr]
    send, recv, loc = refs[2 * n_arr:]
    x, y, c = _mesh_position()
    me = 4 * x + 2 * y + c
    peers = _peers(x, y, c)
    local, remote = [], []
    for k in range(n_arr):
        own = pltpu.make_async_copy(srcs[k].at[me], outs[k].at[me], loc.at[k])
        own.start()
        local.append(own)
        for j, (peer, peer_idx) in enumerate(peers):
            cp = pltpu.make_async_remote_copy(
                src_ref=srcs[k].at[peer_idx], dst_ref=outs[k].at[me], send_sem=send.at[k, j],
                recv_sem=recv.at[k, j], device_id=peer, device_id_type=pl.DeviceIdType.MESH)
            cp.start()
            remote.append(cp)
    for k in range(n_arr):
        for j, (peer, peer_idx) in enumerate(peers):
            pltpu.make_async_remote_copy(
                src_ref=srcs[k].at[peer_idx], dst_ref=outs[k].at[peer_idx], send_sem=send.at[k, j],
                recv_sem=recv.at[k, j], device_id=peer, device_id_type=pl.DeviceIdType.MESH).wait_recv()
    for cp in remote:
        cp.wait_send()
    for cp in local:
        cp.wait()


def _exchange(grads, name):
    n_arr = len(grads)
    outs = tuple(_sds(g.shape, g.dtype) for g in grads)
    any_spec = pl.BlockSpec(memory_space=pl.ANY)
    res = _call(
        functools.partial(_exchange_body, n_arr=n_arr), name=name, out_shape=outs,
        in_specs=[any_spec] * n_arr, out_specs=tuple([any_spec] * n_arr),
        scratch_shapes=[pltpu.SemaphoreType.DMA((n_arr, N_PEERS)), pltpu.SemaphoreType.DMA((n_arr, N_PEERS)),
                        pltpu.SemaphoreType.DMA((n_arr,))])(*grads)
    return list(res)


def _adamw_math(w, g, m, v):
    m = ADAM_B1 * m + (1.0 - ADAM_B1) * g
    v = ADAM_B2 * v + (1.0 - ADAM_B2) * (g * g)
    m_hat = m / (1.0 - ADAM_B1 ** ADAM_STEP)
    v_hat = v / (1.0 - ADAM_B2 ** ADAM_STEP)
    delta = -ADAM_LR * (m_hat / (jnp.sqrt(v_hat) + ADAM_EPS) + ADAM_WD * w)
    return delta, m, v


def _adamw_body(r_ref, w_ref, m_ref, v_ref, *rest):
    g_out, d_out, m_out, v_out = rest[-4:]
    g = r_ref[0].astype(F32)
    for j in range(1, N_DEV):
        g = g + r_ref[j].astype(F32)
    delta, m, v = _adamw_math(w_ref[...], g, m_ref[...], v_ref[...])
    g_out[...] = g
    d_out[...] = delta
    m_out[...] = m
    v_out[...] = v


def _adamw_layer(recv, w, m, v, prev, layer):
    l, r, c = w.shape
    tr = _slab_rows(r)
    lay = pl.BlockSpec((None, tr, c), lambda i: (layer, i, 0))
    any_spec = pl.BlockSpec(memory_space=pl.ANY)
    out = _sds((l, r, c), F32)
    return _call(
        functools.partial(_adamw_body), name="adamw", out_shape=(out, out, out, out), grid=(r // tr,),
        in_specs=[pl.BlockSpec((N_DEV, tr, c), lambda i: (0, i, 0)), lay, lay, lay] + [any_spec] * 4,
        out_specs=(lay, lay, lay, lay), aliases={4: 0, 5: 1, 6: 2, 7: 3})(recv, w, m, v, *prev)


def _adamw_small_body(r_ref, w_ref, m_ref, v_ref, g_out, d_out, m_out, v_out):
    g = r_ref[0]
    for j in range(1, N_DEV):
        g = g + r_ref[j]
    delta, m, v = _adamw_math(w_ref[...], g, m_ref[...], v_ref[...])
    g_out[...] = g
    d_out[...] = delta
    m_out[...] = m
    v_out[...] = v


def _adamw_small(recv, w, m, v):
    r, c = w.shape
    tr = 8
    for cand in (1024, 512, 256, 128, 64, 32, 16, 8):
        if r % cand == 0:
            tr = cand
            break
    blk = pl.BlockSpec((tr, c), lambda i: (i, 0))
    out = _sds((r, c), F32)
    return _call(
        functools.partial(_adamw_small_body), name="adamw_small", out_shape=(out, out, out, out),
        grid=(r // tr,),
        in_specs=[pl.BlockSpec((N_DEV, tr, c), lambda i: (0, i, 0)), blk, blk, blk],
        out_specs=(blk, blk, blk, blk))(recv, w, m, v)


def _as3d(a):
    return a.reshape(a.shape[0], -1, a.shape[-1])


def _ffn_forward(x, g_pre, wu, wd, g_post):
    h = _rms_fwd(x, g_pre)
    gu, a = _ffn_up(h, wu)
    x_out, f = _down_norm_res(a, wd, x, g_post, MACARON_WEIGHT)
    return x_out, (x, h, gu, a, f)


def _ffn_backward(dy, saved, g_pre, wu, wd, g_post):
    x, h, gu, a, f = saved
    df, dg_post = _norm_res_bwd(dy, f, g_post, MACARON_WEIGHT)
    dz = _dgrad_ffn(df, wd, gu)
    dwd = _wgrad_rows(a, df)
    dwu = _wgrad_cs(h, dz)
    dy, dg_pre = _dgrad_cs_norm(dz, wu, x, g_pre, dy)
    return dy, dg_pre, dwu, dwd, dg_post


def kernel(x, g_ffn1_pre, w_ffn1_up, w_ffn1_down, g_ffn1_post, g_mix_pre, w_in, pool_group_w, pool_scale, w_pool_out, sgu_v_gain, sgu_w_s, sgu_b_s, w_sgu_out, w_out, g_mix_post, g_ffn2_pre, w_ffn2_up, w_ffn2_down, g_ffn2_post, loss_target, m_g_ffn1_pre, m_w_ffn1_up, m_w_ffn1_down, m_g_ffn1_post, m_g_mix_pre, m_w_in, m_pool_group_w, m_pool_scale, m_w_pool_out, m_sgu_v_gain, m_sgu_w_s, m_sgu_b_s, m_w_sgu_out, m_w_out, m_g_mix_post, m_g_ffn2_pre, m_w_ffn2_up, m_w_ffn2_down, m_g_ffn2_post, v_g_ffn1_pre, v_w_ffn1_up, v_w_ffn1_down, v_g_ffn1_post, v_g_mix_pre, v_w_in, v_pool_group_w, v_pool_scale, v_w_pool_out, v_sgu_v_gain, v_sgu_w_s, v_sgu_b_s, v_w_sgu_out, v_w_out, v_g_mix_post, v_g_ffn2_pre, v_w_ffn2_up, v_w_ffn2_down, v_g_ffn2_post):
    given = dict(locals())
    weights = {n: given[n] for n in ORDER}
    mom_m = {n: given["m_" + n] for n in ORDER}
    mom_v = {n: given["v_" + n] for n in ORDER}
    depth = g_ffn1_pre.shape[0]
    d_model = x.shape[-1]
    pool_w = pool_scale.shape[1]
    sgu_w = sgu_v_gain.shape[1]
    groups, grp_rows, grp_c = pool_group_w.shape[1:]
    gate_off = pool_w + 2 * sgu_w

    xs = x.reshape(x.shape[-2], d_model)
    target = loss_target.reshape(xs.shape)

    shards16 = [_cast_bf16(_as3d(weights[n])) for n in BIG]
    gathered = [dict(zip(BIG, _all_gather(shards16, i, "gather_weights"))) for i in range(depth)]

    def vec(name, i):
        return weights[name][i].reshape(1, -1)

    def layer_weights(i):
        gw = gathered[i]
        wgrp = gw["pool_group_w"].reshape(N_DEV, groups, grp_rows, grp_c).transpose(1, 0, 2, 3)
        return dict(
            wu1=gw["w_ffn1_up"], wd1=gw["w_ffn1_down"].reshape(-1, d_model), win=gw["w_in"],
            wgrp=wgrp.reshape(groups, grp_c, grp_c), wpo=gw["w_pool_out"], wso=gw["w_sgu_out"],
            wout=gw["w_out"].reshape(-1, d_model), wu2=gw["w_ffn2_up"],
            wd2=gw["w_ffn2_down"].reshape(-1, d_model),
            ws=weights["sgu_w_s"][i], b3=weights["sgu_b_s"][i][:, :, None])

    saved = []
    cur = xs
    for i in range(depth):
        lw = layer_weights(i)
        cur, s1 = _ffn_forward(cur, vec("g_ffn1_pre", i), lw["wu1"], lw["wd1"], vec("g_ffn1_post", i))
        x1 = cur
        h2 = _rms_fwd(x1, vec("g_mix_pre", i))
        proj = _mm_cs(h2, lw["win"])
        yps = _pool_fwd(proj, lw["wgrp"], vec("pool_scale", i))
        q = _sgu_fwd(proj, pool_w, vec("sgu_v_gain", i), lw["ws"], lw["b3"])
        m, ya, yb = _mix_gate(yps, q, proj, gate_off, lw["wpo"], lw["wso"])
        cur, o = _down_norm_res(m, lw["wout"], x1, vec("g_mix_post", i), 1.0)
        s2 = (x1, h2, proj, yps, q, ya, yb, m, o)
        cur, s3 = _ffn_forward(cur, vec("g_ffn2_pre", i), lw["wu2"], lw["wd2"], vec("g_ffn2_post", i))
        saved.append((lw, s1, s2, s3))

    dy, sq = _loss_grad(cur, target)
    loss = lax.psum(0.5 * sq[0, 0] / d_model, MESH_AXES)

    small_grads = {n: [None] * depth for n in SMALL}
    outs = {n: None for n in BIG}
    for i in reversed(range(depth)):
        lw, s1, s2, s3 = saved[i]
        dy, dg, dwu2, dwd2, dgp = _ffn_backward(dy, s3, vec("g_ffn2_pre", i), lw["wu2"], lw["wd2"],
                                                vec("g_ffn2_post", i))
        small_grads["g_ffn2_pre"][i], small_grads["g_ffn2_post"][i] = dg, dgp

        x1, h2, proj, yps, q, ya, yb, m, o = s2
        do, small_grads["g_mix_post"][i] = _norm_res_bwd(dy, o, vec("g_mix_post", i), 1.0)
        dya, dyb, dga, dgb = _dgrad_gate(do, lw["wout"], ya, yb, proj, gate_off)
        dwout = _wgrad_rows(m, do)
        dwpo = _wgrad_cs(yps, dya[None])
        dwso = _wgrad_cs(q, dyb[None])
        dyps = _dgrad_cs(dya[None], lw["wpo"])
        dq = _dgrad_cs(dyb[None], lw["wso"])
        dp, dwgrp, small_grads["pool_scale"][i] = _pool_bwd(proj, dyps, lw["wgrp"], vec("pool_scale", i))
        duv, dws, db3, small_grads["sgu_v_gain"][i] = _sgu_bwd(
            proj, pool_w, dq, vec("sgu_v_gain", i), lw["ws"], lw["b3"])
        small_grads["sgu_w_s"][i] = dws
        small_grads["sgu_b_s"][i] = db3
        dproj = jnp.concatenate([dp, duv, dga, dgb], axis=1)[None]
        dwin = _wgrad_cs(h2, dproj)
        dy, small_grads["g_mix_pre"][i] = _dgrad_cs_norm(dproj, lw["win"], x1, vec("g_mix_pre", i), dy)

        dy, dg, dwu1, dwd1, dgp = _ffn_backward(dy, s1, vec("g_ffn1_pre", i), lw["wu1"], lw["wd1"],
                                                vec("g_ffn1_post", i))
        small_grads["g_ffn1_pre"][i], small_grads["g_ffn1_post"][i] = dg, dgp

        dwgrp8 = dwgrp.reshape(groups, N_DEV, grp_rows, grp_c).transpose(1, 0, 2, 3)
        full = dict(
            w_ffn1_up=dwu1, w_ffn1_down=dwd1.reshape(N_DEV, -1, d_model), w_in=dwin,
            pool_group_w=dwgrp8.reshape(N_DEV, groups * grp_rows, grp_c).astype(BF16),
            w_pool_out=dwpo, w_sgu_out=dwso, w_out=dwout.reshape(N_DEV, -1, d_model),
            w_ffn2_up=dwu2, w_ffn2_down=dwd2.reshape(N_DEV, -1, d_model))
        received = _exchange([full[n] for n in BIG], "scatter_grads")
        for n, r in zip(BIG, received):
            w3, m3, v3 = _as3d(weights[n]), _as3d(mom_m[n]), _as3d(mom_v[n])
            prev = outs[n] if outs[n] is not None else [lax.empty(w3.shape, F32) for _ in range(4)]
            outs[n] = _adamw_layer(r, w3, m3, v3, prev, i)

    def pack(parts):
        return jnp.concatenate([p.reshape(depth, -1) for p in parts], axis=1).reshape(-1, 128)

    part = pack([jnp.stack(small_grads[n]) for n in SMALL])
    (all_parts,) = _all_gather([part], None, "gather_small_grads")
    small_out = _adamw_small(all_parts, pack([weights[n] for n in SMALL]), pack([mom_m[n] for n in SMALL]),
                             pack([mom_v[n] for n in SMALL]))

    results = {}
    sizes = [weights[n][0].size for n in SMALL]
    for kind, packed in zip(("grad", "delta", "new_m", "new_v"), small_out):
        flat = packed.reshape(depth, -1)
        off = 0
        for n, sz in zip(SMALL, sizes):
            results[(kind, n)] = flat[:, off:off + sz].reshape(weights[n].shape)
            off += sz
    for n in BIG:
        for kind, arr in zip(("grad", "delta", "new_m", "new_v"), outs[n]):
            results[(kind, n)] = arr.reshape(weights[n].shape)

    grad_x = dy.reshape(x.shape)
    return (loss, grad_x, *[results[("grad", n)] for n in ORDER], *[results[("delta", n)] for n in ORDER],
            *[results[("new_m", n)] for n in ORDER], *[results[("new_v", n)] for n in ORDER])
```

```python
import functools

import jax
import jax.numpy as jnp
from jax import lax
from jax.experimental import pallas as pl
from jax.experimental.pallas import tpu as pltpu

F32 = jnp.float32
BF16 = jnp.bfloat16

N_DEV = 8
MESH_AXES = ("x", "y", "c")
N_PEERS = N_DEV - 1

EPS = 1e-6
MACARON_WEIGHT = 0.5
N_POOL_GROUPS = 4
POOL_HALO = 16
SGU_HEAD = 128
SGU_CHUNK = 128
GELU_C = 0.7978845608028654
GELU_A = 0.044715

ADAM_LR = 0.001
ADAM_B1 = 0.9
ADAM_B2 = 0.999
ADAM_EPS = 1e-08
ADAM_WD = 0.01
ADAM_STEP = 10

ROW_TILE = 512
VMEM_LIMIT_BYTES = 56 * 1024 * 1024

BIG = ("w_ffn1_up", "w_ffn1_down", "w_in", "pool_group_w", "w_pool_out", "w_sgu_out", "w_out",
       "w_ffn2_up", "w_ffn2_down")
SMALL = ("g_ffn1_pre", "g_ffn1_post", "g_mix_pre", "pool_scale", "sgu_v_gain", "sgu_w_s", "sgu_b_s",
         "g_mix_post", "g_ffn2_pre", "g_ffn2_post")
ORDER = ("g_ffn1_pre", "w_ffn1_up", "w_ffn1_down", "g_ffn1_post", "g_mix_pre", "w_in", "pool_group_w",
         "pool_scale", "w_pool_out", "sgu_v_gain", "sgu_w_s", "sgu_b_s", "w_sgu_out", "w_out",
         "g_mix_post", "g_ffn2_pre", "w_ffn2_up", "w_ffn2_down", "g_ffn2_post")


ANY_SPEC = pl.BlockSpec(memory_space=pl.ANY)
HBM_SPEC = pl.BlockSpec(memory_space=pltpu.HBM)
SEM_SPEC = pl.BlockSpec(memory_space=pltpu.SEMAPHORE)


def _skip_refs(body, n_in, n_skip, *refs):
    return body(*refs[:n_in], *refs[n_in + n_skip:])


def _call(body, *, name, out_shape, grid=(), in_specs=None, out_specs=None, scratch_shapes=(),
          aliases=None, after=(), grid_spec=None):
    extra = {}
    if grid_spec is not None:
        extra["grid_spec"] = grid_spec
    else:
        extra.update(in_specs=list(in_specs) + [ANY_SPEC] * len(after), out_specs=out_specs,
                     scratch_shapes=list(scratch_shapes))
        if grid:
            extra["grid"] = grid
        if after:
            body = functools.partial(_skip_refs, body, len(in_specs), len(after))
    if aliases:
        extra["input_output_aliases"] = aliases
    call = pl.pallas_call(
        body, name=name, out_shape=out_shape,
        compiler_params=pltpu.CompilerParams(vmem_limit_bytes=VMEM_LIMIT_BYTES), **extra)
    if after:
        return lambda *args: call(*args, *after)
    return call


def _sds(shape, dtype):
    return jax.ShapeDtypeStruct(tuple(shape), dtype)


def _dot(a, b):
    return jnp.dot(a, b, preferred_element_type=F32)


def _dot_nt(a, b):
    return lax.dot_general(a, b, (((1,), (1,)), ((), ())), preferred_element_type=F32)


def _dot_tn(a, b):
    return lax.dot_general(a, b, (((0,), (0,)), ((), ())), preferred_element_type=F32)


def _sigmoid(x):
    return 1.0 / (1.0 + jnp.exp(-x))


def _gelu(x):
    return 0.5 * x * (1.0 + jnp.tanh(GELU_C * (x + GELU_A * x * x * x)))


def _gelu_grad(x):
    t = jnp.tanh(GELU_C * (x + GELU_A * x * x * x))
    return 0.5 * (1.0 + t) + 0.5 * x * (1.0 - t * t) * (GELU_C * (1.0 + 3.0 * GELU_A * x * x))


def _rstd(x):
    return lax.rsqrt(jnp.mean(x * x, axis=-1, keepdims=True) + EPS)


def _row_tile(t):
    return min(t, ROW_TILE)


def _slab_rows(r):
    for cand in (256, 128, 64, 32, 16):
        if r % cand == 0:
            return cand
    return r


def _rms_fwd_body(x_ref, g_ref, h_ref):
    x = x_ref[...]
    h_ref[...] = ((x * _rstd(x)) * g_ref[...]).astype(h_ref.dtype)


def _rms_fwd(x, g, after=()):
    t, d = x.shape
    tm = _row_tile(t)
    return _call(
        functools.partial(_rms_fwd_body), name="rms_fwd", out_shape=_sds((t, d), BF16),
        grid=(t // tm,),
        in_specs=[pl.BlockSpec((tm, d), lambda i: (i, 0)), pl.BlockSpec((1, d), lambda i: (0, 0))],
        out_specs=pl.BlockSpec((tm, d), lambda i: (i, 0)), after=after)(x, g)


def _mm_cs_body(h_ref, w_ref, o_ref):
    o_ref[...] = _dot(h_ref[...], w_ref[...]).astype(o_ref.dtype)


def _mm_cs(h, wg):
    t, k = h.shape
    n = wg.shape[2]
    tm = _row_tile(t)
    return _call(
        functools.partial(_mm_cs_body), name="mm_cs", out_shape=_sds((t, N_DEV * n), BF16),
        grid=(N_DEV, t // tm),
        in_specs=[pl.BlockSpec((tm, k), lambda j, i: (i, 0)),
                  pl.BlockSpec((None, k, n), lambda j, i: (j, 0, 0))],
        out_specs=pl.BlockSpec((tm, n), lambda j, i: (i, j)))(h, wg)


def _ffn_up_body(h_ref, wg_ref, wu_ref, gu_ref, a_ref):
    h = h_ref[...]
    g = _dot(h, wg_ref[...])
    u = _dot(h, wu_ref[...])
    gu_ref[0] = g.astype(gu_ref.dtype)
    gu_ref[1] = u.astype(gu_ref.dtype)
    a_ref[...] = (g * _sigmoid(g) * u).astype(a_ref.dtype)


def _ffn_up(h, wg):
    t, k = h.shape
    n = wg.shape[2]
    half = N_DEV // 2
    f = half * n
    tm = _row_tile(t)
    return _call(
        functools.partial(_ffn_up_body), name="ffn_up",
        out_shape=(_sds((2, t, f), BF16), _sds((t, f), BF16)),
        grid=(half, t // tm),
        in_specs=[pl.BlockSpec((tm, k), lambda j, i: (i, 0)),
                  pl.BlockSpec((None, k, n), lambda j, i: (j, 0, 0)),
                  pl.BlockSpec((None, k, n), lambda j, i: (j + N_DEV // 2, 0, 0))],
        out_specs=(pl.BlockSpec((2, tm, n), lambda j, i: (0, i, j)),
                   pl.BlockSpec((tm, n), lambda j, i: (i, j))))(h, wg, wg)


def _down_body(a_ref, w_ref, x_ref, g_ref, xo_ref, f_ref, acc_ref, *, scale, nk):
    k = pl.program_id(1)

    @pl.when(k == 0)
    def _():
        acc_ref[...] = jnp.zeros_like(acc_ref)

    acc_ref[...] += _dot(a_ref[...], w_ref[...])

    @pl.when(k == nk - 1)
    def _():
        f = acc_ref[...]
        f_ref[...] = f.astype(f_ref.dtype)
        xo_ref[...] = x_ref[...] + scale * ((f * _rstd(f)) * g_ref[...])


def _down_norm_res(a, w, x, g, scale, after=()):
    t, kk = a.shape
    d = w.shape[1]
    tm = _row_tile(t)
    tk = 512 if kk % 512 == 0 else 256
    nk = kk // tk
    return _call(
        functools.partial(_down_body, scale=scale, nk=nk), name="down_norm_res",
        out_shape=(_sds((t, d), F32), _sds((t, d), BF16)),
        grid=(t // tm, nk),
        in_specs=[pl.BlockSpec((tm, tk), lambda i, k: (i, k)),
                  pl.BlockSpec((tk, d), lambda i, k: (k, 0)),
                  pl.BlockSpec((tm, d), lambda i, k: (i, 0)),
                  pl.BlockSpec((1, d), lambda i, k: (0, 0))],
        out_specs=(pl.BlockSpec((tm, d), lambda i, k: (i, 0)),
                   pl.BlockSpec((tm, d), lambda i, k: (i, 0))),
        scratch_shapes=[pltpu.VMEM((tm, d), F32)], after=after)(a, w, x, g)


def _pool_inv_count(g, i, tm):
    t1 = i * tm + lax.broadcasted_iota(jnp.int32, (tm, 1), 0) + 1
    return 1.0 / jnp.minimum(t1, 2 << g).astype(F32)


def _pool_windows(e, g):
    s2 = e + pltpu.roll(e, 1, 0)
    s4 = s2 + pltpu.roll(s2, 2, 0)
    s8 = s4 + pltpu.roll(s4, 4, 0)
    s16 = s8 + pltpu.roll(s8, 8, 0)
    return jnp.where(g == 0, s2, jnp.where(g == 1, s4, jnp.where(g == 2, s8, s16)))


def _pool_delta(p_ref, ph_ref, g, i, tm):
    pm = p_ref[...].astype(F32)
    ph = jnp.where(i == 0, 0.0, ph_ref[...].astype(F32))
    e = jnp.concatenate([ph, pm], axis=0)
    s = _pool_windows(e, g)[POOL_HALO:]
    return s * _pool_inv_count(g, i, tm) - pm


def _pool_fwd_body(p_ref, ph_ref, w_ref, sc_ref, o_ref, *, tm):
    g = pl.program_id(0)
    i = pl.program_id(1)
    d = _pool_delta(p_ref, ph_ref, g, i, tm)
    o_ref[...] = (_dot(d.astype(BF16), w_ref[...]) * sc_ref[...]).astype(o_ref.dtype)


def _pool_specs(t, c, tm):
    hb = tm // POOL_HALO
    main = pl.BlockSpec((tm, c), lambda g, i: (i, g))
    halo_before = pl.BlockSpec((POOL_HALO, c), lambda g, i: (jnp.maximum(i * hb - 1, 0), g))
    return main, halo_before


def _pool_fwd(proj, wgrp, scale):
    t = proj.shape[0]
    c = wgrp.shape[1]
    tm = _row_tile(t)
    main, halo = _pool_specs(t, c, tm)
    return _call(
        functools.partial(_pool_fwd_body, tm=tm), name="pool_fwd",
        out_shape=_sds((t, N_POOL_GROUPS * c), BF16), grid=(N_POOL_GROUPS, t // tm),
        in_specs=[main, halo, pl.BlockSpec((None, c, c), lambda g, i: (g, 0, 0)),
                  pl.BlockSpec((1, c), lambda g, i: (0, g))],
        out_specs=pl.BlockSpec((tm, c), lambda g, i: (i, g)))(proj, proj, wgrp, scale)


def _tril_mask():
    r = lax.broadcasted_iota(jnp.int32, (SGU_CHUNK, SGU_CHUNK), 0)
    c = lax.broadcasted_iota(jnp.int32, (SGU_CHUNK, SGU_CHUNK), 1)
    return r >= c


def _tril_weights(ws_ref, h):
    return jnp.where(_tril_mask(), ws_ref[h], 0.0)


def _sgu_fwd_body(u_ref, v_ref, gain_ref, ws_ref, b_ref, q_ref, *, tm, heads):
    ug = _gelu(u_ref[...].astype(F32))
    vg = _gelu(v_ref[...].astype(F32))
    vn = ((vg * _rstd(vg)) * gain_ref[...]).astype(BF16)
    for h in range(heads):
        wt = _tril_weights(ws_ref, h).astype(BF16)
        cols = slice(h * SGU_HEAD, (h + 1) * SGU_HEAD)
        for c in range(tm // SGU_CHUNK):
            rows = slice(c * SGU_CHUNK, (c + 1) * SGU_CHUNK)
            sg = _dot(wt, vn[rows, cols]) + b_ref[h]
            q_ref[rows, cols] = (ug[rows, cols] * sg).astype(q_ref.dtype)


def _sgu_fwd(proj, pw, gain, ws, b3):
    t = proj.shape[0]
    sw = gain.shape[1]
    heads = sw // SGU_HEAD
    tm = _row_tile(t)
    ub = pw // sw
    return _call(
        functools.partial(_sgu_fwd_body, tm=tm, heads=heads), name="sgu_fwd",
        out_shape=_sds((t, sw), BF16), grid=(t // tm,),
        in_specs=[pl.BlockSpec((tm, sw), lambda i: (i, ub)),
                  pl.BlockSpec((tm, sw), lambda i: (i, ub + 1)),
                  pl.BlockSpec((1, sw), lambda i: (0, 0)),
                  pl.BlockSpec((heads, SGU_CHUNK, SGU_CHUNK), lambda i: (0, 0, 0)),
                  pl.BlockSpec((heads, SGU_CHUNK, 1), lambda i: (0, 0, 0))],
        out_specs=pl.BlockSpec((tm, sw), lambda i: (i, 0)))(proj, proj, gain, ws, b3)


def _mix_gate_body(yps_ref, q_ref, ga_ref, gb_ref, wp_ref, ws_ref, m_ref, ya_ref, yb_ref):
    ya = _dot(yps_ref[...], wp_ref[...])
    yb = _dot(q_ref[...], ws_ref[...])
    ya_ref[...] = ya.astype(ya_ref.dtype)
    yb_ref[...] = yb.astype(yb_ref.dtype)
    m = _sigmoid(ga_ref[...].astype(F32)) * ya + _sigmoid(gb_ref[...].astype(F32)) * yb
    m_ref[...] = m.astype(m_ref.dtype)


def _mix_gate(yps, q, proj, gate_off, wpo, wso):
    t, pw = yps.shape
    sw = q.shape[1]
    n = wpo.shape[2]
    d = N_DEV * n
    tm = _row_tile(t)
    ga0 = gate_off // n
    gb0 = (gate_off + d) // n
    out = _sds((t, d), BF16)
    blk = pl.BlockSpec((tm, n), lambda j, i: (i, j))
    return _call(
        functools.partial(_mix_gate_body), name="mix_gate", out_shape=(out, out, out),
        grid=(N_DEV, t // tm),
        in_specs=[pl.BlockSpec((tm, pw), lambda j, i: (i, 0)),
                  pl.BlockSpec((tm, sw), lambda j, i: (i, 0)),
                  pl.BlockSpec((tm, n), lambda j, i: (i, ga0 + j)),
                  pl.BlockSpec((tm, n), lambda j, i: (i, gb0 + j)),
                  pl.BlockSpec((None, pw, n), lambda j, i: (j, 0, 0)),
                  pl.BlockSpec((None, sw, n), lambda j, i: (j, 0, 0))],
        out_specs=(blk, blk, blk))(yps, q, proj, proj, wpo, wso)


def _loss_body(y_ref, t_ref, dy_ref, l_ref, *, inv_d):
    i = pl.program_id(0)

    @pl.when(i == 0)
    def _():
        l_ref[...] = jnp.zeros_like(l_ref)

    e = y_ref[...] - t_ref[...]
    dy_ref[...] = e * inv_d
    l_ref[...] += jnp.sum(e * e)


def _loss_grad(y, target):
    t, d = y.shape
    tm = _row_tile(t)
    blk = pl.BlockSpec((tm, d), lambda i: (i, 0))
    return _call(
        functools.partial(_loss_body, inv_d=1.0 / d), name="loss_grad",
        out_shape=(_sds((t, d), F32), _sds((8, 128), F32)), grid=(t // tm,),
        in_specs=[blk, blk],
        out_specs=(blk, pl.BlockSpec((8, 128), lambda i: (0, 0))))(y, target)


def _norm_bwd_body(dy_ref, f_ref, g_ref, df_ref, dg_ref, *, scale):
    i = pl.program_id(0)

    @pl.when(i == 0)
    def _():
        dg_ref[...] = jnp.zeros_like(dg_ref)

    f = f_ref[...].astype(F32)
    rstd = _rstd(f)
    fn = f * rstd
    dr = scale * dy_ref[...]
    dg_ref[...] += jnp.sum(dr * fn, axis=0, keepdims=True)
    dfn = dr * g_ref[...]
    df = rstd * (dfn - fn * jnp.mean(dfn * fn, axis=-1, keepdims=True))
    df_ref[...] = df.astype(df_ref.dtype)


def _norm_res_bwd(dy, f, g, scale, after=()):
    t, d = dy.shape
    tm = _row_tile(t)
    blk = pl.BlockSpec((tm, d), lambda i: (i, 0))
    vec = pl.BlockSpec((1, d), lambda i: (0, 0))
    return _call(
        functools.partial(_norm_bwd_body, scale=scale), name="norm_res_bwd",
        out_shape=(_sds((t, d), BF16), _sds((1, d), F32)), grid=(t // tm,),
        in_specs=[blk, blk, vec], out_specs=(blk, vec), after=after)(dy, f, g)


def _dgrad_ffn_body(df_ref, w_ref, gu_ref, dz_ref):
    da = _dot_nt(df_ref[...], w_ref[...])
    g = gu_ref[0].astype(F32)
    u = gu_ref[1].astype(F32)
    sig = _sigmoid(g)
    dz_ref[0] = (da * u * (sig * (1.0 + g * (1.0 - sig)))).astype(dz_ref.dtype)
    dz_ref[1] = (da * (g * sig)).astype(dz_ref.dtype)


def _dgrad_ffn(df, wd, gu):
    t, d = df.shape
    f = wd.shape[0]
    tm = _row_tile(t)
    tn = 512 if f % 512 == 0 else 256
    blk = pl.BlockSpec((2, tm, tn), lambda j, i: (0, i, j))
    return _call(
        functools.partial(_dgrad_ffn_body), name="dgrad_ffn", out_shape=_sds((2, t, f), BF16),
        grid=(f // tn, t // tm),
        in_specs=[pl.BlockSpec((tm, d), lambda j, i: (i, 0)),
                  pl.BlockSpec((tn, d), lambda j, i: (j, 0)), blk],
        out_specs=blk)(df, wd, gu)


def _dgrad_gate_body(do_ref, w_ref, ya_ref, yb_ref, ga_ref, gb_ref, dya_ref, dyb_ref, dga_ref, dgb_ref):
    dm = _dot_nt(do_ref[...], w_ref[...])
    sa = _sigmoid(ga_ref[...].astype(F32))
    sb = _sigmoid(gb_ref[...].astype(F32))
    dya_ref[...] = (dm * sa).astype(dya_ref.dtype)
    dyb_ref[...] = (dm * sb).astype(dyb_ref.dtype)
    dga_ref[...] = (dm * ya_ref[...].astype(F32) * (sa * (1.0 - sa))).astype(dga_ref.dtype)
    dgb_ref[...] = (dm * yb_ref[...].astype(F32) * (sb * (1.0 - sb))).astype(dgb_ref.dtype)


def _dgrad_gate(do, wout, ya, yb, proj, gate_off):
    t, d = do.shape
    tm = _row_tile(t)
    tn = 512 if d % 512 == 0 else 256
    ga0 = gate_off // tn
    gb0 = (gate_off + d) // tn
    blk = pl.BlockSpec((tm, tn), lambda j, i: (i, j))
    out = _sds((t, d), BF16)
    return _call(
        functools.partial(_dgrad_gate_body), name="dgrad_gate", out_shape=(out, out, out, out),
        grid=(d // tn, t // tm),
        in_specs=[pl.BlockSpec((tm, d), lambda j, i: (i, 0)),
                  pl.BlockSpec((tn, d), lambda j, i: (j, 0)), blk, blk,
                  pl.BlockSpec((tm, tn), lambda j, i: (i, ga0 + j)),
                  pl.BlockSpec((tm, tn), lambda j, i: (i, gb0 + j))],
        out_specs=(blk, blk, blk, blk))(do, wout, ya, yb, proj, proj)


def _wgrad_rows_body(a_ref, b_ref, o_ref, acc_ref, *, nt):
    t = pl.program_id(1)

    @pl.when(t == 0)
    def _():
        acc_ref[...] = jnp.zeros_like(acc_ref)

    acc_ref[...] += _dot_tn(a_ref[...], b_ref[...])

    @pl.when(t == nt - 1)
    def _():
        o_ref[...] = acc_ref[...].astype(o_ref.dtype)


def _wgrad_rows(a, b, after=()):
    t, k = a.shape
    n = b.shape[1]
    tt = _row_tile(t)
    tk = k // 4
    nt = t // tt
    return _call(
        functools.partial(_wgrad_rows_body, nt=nt), name="wgrad_rows", out_shape=_sds((k, n), BF16),
        grid=(k // tk, nt),
        in_specs=[pl.BlockSpec((tt, tk), lambda r, s: (s, r)),
                  pl.BlockSpec((tt, n), lambda r, s: (s, 0))],
        out_specs=pl.BlockSpec((tk, n), lambda r, s: (r, 0)),
        scratch_shapes=[pltpu.VMEM((tk, n), F32)], after=after)(a, b)


def _wgrad_cs(a, b3):
    t, k = a.shape
    parts = b3.shape[0]
    per = N_DEV // parts
    n = b3.shape[2] // per
    tt = _row_tile(t)
    nt = t // tt
    return _call(
        functools.partial(_wgrad_rows_body, nt=nt), name="wgrad_cs", out_shape=_sds((N_DEV, k, n), BF16),
        grid=(N_DEV, nt),
        in_specs=[pl.BlockSpec((tt, k), lambda r, s: (s, 0)),
                  pl.BlockSpec((None, tt, n), lambda r, s: (r // per, s, r % per))],
        out_specs=pl.BlockSpec((None, k, n), lambda r, s: (r, 0, 0)),
        scratch_shapes=[pltpu.VMEM((k, n), F32)])(a, b3)


def _dgrad_cs_body(dz_ref, w_ref, o_ref, acc_ref):
    s = pl.program_id(1)

    @pl.when(s == 0)
    def _():
        acc_ref[...] = jnp.zeros_like(acc_ref)

    acc_ref[...] += _dot_nt(dz_ref[...], w_ref[...])

    @pl.when(s == N_DEV - 1)
    def _():
        o_ref[...] = acc_ref[...].astype(o_ref.dtype)


def _dgrad_cs(dz3, wg):
    parts, t, _ = dz3.shape
    per = N_DEV // parts
    k, n = wg.shape[1], wg.shape[2]
    tm = _row_tile(t)
    return _call(
        functools.partial(_dgrad_cs_body), name="dgrad_cs", out_shape=_sds((t, k), BF16),
        grid=(t // tm, N_DEV),
        in_specs=[pl.BlockSpec((None, tm, n), lambda i, s: (s // per, i, s % per)),
                  pl.BlockSpec((None, k, n), lambda i, s: (s, 0, 0))],
        out_specs=pl.BlockSpec((tm, k), lambda i, s: (i, 0)),
        scratch_shapes=[pltpu.VMEM((tm, k), F32)])(dz3, wg)


def _dgrad_cs_norm_body(dz_ref, w_ref, x_ref, g_ref, dy_ref, o_ref, dg_ref, acc_ref):
    i = pl.program_id(0)
    s = pl.program_id(1)

    @pl.when(jnp.logical_and(i == 0, s == 0))
    def _():
        dg_ref[...] = jnp.zeros_like(dg_ref)

    @pl.when(s == 0)
    def _():
        acc_ref[...] = jnp.zeros_like(acc_ref)

    acc_ref[...] += _dot_nt(dz_ref[...], w_ref[...])

    @pl.when(s == N_DEV - 1)
    def _():
        dh = acc_ref[...]
        x = x_ref[...]
        rstd = _rstd(x)
        xn = x * rstd
        dg_ref[...] += jnp.sum(dh * xn, axis=0, keepdims=True)
        dxn = dh * g_ref[...]
        o_ref[...] = dy_ref[...] + rstd * (dxn - xn * jnp.mean(dxn * xn, axis=-1, keepdims=True))


def _dgrad_cs_norm(dz3, wg, x, g, dy):
    parts, t, _ = dz3.shape
    per = N_DEV // parts
    d, n = wg.shape[1], wg.shape[2]
    tm = _row_tile(t)
    blk = pl.BlockSpec((tm, d), lambda i, s: (i, 0))
    vec = pl.BlockSpec((1, d), lambda i, s: (0, 0))
    return _call(
        functools.partial(_dgrad_cs_norm_body), name="dgrad_cs_norm",
        out_shape=(_sds((t, d), F32), _sds((1, d), F32)), grid=(t // tm, N_DEV),
        in_specs=[pl.BlockSpec((None, tm, n), lambda i, s: (s // per, i, s % per)),
                  pl.BlockSpec((None, d, n), lambda i, s: (s, 0, 0)), blk, vec, blk],
        out_specs=(blk, vec),
        scratch_shapes=[pltpu.VMEM((tm, d), F32)])(dz3, wg, x, g, dy)


def _pool_bwd_body(p_ref, ph_ref, dy_ref, dyh_ref, w_ref, sc_ref, dp_ref, dw_ref, dsc_ref, *, tm, nt):
    g = pl.program_id(0)
    i = pl.program_id(1)

    @pl.when(i == 0)
    def _():
        dw_ref[...] = jnp.zeros_like(dw_ref)
        dsc_ref[...] = jnp.zeros_like(dsc_ref)

    w = w_ref[...]
    sc = sc_ref[...]
    d = _pool_delta(p_ref, ph_ref, g, i, tm).astype(BF16)
    dyps = dy_ref[...].astype(F32)
    dsc_ref[...] += jnp.sum(dyps * _dot(d, w), axis=0, keepdims=True)
    dyp = (dyps * sc).astype(BF16)
    dw_ref[...] += _dot_tn(d, dyp)
    dyp_after = (jnp.where(i == nt - 1, 0.0, dyh_ref[...].astype(F32)) * sc).astype(BF16)
    dd = jnp.concatenate([_dot_nt(dyp, w), _dot_nt(dyp_after, w)], axis=0)
    rows = tm + POOL_HALO
    t1 = i * tm + lax.broadcasted_iota(jnp.int32, (rows, 1), 0) + 1
    e = dd / jnp.minimum(t1, 2 << g).astype(F32)
    r2 = e + pltpu.roll(e, rows - 1, 0)
    r4 = r2 + pltpu.roll(r2, rows - 2, 0)
    r8 = r4 + pltpu.roll(r4, rows - 4, 0)
    r16 = r8 + pltpu.roll(r8, rows - 8, 0)
    r = jnp.where(g == 0, r2, jnp.where(g == 1, r4, jnp.where(g == 2, r8, r16)))
    dp_ref[...] = (r[:tm] - dd[:tm]).astype(dp_ref.dtype)


def _pool_bwd(proj, dyps, wgrp, scale):
    t = proj.shape[0]
    c = wgrp.shape[1]
    tm = _row_tile(t)
    nt = t // tm
    hb = tm // POOL_HALO
    last = t // POOL_HALO - 1
    main, halo = _pool_specs(t, c, tm)
    halo_after = pl.BlockSpec((POOL_HALO, c), lambda g, i: (jnp.minimum((i + 1) * hb, last), g))
    return _call(
        functools.partial(_pool_bwd_body, tm=tm, nt=nt), name="pool_bwd",
        out_shape=(_sds((t, N_POOL_GROUPS * c), BF16), _sds((N_POOL_GROUPS, c, c), F32),
                   _sds((1, N_POOL_GROUPS * c), F32)),
        grid=(N_POOL_GROUPS, nt),
        in_specs=[main, halo, main, halo_after, pl.BlockSpec((None, c, c), lambda g, i: (g, 0, 0)),
                  pl.BlockSpec((1, c), lambda g, i: (0, g))],
        out_specs=(main, pl.BlockSpec((None, c, c), lambda g, i: (g, 0, 0)),
                   pl.BlockSpec((1, c), lambda g, i: (0, g))))(proj, proj, dyps, dyps, wgrp, scale)


def _sgu_bwd_body(u_ref, v_ref, dq_ref, gain_ref, ws_ref, b_ref, duv_ref, dws_ref, db_ref, dgain_ref,
                  dvn_ref, *, tm, heads):
    i = pl.program_id(0)

    @pl.when(i == 0)
    def _():
        dws_ref[...] = jnp.zeros_like(dws_ref)
        db_ref[...] = jnp.zeros_like(db_ref)
        dgain_ref[...] = jnp.zeros_like(dgain_ref)

    sw = heads * SGU_HEAD
    u = u_ref[...].astype(F32)
    v = v_ref[...].astype(F32)
    ug = _gelu(u)
    vg = _gelu(v)
    rstd = _rstd(vg)
    vgn = vg * rstd
    gain = gain_ref[...]
    vn = (vgn * gain).astype(BF16)
    dq = dq_ref[...].astype(F32)
    for h in range(heads):
        wt = _tril_weights(ws_ref, h).astype(BF16)
        cols = slice(h * SGU_HEAD, (h + 1) * SGU_HEAD)
        dws = jnp.zeros((SGU_CHUNK, SGU_CHUNK), F32)
        dbh = jnp.zeros((SGU_CHUNK, 1), F32)
        for c in range(tm // SGU_CHUNK):
            rows = slice(c * SGU_CHUNK, (c + 1) * SGU_CHUNK)
            vn_c = vn[rows, cols]
            sg = _dot(wt, vn_c) + b_ref[h]
            dq_c = dq[rows, cols]
            dsg = dq_c * ug[rows, cols]
            duv_ref[rows, cols] = (dq_c * sg * _gelu_grad(u[rows, cols])).astype(duv_ref.dtype)
            dsg_b = dsg.astype(BF16)
            dws = dws + _dot_nt(dsg_b, vn_c)
            dbh = dbh + jnp.sum(dsg, axis=1, keepdims=True)
            dvn_ref[rows, cols] = _dot_tn(wt, dsg_b)
        dws_ref[h] += jnp.where(_tril_mask(), dws, 0.0)
        db_ref[h] += dbh
    dvn = dvn_ref[...]
    dgain_ref[...] += jnp.sum(dvn * vgn, axis=0, keepdims=True)
    dvgn = dvn * gain
    dvg = rstd * (dvgn - vgn * jnp.mean(dvgn * vgn, axis=-1, keepdims=True))
    duv_ref[:, sw:] = (dvg * _gelu_grad(v)).astype(duv_ref.dtype)


def _sgu_bwd(proj, pw, dq, gain, ws, b3):
    t = proj.shape[0]
    sw = gain.shape[1]
    heads = sw // SGU_HEAD
    tm = _row_tile(t)
    ub = pw // sw
    wblk = pl.BlockSpec((heads, SGU_CHUNK, SGU_CHUNK), lambda i: (0, 0, 0))
    bblk = pl.BlockSpec((heads, SGU_CHUNK, 1), lambda i: (0, 0, 0))
    vec = pl.BlockSpec((1, sw), lambda i: (0, 0))
    return _call(
        functools.partial(_sgu_bwd_body, tm=tm, heads=heads), name="sgu_bwd",
        out_shape=(_sds((t, 2 * sw), BF16), _sds((heads, SGU_CHUNK, SGU_CHUNK), F32),
                   _sds((heads, SGU_CHUNK, 1), F32), _sds((1, sw), F32)),
        grid=(t // tm,),
        in_specs=[pl.BlockSpec((tm, sw), lambda i: (i, ub)),
                  pl.BlockSpec((tm, sw), lambda i: (i, ub + 1)),
                  pl.BlockSpec((tm, sw), lambda i: (i, 0)), vec, wblk, bblk],
        out_specs=(pl.BlockSpec((tm, 2 * sw), lambda i: (i, 0)), wblk, bblk, vec),
        scratch_shapes=[pltpu.VMEM((tm, sw), F32)])(proj, proj, dq, gain, ws, b3)


def _cast_slot_body(me_ref, x_ref, o_ref):
    o_ref[...] = x_ref[...].astype(o_ref.dtype)


def _cast_to_slot(w, layer, me):
    _, r, c = w.shape
    tr = _slab_rows(r)
    spec = pltpu.PrefetchScalarGridSpec(
        num_scalar_prefetch=1, grid=(r // tr,),
        in_specs=[pl.BlockSpec((None, tr, c), lambda i, me_ref: (layer, i, 0))],
        out_specs=pl.BlockSpec((None, tr, c), lambda i, me_ref: (me_ref[0], i, 0)))
    return _call(functools.partial(_cast_slot_body), name="cast_to_slot", out_shape=_sds((N_DEV, r, c), BF16),
                 grid_spec=spec)(me, w)


def _mesh_position():
    x, y, c = (lax.axis_index(a) for a in MESH_AXES)
    return x, y, c


def _peers(x, y, c):
    out = []
    for m in range(1, N_DEV):
        px = 1 - x if m & 4 else x
        py = 1 - y if m & 2 else y
        pc = 1 - c if m & 1 else c
        out.append(((px, py, pc), 4 * px + 2 * py + pc))
    return out


def _remote(src, dst, send, recv, device):
    return pltpu.make_async_remote_copy(src_ref=src, dst_ref=dst, send_sem=send, recv_sem=recv,
                                        device_id=device, device_id_type=pl.DeviceIdType.MESH)


def _all_gather_body(src, out, send, recv, loc):
    x, y, c = _mesh_position()
    me = 4 * x + 2 * y + c
    peers = _peers(x, y, c)
    own = pltpu.make_async_copy(src, out.at[me], loc)
    own.start()
    for j, (peer, _) in enumerate(peers):
        _remote(src, out.at[me], send.at[j], recv.at[j], peer).start()
    for j, (peer, peer_idx) in enumerate(peers):
        cp = _remote(src, out.at[peer_idx], send.at[j], recv.at[j], peer)
        cp.wait_recv()
        cp.wait_send()
    own.wait()


def _all_gather(part, name):
    return _call(
        functools.partial(_all_gather_body), name=name, out_shape=_sds((N_DEV,) + part.shape, part.dtype),
        in_specs=[ANY_SPEC], out_specs=ANY_SPEC,
        scratch_shapes=[pltpu.SemaphoreType.DMA((N_PEERS,)), pltpu.SemaphoreType.DMA((N_PEERS,)),
                        pltpu.SemaphoreType.DMA(())])(part)


def _other_chips(x, y):
    return [(1 - x, y), (x, 1 - y), (1 - x, 1 - y)]


def _gather_send_own(bufs, send, recv):
    x, y, c = _mesh_position()
    me = 4 * x + 2 * y + c
    targets = [(x, y, 1 - c)] + [(px, py, c) for px, py in _other_chips(x, y)]
    for k, buf in enumerate(bufs):
        for j, dev in enumerate(targets):
            _remote(buf.at[me], buf.at[me], send.at[4 * k + j], recv.at[4 * k + j], dev).start()


def _gather_await_own(bufs, send, recv):
    x, y, c = _mesh_position()
    me = 4 * x + 2 * y + c
    targets = [(x, y, 1 - c)] + [(px, py, c) for px, py in _other_chips(x, y)]
    for k, buf in enumerate(bufs):
        for j, dev in enumerate(targets):
            cp = _remote(buf.at[me], buf.at[4 * dev[0] + 2 * dev[1] + dev[2]], send.at[4 * k + j],
                         recv.at[4 * k + j], dev)
            cp.wait_recv()
            cp.wait_send()


def _gather_relay_send(bufs, send, recv):
    x, y, c = _mesh_position()
    for k, buf in enumerate(bufs):
        for j, (px, py) in enumerate(_other_chips(x, y)):
            blk = buf.at[4 * px + 2 * py + c]
            _remote(blk, blk, send.at[3 * k + j], recv.at[3 * k + j], (x, y, 1 - c)).start()


def _gather_relay_await(bufs, send, recv):
    x, y, c = _mesh_position()
    for k, buf in enumerate(bufs):
        for j, (px, py) in enumerate(_other_chips(x, y)):
            cp = _remote(buf.at[4 * px + 2 * py + c], buf.at[4 * px + 2 * py + 1 - c], send.at[3 * k + j],
                         recv.at[3 * k + j], (x, y, 1 - c))
            cp.wait_recv()
            cp.wait_send()


def _reduce_pair_send(grads, stage, send, recv):
    x, y, c = _mesh_position()
    for k, (g, s) in enumerate(zip(grads, stage)):
        for q in range(4):
            _remote(g.at[2 * q + 1 - c], s.at[q], send.at[4 * k + q], recv.at[4 * k + q], (x, y, 1 - c)).start()


def _reduce_pair_await(grads, stage, send, recv):
    x, y, c = _mesh_position()
    for k, (g, s) in enumerate(zip(grads, stage)):
        for q in range(4):
            cp = _remote(g.at[2 * q + 1 - c], s.at[q], send.at[4 * k + q], recv.at[4 * k + q], (x, y, 1 - c))
            cp.wait_recv()
            cp.wait_send()


def _reduce_chip_send(sums, stage, send, recv):
    x, y, c = _mesh_position()
    for k, (p, s) in enumerate(zip(sums, stage)):
        for j, (px, py) in enumerate(_other_chips(x, y)):
            _remote(p.at[2 * px + py], s.at[2 * x + y], send.at[3 * k + j], recv.at[3 * k + j],
                    (px, py, c)).start()


def _reduce_chip_await(sums, stage, send, recv):
    x, y, c = _mesh_position()
    for k, (p, s) in enumerate(zip(sums, stage)):
        for j, (px, py) in enumerate(_other_chips(x, y)):
            cp = _remote(p.at[2 * px + py], s.at[2 * px + py], send.at[3 * k + j], recv.at[3 * k + j],
                         (px, py, c))
            cp.wait_recv()
            cp.wait_send()


def _split_body(*refs, n, n_sem_in, n_after, n_sem_out, steps, token):
    bufs = refs[:n]
    sems = refs[n:n + n_sem_in] + refs[n + n_sem_in + n_after:n + n_sem_in + n_after + n_sem_out]
    for step, lo in steps:
        step(bufs, sems[lo], sems[lo + 1])
    if token:
        refs[-1][...] = jnp.zeros_like(refs[-1])


def _split_call(steps, *, name, bufs, sems_in=(), sems_out=(), after=(), token=True):
    n = len(bufs)
    operands = [pltpu.with_memory_space_constraint(b, pltpu.HBM) for b in bufs] + list(sems_in) + list(after)
    in_specs = [HBM_SPEC] * n + [SEM_SPEC] * len(sems_in) + [ANY_SPEC] * len(after)
    out_shape = list(sems_out) + [pltpu.HBM(b.shape, b.dtype) for b in bufs]
    out_specs = [SEM_SPEC] * len(sems_out) + [HBM_SPEC] * n
    if token:
        out_shape.append(_sds((8, 128), F32))
        out_specs.append(pl.BlockSpec(memory_space=pltpu.VMEM))
    body = functools.partial(_split_body, n=n, n_sem_in=len(sems_in), n_after=len(after),
                             n_sem_out=len(sems_out), steps=tuple(steps), token=token)
    return pl.pallas_call(
        body, name=name, in_specs=in_specs, out_specs=tuple(out_specs), out_shape=tuple(out_shape),
        input_output_aliases={i: len(sems_out) + i for i in range(n)},
        compiler_params=pltpu.CompilerParams(
            has_side_effects=pltpu.SideEffectType.DATAFLOW_SIDE_EFFECTING))(*operands)


def _sem_pair(n, m):
    return [pltpu.SemaphoreType.DMA((n * m,)), pltpu.SemaphoreType.DMA((n * m,))]


def _gather_start(bufs, after, tag):
    n = len(bufs)
    out = _split_call([(_gather_send_own, 0)], name="gather_start_" + tag, bufs=bufs,
                      sems_out=_sem_pair(n, 4), after=after)
    return out[0], out[1], list(out[2:2 + n]), out[-1]


def _gather_relay(state, after, tag):
    send, recv, bufs, _ = state
    n = len(bufs)
    out = _split_call([(_gather_await_own, 0), (_gather_relay_send, 2)], name="gather_relay_" + tag, bufs=bufs,
                      sems_in=[send, recv], sems_out=_sem_pair(n, 3), after=after)
    return out[0], out[1], list(out[2:2 + n]), out[-1]


def _gather_finish(state, after, tag):
    send, recv, bufs, _ = state
    out = _split_call([(_gather_relay_await, 0)], name="gather_finish_" + tag, bufs=bufs,
                      sems_in=[send, recv], after=after, token=False)
    return list(out)


def _pair_arrays(n):
    def split(step):
        return lambda bufs, send, recv: step(bufs[:n], bufs[n:], send, recv)
    return split


def _reduce_pairs_start(grads, after, tag):
    n = len(grads)
    stage = [lax.empty((4,) + g.shape[1:], g.dtype) for g in grads]
    out = _split_call([(_pair_arrays(n)(_reduce_pair_send), 0)], name="reduce_pairs_start_" + tag,
                      bufs=list(grads) + stage, sems_out=_sem_pair(n, 4), after=after)
    return out[0], out[1], list(out[2:2 + n]), list(out[2 + n:2 + 2 * n]), out[-1]


def _reduce_pairs_finish(state, after, tag):
    send, recv, grads, stage, _ = state
    n = len(grads)
    out = _split_call([(_pair_arrays(n)(_reduce_pair_await), 0)], name="reduce_pairs_finish_" + tag,
                      bufs=list(grads) + list(stage), sems_in=[send, recv], after=after, token=False)
    return list(out[:n]), list(out[n:])


def _reduce_chips_start(sums, after, tag):
    n = len(sums)
    stage = [lax.empty(p.shape, p.dtype) for p in sums]
    out = _split_call([(_pair_arrays(n)(_reduce_chip_send), 0)], name="reduce_chips_start_" + tag,
                      bufs=list(sums) + stage, sems_out=_sem_pair(n, 3), after=after)
    return out[0], out[1], list(out[2:2 + n]), list(out[2 + n:2 + 2 * n]), out[-1]


def _reduce_chips_finish(state, after, tag):
    send, recv, sums, stage, _ = state
    n = len(sums)
    out = _split_call([(_pair_arrays(n)(_reduce_chip_await), 0)], name="reduce_chips_finish_" + tag,
                      bufs=list(sums) + list(stage), sems_in=[send, recv], after=after, token=False)
    return list(out[:n]), list(out[n:])


def _pair_sum_body(c_ref, a_ref, b_ref, o_ref):
    o_ref[...] = (a_ref[...].astype(F32) + b_ref[...].astype(F32)).astype(o_ref.dtype)


def _pair_sum(grad, stage, core):
    _, r, c = grad.shape
    tr = _slab_rows(r)
    blk = pl.BlockSpec((None, tr, c), lambda q, i, c_ref: (q, i, 0))
    spec = pltpu.PrefetchScalarGridSpec(
        num_scalar_prefetch=1, grid=(4, r // tr),
        in_specs=[pl.BlockSpec((None, tr, c), lambda q, i, c_ref: (2 * q + c_ref[0], i, 0)), blk],
        out_specs=blk)
    return _call(functools.partial(_pair_sum_body), name="pair_sum", out_shape=_sds((4, r, c), grad.dtype),
                 grid_spec=spec)(core, grad, stage)


def _adamw_math(w, g, m, v):
    m = ADAM_B1 * m + (1.0 - ADAM_B1) * g
    v = ADAM_B2 * v + (1.0 - ADAM_B2) * (g * g)
    m_hat = m / (1.0 - ADAM_B1 ** ADAM_STEP)
    v_hat = v / (1.0 - ADAM_B2 ** ADAM_STEP)
    delta = -ADAM_LR * (m_hat / (jnp.sqrt(v_hat) + ADAM_EPS) + ADAM_WD * w)
    return delta, m, v


def _adamw_body(q_ref, own_ref, r_ref, w_ref, m_ref, v_ref, *rest):
    g_out, d_out, m_out, v_out = rest[-4:]
    q = q_ref[0]
    g = None
    for j in range(4):
        term = jnp.where(q == j, own_ref[...], r_ref[j]).astype(F32)
        g = term if g is None else g + term
    delta, m, v = _adamw_math(w_ref[...], g, m_ref[...], v_ref[...])
    g_out[...] = g
    d_out[...] = delta
    m_out[...] = m
    v_out[...] = v


def _adamw_layer(sums, recv, w, m, v, prev, layer, chip):
    l, r, c = w.shape
    tr = _slab_rows(r)
    lay = pl.BlockSpec((None, tr, c), lambda i, q_ref: (layer, i, 0))
    spec = pltpu.PrefetchScalarGridSpec(
        num_scalar_prefetch=1, grid=(r // tr,),
        in_specs=[pl.BlockSpec((None, tr, c), lambda i, q_ref: (q_ref[0], i, 0)),
                  pl.BlockSpec((4, tr, c), lambda i, q_ref: (0, i, 0)), lay, lay, lay] + [ANY_SPEC] * 4,
        out_specs=(lay, lay, lay, lay))
    out = _sds((l, r, c), F32)
    return _call(
        functools.partial(_adamw_body), name="adamw", out_shape=(out, out, out, out), grid_spec=spec,
        aliases={6: 0, 7: 1, 8: 2, 9: 3})(chip, sums, recv, w, m, v, *prev)


def _adamw_small_body(r_ref, w_ref, m_ref, v_ref, g_out, d_out, m_out, v_out):
    g = r_ref[0]
    for j in range(1, N_DEV):
        g = g + r_ref[j]
    delta, m, v = _adamw_math(w_ref[...], g, m_ref[...], v_ref[...])
    g_out[...] = g
    d_out[...] = delta
    m_out[...] = m
    v_out[...] = v


def _adamw_small(recv, w, m, v):
    r, c = w.shape
    tr = 8
    for cand in (1024, 512, 256, 128, 64, 32, 16, 8):
        if r % cand == 0:
            tr = cand
            break
    blk = pl.BlockSpec((tr, c), lambda i: (i, 0))
    out = _sds((r, c), F32)
    return _call(
        functools.partial(_adamw_small_body), name="adamw_small", out_shape=(out, out, out, out),
        grid=(r // tr,),
        in_specs=[pl.BlockSpec((N_DEV, tr, c), lambda i: (0, i, 0)), blk, blk, blk],
        out_specs=(blk, blk, blk, blk))(recv, w, m, v)


def _as3d(a):
    return a.reshape(a.shape[0], -1, a.shape[-1])


def _ffn_forward(x, g_pre, wu, wd, g_post, after=(), mid=None):
    h = _rms_fwd(x, g_pre, after=after)
    gu, a = _ffn_up(h, wu)
    x_out, f = _down_norm_res(a, wd, x, g_post, MACARON_WEIGHT, after=mid(a) if mid else ())
    return x_out, (x, h, gu, a, f)


def _ffn_backward(dy, saved, g_pre, wu, wd, g_post, after=(), mid=None):
    x, h, gu, a, f = saved
    df, dg_post = _norm_res_bwd(dy, f, g_post, MACARON_WEIGHT, after=after)
    dz = _dgrad_ffn(df, wd, gu)
    dwd = _wgrad_rows(a, df, after=mid(dz) if mid else ())
    dwu = _wgrad_cs(h, dz)
    dy, dg_pre = _dgrad_cs_norm(dz, wu, x, g_pre, dy)
    return dy, dg_pre, dwu, dwd, dg_post


def kernel(x, g_ffn1_pre, w_ffn1_up, w_ffn1_down, g_ffn1_post, g_mix_pre, w_in, pool_group_w, pool_scale, w_pool_out, sgu_v_gain, sgu_w_s, sgu_b_s, w_sgu_out, w_out, g_mix_post, g_ffn2_pre, w_ffn2_up, w_ffn2_down, g_ffn2_post, loss_target, m_g_ffn1_pre, m_w_ffn1_up, m_w_ffn1_down, m_g_ffn1_post, m_g_mix_pre, m_w_in, m_pool_group_w, m_pool_scale, m_w_pool_out, m_sgu_v_gain, m_sgu_w_s, m_sgu_b_s, m_w_sgu_out, m_w_out, m_g_mix_post, m_g_ffn2_pre, m_w_ffn2_up, m_w_ffn2_down, m_g_ffn2_post, v_g_ffn1_pre, v_w_ffn1_up, v_w_ffn1_down, v_g_ffn1_post, v_g_mix_pre, v_w_in, v_pool_group_w, v_pool_scale, v_w_pool_out, v_sgu_v_gain, v_sgu_w_s, v_sgu_b_s, v_w_sgu_out, v_w_out, v_g_mix_post, v_g_ffn2_pre, v_w_ffn2_up, v_w_ffn2_down, v_g_ffn2_post):
    given = dict(locals())
    weights = {n: given[n] for n in ORDER}
    mom_m = {n: given["m_" + n] for n in ORDER}
    mom_v = {n: given["v_" + n] for n in ORDER}
    depth = g_ffn1_pre.shape[0]
    d_model = x.shape[-1]
    pool_w = pool_scale.shape[1]
    sgu_w = sgu_v_gain.shape[1]
    groups, grp_rows, grp_c = pool_group_w.shape[1:]
    gate_off = pool_w + 2 * sgu_w

    xs = x.reshape(x.shape[-2], d_model)
    target = loss_target.reshape(xs.shape)

    px, py, pc = _mesh_position()
    me = (4 * px + 2 * py + pc).astype(jnp.int32).reshape(1)
    core = pc.astype(jnp.int32).reshape(1)
    chip = (2 * px + py).astype(jnp.int32).reshape(1)

    def start_gather(i, after):
        bufs = [_cast_to_slot(_as3d(weights[n]), i, me) for n in BIG]
        return _gather_start(bufs, after, "l%d" % i)

    def vec(name, i):
        return weights[name][i].reshape(1, -1)

    def layer_weights(bufs, i):
        gw = dict(zip(BIG, bufs))
        wgrp = gw["pool_group_w"].reshape(N_DEV, groups, grp_rows, grp_c).transpose(1, 0, 2, 3)
        return dict(
            wu1=gw["w_ffn1_up"], wd1=gw["w_ffn1_down"].reshape(-1, d_model), win=gw["w_in"],
            wgrp=wgrp.reshape(groups, grp_c, grp_c), wpo=gw["w_pool_out"], wso=gw["w_sgu_out"],
            wout=gw["w_out"].reshape(-1, d_model), wu2=gw["w_ffn2_up"],
            wd2=gw["w_ffn2_down"].reshape(-1, d_model),
            ws=weights["sgu_w_s"][i], b3=weights["sgu_b_s"][i][:, :, None])

    saved = []
    cur = xs
    state = _gather_relay(start_gather(0, ()), (), "l0")
    bufs = _gather_finish(state, (), "l0")
    for i in range(depth):
        lw = layer_weights(bufs, i)
        nxt = i + 1 < depth
        if nxt:
            state = start_gather(i + 1, (bufs[0],))
        cur, s1 = _ffn_forward(cur, vec("g_ffn1_pre", i), lw["wu1"], lw["wd1"], vec("g_ffn1_post", i),
                               after=(state[-1],) if nxt else ())
        x1 = cur
        h2 = _rms_fwd(x1, vec("g_mix_pre", i))
        proj = _mm_cs(h2, lw["win"])
        yps = _pool_fwd(proj, lw["wgrp"], vec("pool_scale", i))
        q = _sgu_fwd(proj, pool_w, vec("sgu_v_gain", i), lw["ws"], lw["b3"])
        m, ya, yb = _mix_gate(yps, q, proj, gate_off, lw["wpo"], lw["wso"])
        cur, o = _down_norm_res(m, lw["wout"], x1, vec("g_mix_post", i), 1.0)
        s2 = (x1, h2, proj, yps, q, ya, yb, m, o)
        relayed = []

        def relay(a, i=i, relayed=relayed):
            relayed.append(_gather_relay(state, (a,), "l%d" % (i + 1)))
            return (relayed[0][-1],)

        cur, s3 = _ffn_forward(cur, vec("g_ffn2_pre", i), lw["wu2"], lw["wd2"], vec("g_ffn2_post", i),
                               mid=relay if nxt else None)
        saved.append((lw, s1, s2, s3))
        if nxt:
            bufs = _gather_finish(relayed[0], (cur,), "l%d" % (i + 1))

    dy, sq = _loss_grad(cur, target)
    loss = lax.psum(0.5 * sq[0, 0] / d_model, MESH_AXES)

    small_grads = {n: [None] * depth for n in SMALL}
    outs = {n: None for n in BIG}

    def chip_sums(pending, after, tag):
        grads, stage = _reduce_pairs_finish(pending, after, tag)
        sums = [_pair_sum(g, s, core) for g, s in zip(grads, stage)]
        return _reduce_chips_start(sums, (), tag)

    def update(chips, after, layer):
        sums, recv = _reduce_chips_finish(chips, after, "l%d" % layer)
        for n, p, r in zip(BIG, sums, recv):
            w3, m3, v3 = _as3d(weights[n]), _as3d(mom_m[n]), _as3d(mom_v[n])
            prev = outs[n] if outs[n] is not None else [lax.empty(w3.shape, F32) for _ in range(4)]
            outs[n] = _adamw_layer(p, r, w3, m3, v3, prev, layer, chip)

    pending = None
    for i in reversed(range(depth)):
        lw, s1, s2, s3 = saved[i]
        chips = []

        def mid(dz, i=i, chips=chips):
            chips.append(chip_sums(pending, (dz,), "l%d" % (i + 1)))
            return (chips[0][-1],)

        dy, dg, dwu2, dwd2, dgp = _ffn_backward(
            dy, s3, vec("g_ffn2_pre", i), lw["wu2"], lw["wd2"], vec("g_ffn2_post", i),
            after=(pending[-1],) if pending else (), mid=mid if pending else None)
        small_grads["g_ffn2_pre"][i], small_grads["g_ffn2_post"][i] = dg, dgp

        x1, h2, proj, yps, q, ya, yb, m, o = s2
        do, small_grads["g_mix_post"][i] = _norm_res_bwd(dy, o, vec("g_mix_post", i), 1.0)
        dya, dyb, dga, dgb = _dgrad_gate(do, lw["wout"], ya, yb, proj, gate_off)
        dwout = _wgrad_rows(m, do)
        dwpo = _wgrad_cs(yps, dya[None])
        dwso = _wgrad_cs(q, dyb[None])
        dyps = _dgrad_cs(dya[None], lw["wpo"])
        dq = _dgrad_cs(dyb[None], lw["wso"])
        dp, dwgrp, small_grads["pool_scale"][i] = _pool_bwd(proj, dyps, lw["wgrp"], vec("pool_scale", i))
        duv, dws, db3, small_grads["sgu_v_gain"][i] = _sgu_bwd(
            proj, pool_w, dq, vec("sgu_v_gain", i), lw["ws"], lw["b3"])
        small_grads["sgu_w_s"][i] = dws
        small_grads["sgu_b_s"][i] = db3
        dproj = jnp.concatenate([dp, duv, dga, dgb], axis=1)[None]
        dwin = _wgrad_cs(h2, dproj)
        dy, small_grads["g_mix_pre"][i] = _dgrad_cs_norm(dproj, lw["win"], x1, vec("g_mix_pre", i), dy)

        dy, dg, dwu1, dwd1, dgp = _ffn_backward(dy, s1, vec("g_ffn1_pre", i), lw["wu1"], lw["wd1"],
                                                vec("g_ffn1_post", i))
        small_grads["g_ffn1_pre"][i], small_grads["g_ffn1_post"][i] = dg, dgp

        dwgrp8 = dwgrp.reshape(groups, N_DEV, grp_rows, grp_c).transpose(1, 0, 2, 3)
        full = dict(
            w_ffn1_up=dwu1, w_ffn1_down=dwd1.reshape(N_DEV, -1, d_model), w_in=dwin,
            pool_group_w=dwgrp8.reshape(N_DEV, groups * grp_rows, grp_c).astype(BF16),
            w_pool_out=dwpo, w_sgu_out=dwso, w_out=dwout.reshape(N_DEV, -1, d_model),
            w_ffn2_up=dwu2, w_ffn2_down=dwd2.reshape(N_DEV, -1, d_model))
        if chips:
            update(chips[0], (dy,), i + 1)
        pending = _reduce_pairs_start([full[n] for n in BIG], (), "l%d" % i)
    update(chip_sums(pending, (), "l0"), (), 0)

    def pack(parts):
        return jnp.concatenate([p.reshape(depth, -1) for p in parts], axis=1).reshape(-1, 128)

    part = pack([jnp.stack(small_grads[n]) for n in SMALL])
    all_parts = _all_gather(part, "gather_small_grads")
    small_out = _adamw_small(all_parts, pack([weights[n] for n in SMALL]), pack([mom_m[n] for n in SMALL]),
                             pack([mom_v[n] for n in SMALL]))

    results = {}
    sizes = [weights[n][0].size for n in SMALL]
    for kind, packed in zip(("grad", "delta", "new_m", "new_v"), small_out):
        flat = packed.reshape(depth, -1)
        off = 0
        for n, sz in zip(SMALL, sizes):
            results[(kind, n)] = flat[:, off:off + sz].reshape(weights[n].shape)
            off += sz
    for n in BIG:
        for kind, arr in zip(("grad", "delta", "new_m", "new_v"), outs[n]):
            results[(kind, n)] = arr.reshape(weights[n].shape)

    grad_x = dy.reshape(x.shape)
    return (loss, grad_x, *[results[("grad", n)] for n in ORDER], *[results[("delta", n)] for n in ORDER],
            *[results[("new_m", n)] for n in ORDER], *[results[("new_v", n)] for n in ORDER])
```

```python
import functools

import jax
import jax.numpy as jnp
from jax import lax
from jax.experimental import pallas as pl
from jax.experimental.pallas import tpu as pltpu

F32 = jnp.float32
BF16 = jnp.bfloat16

N_DEV = 8
MESH_AXES = ("x", "y", "c")
N_PEERS = N_DEV - 1

EPS = 1e-6
MACARON_WEIGHT = 0.5
N_POOL_GROUPS = 4
POOL_HALO = 16
SGU_HEAD = 128
SGU_CHUNK = 128
GELU_C = 0.7978845608028654
GELU_A = 0.044715

ADAM_LR = 0.001
ADAM_B1 = 0.9
ADAM_B2 = 0.999
ADAM_EPS = 1e-08
ADAM_WD = 0.01
ADAM_STEP = 10

ROW_TILE = 512
VMEM_LIMIT_BYTES = 56 * 1024 * 1024

BIG = ("w_ffn1_up", "w_ffn1_down", "w_in", "pool_group_w", "w_pool_out", "w_sgu_out", "w_out",
       "w_ffn2_up", "w_ffn2_down")
SMALL = ("g_ffn1_pre", "g_ffn1_post", "g_mix_pre", "pool_scale", "sgu_v_gain", "sgu_w_s", "sgu_b_s",
         "g_mix_post", "g_ffn2_pre", "g_ffn2_post")
ORDER = ("g_ffn1_pre", "w_ffn1_up", "w_ffn1_down", "g_ffn1_post", "g_mix_pre", "w_in", "pool_group_w",
         "pool_scale", "w_pool_out", "sgu_v_gain", "sgu_w_s", "sgu_b_s", "w_sgu_out", "w_out",
         "g_mix_post", "g_ffn2_pre", "w_ffn2_up", "w_ffn2_down", "g_ffn2_post")


ANY_SPEC = pl.BlockSpec(memory_space=pl.ANY)
HBM_SPEC = pl.BlockSpec(memory_space=pltpu.HBM)
SEM_SPEC = pl.BlockSpec(memory_space=pltpu.SEMAPHORE)


def _skip_refs(body, n_in, n_skip, *refs):
    return body(*refs[:n_in], *refs[n_in + n_skip:])


def _call(body, *, name, out_shape, grid=(), in_specs=None, out_specs=None, scratch_shapes=(),
          aliases=None, after=(), grid_spec=None):
    extra = {}
    if grid_spec is not None:
        extra["grid_spec"] = grid_spec
    else:
        extra.update(in_specs=list(in_specs) + [ANY_SPEC] * len(after), out_specs=out_specs,
                     scratch_shapes=list(scratch_shapes))
        if grid:
            extra["grid"] = grid
        if after:
            body = functools.partial(_skip_refs, body, len(in_specs), len(after))
    if aliases:
        extra["input_output_aliases"] = aliases
    call = pl.pallas_call(
        body, name=name, out_shape=out_shape,
        compiler_params=pltpu.CompilerParams(vmem_limit_bytes=VMEM_LIMIT_BYTES), **extra)
    if after:
        return lambda *args: call(*args, *after)
    return call


def _sds(shape, dtype):
    return jax.ShapeDtypeStruct(tuple(shape), dtype)


def _dot(a, b):
    return jnp.dot(a, b, preferred_element_type=F32)


def _dot_nt(a, b):
    return lax.dot_general(a, b, (((1,), (1,)), ((), ())), preferred_element_type=F32)


def _dot_tn(a, b):
    return lax.dot_general(a, b, (((0,), (0,)), ((), ())), preferred_element_type=F32)


def _sigmoid(x):
    return 1.0 / (1.0 + jnp.exp(-x))


def _gelu(x):
    return 0.5 * x * (1.0 + jnp.tanh(GELU_C * (x + GELU_A * x * x * x)))


def _gelu_grad(x):
    t = jnp.tanh(GELU_C * (x + GELU_A * x * x * x))
    return 0.5 * (1.0 + t) + 0.5 * x * (1.0 - t * t) * (GELU_C * (1.0 + 3.0 * GELU_A * x * x))


def _rstd(x):
    return lax.rsqrt(jnp.mean(x * x, axis=-1, keepdims=True) + EPS)


def _row_tile(t):
    return min(t, ROW_TILE)


def _slab_rows(r):
    for cand in (256, 128, 64, 32, 16):
        if r % cand == 0:
            return cand
    return r


def _rms_fwd_body(x_ref, g_ref, h_ref):
    x = x_ref[...]
    h_ref[...] = ((x * _rstd(x)) * g_ref[...]).astype(h_ref.dtype)


def _rms_fwd(x, g, after=()):
    t, d = x.shape
    tm = _row_tile(t)
    return _call(
        functools.partial(_rms_fwd_body), name="rms_fwd", out_shape=_sds((t, d), BF16),
        grid=(t // tm,),
        in_specs=[pl.BlockSpec((tm, d), lambda i: (i, 0)), pl.BlockSpec((1, d), lambda i: (0, 0))],
        out_specs=pl.BlockSpec((tm, d), lambda i: (i, 0)), after=after)(x, g)


def _mm_cs_body(h_ref, w_ref, o_ref):
    o_ref[...] = _dot(h_ref[...], w_ref[...]).astype(o_ref.dtype)


def _mm_cs(h, wg):
    t, k = h.shape
    n = wg.shape[2]
    tm = _row_tile(t)
    return _call(
        functools.partial(_mm_cs_body), name="mm_cs", out_shape=_sds((t, N_DEV * n), BF16),
        grid=(N_DEV, t // tm),
        in_specs=[pl.BlockSpec((tm, k), lambda j, i: (i, 0)),
                  pl.BlockSpec((None, k, n), lambda j, i: (j, 0, 0))],
        out_specs=pl.BlockSpec((tm, n), lambda j, i: (i, j)))(h, wg)


def _ffn_up_body(h_ref, wg_ref, wu_ref, gu_ref, a_ref):
    h = h_ref[...]
    g = _dot(h, wg_ref[...])
    u = _dot(h, wu_ref[...])
    gu_ref[0] = g.astype(gu_ref.dtype)
    gu_ref[1] = u.astype(gu_ref.dtype)
    a_ref[...] = (g * _sigmoid(g) * u).astype(a_ref.dtype)


def _ffn_up(h, wg):
    t, k = h.shape
    n = wg.shape[2]
    half = N_DEV // 2
    f = half * n
    tm = _row_tile(t)
    return _call(
        functools.partial(_ffn_up_body), name="ffn_up",
        out_shape=(_sds((2, t, f), BF16), _sds((t, f), BF16)),
        grid=(half, t // tm),
        in_specs=[pl.BlockSpec((tm, k), lambda j, i: (i, 0)),
                  pl.BlockSpec((None, k, n), lambda j, i: (j, 0, 0)),
                  pl.BlockSpec((None, k, n), lambda j, i: (j + N_DEV // 2, 0, 0))],
        out_specs=(pl.BlockSpec((2, tm, n), lambda j, i: (0, i, j)),
                   pl.BlockSpec((tm, n), lambda j, i: (i, j))))(h, wg, wg)


def _down_body(a_ref, w_ref, x_ref, g_ref, xo_ref, f_ref, acc_ref, *, scale, nk):
    k = pl.program_id(1)

    @pl.when(k == 0)
    def _():
        acc_ref[...] = jnp.zeros_like(acc_ref)

    acc_ref[...] += _dot(a_ref[...], w_ref[...])

    @pl.when(k == nk - 1)
    def _():
        f = acc_ref[...]
        f_ref[...] = f.astype(f_ref.dtype)
        xo_ref[...] = x_ref[...] + scale * ((f * _rstd(f)) * g_ref[...])


def _down_norm_res(a, w, x, g, scale, after=()):
    t, kk = a.shape
    d = w.shape[1]
    tm = _row_tile(t)
    tk = 512 if kk % 512 == 0 else 256
    nk = kk // tk
    return _call(
        functools.partial(_down_body, scale=scale, nk=nk), name="down_norm_res",
        out_shape=(_sds((t, d), F32), _sds((t, d), BF16)),
        grid=(t // tm, nk),
        in_specs=[pl.BlockSpec((tm, tk), lambda i, k: (i, k)),
                  pl.BlockSpec((tk, d), lambda i, k: (k, 0)),
                  pl.BlockSpec((tm, d), lambda i, k: (i, 0)),
                  pl.BlockSpec((1, d), lambda i, k: (0, 0))],
        out_specs=(pl.BlockSpec((tm, d), lambda i, k: (i, 0)),
                   pl.BlockSpec((tm, d), lambda i, k: (i, 0))),
        scratch_shapes=[pltpu.VMEM((tm, d), F32)], after=after)(a, w, x, g)


def _pool_inv_count(g, i, tm):
    t1 = i * tm + lax.broadcasted_iota(jnp.int32, (tm, 1), 0) + 1
    return 1.0 / jnp.minimum(t1, 2 << g).astype(F32)


def _pool_windows(e, g):
    s2 = e + pltpu.roll(e, 1, 0)
    s4 = s2 + pltpu.roll(s2, 2, 0)
    s8 = s4 + pltpu.roll(s4, 4, 0)
    s16 = s8 + pltpu.roll(s8, 8, 0)
    return jnp.where(g == 0, s2, jnp.where(g == 1, s4, jnp.where(g == 2, s8, s16)))


def _pool_delta(p_ref, ph_ref, g, i, tm):
    pm = p_ref[...].astype(F32)
    ph = jnp.where(i == 0, 0.0, ph_ref[...].astype(F32))
    e = jnp.concatenate([ph, pm], axis=0)
    s = _pool_windows(e, g)[POOL_HALO:]
    return s * _pool_inv_count(g, i, tm) - pm


def _pool_fwd_body(p_ref, ph_ref, w_ref, sc_ref, o_ref, *, tm):
    g = pl.program_id(0)
    i = pl.program_id(1)
    d = _pool_delta(p_ref, ph_ref, g, i, tm)
    o_ref[...] = (_dot(d.astype(BF16), w_ref[...]) * sc_ref[...]).astype(o_ref.dtype)


def _pool_specs(t, c, tm):
    hb = tm // POOL_HALO
    main = pl.BlockSpec((tm, c), lambda g, i: (i, g))
    halo_before = pl.BlockSpec((POOL_HALO, c), lambda g, i: (jnp.maximum(i * hb - 1, 0), g))
    return main, halo_before


def _pool_fwd(proj, wgrp, scale):
    t = proj.shape[0]
    c = wgrp.shape[1]
    tm = _row_tile(t)
    main, halo = _pool_specs(t, c, tm)
    return _call(
        functools.partial(_pool_fwd_body, tm=tm), name="pool_fwd",
        out_shape=_sds((t, N_POOL_GROUPS * c), BF16), grid=(N_POOL_GROUPS, t // tm),
        in_specs=[main, halo, pl.BlockSpec((None, c, c), lambda g, i: (g, 0, 0)),
                  pl.BlockSpec((1, c), lambda g, i: (0, g))],
        out_specs=pl.BlockSpec((tm, c), lambda g, i: (i, g)))(proj, proj, wgrp, scale)


def _tril_mask():
    r = lax.broadcasted_iota(jnp.int32, (SGU_CHUNK, SGU_CHUNK), 0)
    c = lax.broadcasted_iota(jnp.int32, (SGU_CHUNK, SGU_CHUNK), 1)
    return r >= c


def _tril_weights(ws_ref, h):
    return jnp.where(_tril_mask(), ws_ref[h], 0.0)


def _sgu_fwd_body(u_ref, v_ref, gain_ref, ws_ref, b_ref, q_ref, *, tm, heads):
    ug = _gelu(u_ref[...].astype(F32))
    vg = _gelu(v_ref[...].astype(F32))
    vn = ((vg * _rstd(vg)) * gain_ref[...]).astype(BF16)
    for h in range(heads):
        wt = _tril_weights(ws_ref, h).astype(BF16)
        cols = slice(h * SGU_HEAD, (h + 1) * SGU_HEAD)
        for c in range(tm // SGU_CHUNK):
            rows = slice(c * SGU_CHUNK, (c + 1) * SGU_CHUNK)
            sg = _dot(wt, vn[rows, cols]) + b_ref[h]
            q_ref[rows, cols] = (ug[rows, cols] * sg).astype(q_ref.dtype)


def _sgu_fwd(proj, pw, gain, ws, b3):
    t = proj.shape[0]
    sw = gain.shape[1]
    heads = sw // SGU_HEAD
    tm = _row_tile(t)
    ub = pw // sw
    return _call(
        functools.partial(_sgu_fwd_body, tm=tm, heads=heads), name="sgu_fwd",
        out_shape=_sds((t, sw), BF16), grid=(t // tm,),
        in_specs=[pl.BlockSpec((tm, sw), lambda i: (i, ub)),
                  pl.BlockSpec((tm, sw), lambda i: (i, ub + 1)),
                  pl.BlockSpec((1, sw), lambda i: (0, 0)),
                  pl.BlockSpec((heads, SGU_CHUNK, SGU_CHUNK), lambda i: (0, 0, 0)),
                  pl.BlockSpec((heads, SGU_CHUNK, 1), lambda i: (0, 0, 0))],
        out_specs=pl.BlockSpec((tm, sw), lambda i: (i, 0)))(proj, proj, gain, ws, b3)


def _mix_gate_body(yps_ref, q_ref, ga_ref, gb_ref, wp_ref, ws_ref, m_ref, ya_ref, yb_ref):
    ya = _dot(yps_ref[...], wp_ref[...])
    yb = _dot(q_ref[...], ws_ref[...])
    ya_ref[...] = ya.astype(ya_ref.dtype)
    yb_ref[...] = yb.astype(yb_ref.dtype)
    m = _sigmoid(ga_ref[...].astype(F32)) * ya + _sigmoid(gb_ref[...].astype(F32)) * yb
    m_ref[...] = m.astype(m_ref.dtype)


def _mix_gate(yps, q, proj, gate_off, wpo, wso):
    t, pw = yps.shape
    sw = q.shape[1]
    n = wpo.shape[2]
    d = N_DEV * n
    tm = _row_tile(t)
    ga0 = gate_off // n
    gb0 = (gate_off + d) // n
    out = _sds((t, d), BF16)
    blk = pl.BlockSpec((tm, n), lambda j, i: (i, j))
    return _call(
        functools.partial(_mix_gate_body), name="mix_gate", out_shape=(out, out, out),
        grid=(N_DEV, t // tm),
        in_specs=[pl.BlockSpec((tm, pw), lambda j, i: (i, 0)),
                  pl.BlockSpec((tm, sw), lambda j, i: (i, 0)),
                  pl.BlockSpec((tm, n), lambda j, i: (i, ga0 + j)),
                  pl.BlockSpec((tm, n), lambda j, i: (i, gb0 + j)),
                  pl.BlockSpec((None, pw, n), lambda j, i: (j, 0, 0)),
                  pl.BlockSpec((None, sw, n), lambda j, i: (j, 0, 0))],
        out_specs=(blk, blk, blk))(yps, q, proj, proj, wpo, wso)


def _loss_body(y_ref, t_ref, dy_ref, l_ref, *, inv_d):
    i = pl.program_id(0)

    @pl.when(i == 0)
    def _():
        l_ref[...] = jnp.zeros_like(l_ref)

    e = y_ref[...] - t_ref[...]
    dy_ref[...] = e * inv_d
    l_ref[...] += jnp.sum(e * e)


def _loss_grad(y, target):
    t, d = y.shape
    tm = _row_tile(t)
    blk = pl.BlockSpec((tm, d), lambda i: (i, 0))
    return _call(
        functools.partial(_loss_body, inv_d=1.0 / d), name="loss_grad",
        out_shape=(_sds((t, d), F32), _sds((8, 128), F32)), grid=(t // tm,),
        in_specs=[blk, blk],
        out_specs=(blk, pl.BlockSpec((8, 128), lambda i: (0, 0))))(y, target)


def _norm_bwd_body(dy_ref, f_ref, g_ref, df_ref, dg_ref, *, scale):
    i = pl.program_id(0)

    @pl.when(i == 0)
    def _():
        dg_ref[...] = jnp.zeros_like(dg_ref)

    f = f_ref[...].astype(F32)
    rstd = _rstd(f)
    fn = f * rstd
    dr = scale * dy_ref[...]
    dg_ref[...] += jnp.sum(dr * fn, axis=0, keepdims=True)
    dfn = dr * g_ref[...]
    df = rstd * (dfn - fn * jnp.mean(dfn * fn, axis=-1, keepdims=True))
    df_ref[...] = df.astype(df_ref.dtype)


def _norm_res_bwd(dy, f, g, scale, after=()):
    t, d = dy.shape
    tm = _row_tile(t)
    blk = pl.BlockSpec((tm, d), lambda i: (i, 0))
    vec = pl.BlockSpec((1, d), lambda i: (0, 0))
    return _call(
        functools.partial(_norm_bwd_body, scale=scale), name="norm_res_bwd",
        out_shape=(_sds((t, d), BF16), _sds((1, d), F32)), grid=(t // tm,),
        in_specs=[blk, blk, vec], out_specs=(blk, vec), after=after)(dy, f, g)


def _dgrad_ffn_body(df_ref, w_ref, gu_ref, dz_ref):
    da = _dot_nt(df_ref[...], w_ref[...])
    g = gu_ref[0].astype(F32)
    u = gu_ref[1].astype(F32)
    sig = _sigmoid(g)
    dz_ref[0] = (da * u * (sig * (1.0 + g * (1.0 - sig)))).astype(dz_ref.dtype)
    dz_ref[1] = (da * (g * sig)).astype(dz_ref.dtype)


def _dgrad_ffn(df, wd, gu):
    t, d = df.shape
    f = wd.shape[0]
    tm = _row_tile(t)
    tn = 512 if f % 512 == 0 else 256
    blk = pl.BlockSpec((2, tm, tn), lambda j, i: (0, i, j))
    return _call(
        functools.partial(_dgrad_ffn_body), name="dgrad_ffn", out_shape=_sds((2, t, f), BF16),
        grid=(f // tn, t // tm),
        in_specs=[pl.BlockSpec((tm, d), lambda j, i: (i, 0)),
                  pl.BlockSpec((tn, d), lambda j, i: (j, 0)), blk],
        out_specs=blk)(df, wd, gu)


def _dgrad_gate_body(do_ref, w_ref, ya_ref, yb_ref, ga_ref, gb_ref, dya_ref, dyb_ref, dga_ref, dgb_ref):
    dm = _dot_nt(do_ref[...], w_ref[...])
    sa = _sigmoid(ga_ref[...].astype(F32))
    sb = _sigmoid(gb_ref[...].astype(F32))
    dya_ref[...] = (dm * sa).astype(dya_ref.dtype)
    dyb_ref[...] = (dm * sb).astype(dyb_ref.dtype)
    dga_ref[...] = (dm * ya_ref[...].astype(F32) * (sa * (1.0 - sa))).astype(dga_ref.dtype)
    dgb_ref[...] = (dm * yb_ref[...].astype(F32) * (sb * (1.0 - sb))).astype(dgb_ref.dtype)


def _dgrad_gate(do, wout, ya, yb, proj, gate_off):
    t, d = do.shape
    tm = _row_tile(t)
    tn = 512 if d % 512 == 0 else 256
    ga0 = gate_off // tn
    gb0 = (gate_off + d) // tn
    blk = pl.BlockSpec((tm, tn), lambda j, i: (i, j))
    out = _sds((t, d), BF16)
    return _call(
        functools.partial(_dgrad_gate_body), name="dgrad_gate", out_shape=(out, out, out, out),
        grid=(d // tn, t // tm),
        in_specs=[pl.BlockSpec((tm, d), lambda j, i: (i, 0)),
                  pl.BlockSpec((tn, d), lambda j, i: (j, 0)), blk, blk,
                  pl.BlockSpec((tm, tn), lambda j, i: (i, ga0 + j)),
                  pl.BlockSpec((tm, tn), lambda j, i: (i, gb0 + j))],
        out_specs=(blk, blk, blk, blk))(do, wout, ya, yb, proj, proj)


def _wgrad_rows_body(a_ref, b_ref, o_ref, acc_ref, *, nt):
    t = pl.program_id(1)

    @pl.when(t == 0)
    def _():
        acc_ref[...] = jnp.zeros_like(acc_ref)

    acc_ref[...] += _dot_tn(a_ref[...], b_ref[...])

    @pl.when(t == nt - 1)
    def _():
        o_ref[...] = acc_ref[...].astype(o_ref.dtype)


def _wgrad_rows(a, b, after=()):
    t, k = a.shape
    n = b.shape[1]
    tt = _row_tile(t)
    tk = k // 4
    nt = t // tt
    return _call(
        functools.partial(_wgrad_rows_body, nt=nt), name="wgrad_rows", out_shape=_sds((k, n), BF16),
        grid=(k // tk, nt),
        in_specs=[pl.BlockSpec((tt, tk), lambda r, s: (s, r)),
                  pl.BlockSpec((tt, n), lambda r, s: (s, 0))],
        out_specs=pl.BlockSpec((tk, n), lambda r, s: (r, 0)),
        scratch_shapes=[pltpu.VMEM((tk, n), F32)], after=after)(a, b)


def _wgrad_cs(a, b3):
    t, k = a.shape
    parts = b3.shape[0]
    per = N_DEV // parts
    n = b3.shape[2] // per
    tt = _row_tile(t)
    nt = t // tt
    return _call(
        functools.partial(_wgrad_rows_body, nt=nt), name="wgrad_cs", out_shape=_sds((N_DEV, k, n), BF16),
        grid=(N_DEV, nt),
        in_specs=[pl.BlockSpec((tt, k), lambda r, s: (s, 0)),
                  pl.BlockSpec((None, tt, n), lambda r, s: (r // per, s, r % per))],
        out_specs=pl.BlockSpec((None, k, n), lambda r, s: (r, 0, 0)),
        scratch_shapes=[pltpu.VMEM((k, n), F32)])(a, b3)


def _dgrad_cs_body(dz_ref, w_ref, o_ref, acc_ref):
    s = pl.program_id(1)

    @pl.when(s == 0)
    def _():
        acc_ref[...] = jnp.zeros_like(acc_ref)

    acc_ref[...] += _dot_nt(dz_ref[...], w_ref[...])

    @pl.when(s == N_DEV - 1)
    def _():
        o_ref[...] = acc_ref[...].astype(o_ref.dtype)


def _dgrad_cs(dz3, wg, after=()):
    parts, t, _ = dz3.shape
    per = N_DEV // parts
    k, n = wg.shape[1], wg.shape[2]
    tm = _row_tile(t)
    return _call(
        functools.partial(_dgrad_cs_body), name="dgrad_cs", out_shape=_sds((t, k), BF16),
        grid=(t // tm, N_DEV),
        in_specs=[pl.BlockSpec((None, tm, n), lambda i, s: (s // per, i, s % per)),
                  pl.BlockSpec((None, k, n), lambda i, s: (s, 0, 0))],
        out_specs=pl.BlockSpec((tm, k), lambda i, s: (i, 0)),
        scratch_shapes=[pltpu.VMEM((tm, k), F32)], after=after)(dz3, wg)


def _dgrad_cs_norm_body(dz_ref, w_ref, x_ref, g_ref, dy_ref, o_ref, dg_ref, acc_ref):
    i = pl.program_id(0)
    s = pl.program_id(1)

    @pl.when(jnp.logical_and(i == 0, s == 0))
    def _():
        dg_ref[...] = jnp.zeros_like(dg_ref)

    @pl.when(s == 0)
    def _():
        acc_ref[...] = jnp.zeros_like(acc_ref)

    acc_ref[...] += _dot_nt(dz_ref[...], w_ref[...])

    @pl.when(s == N_DEV - 1)
    def _():
        dh = acc_ref[...]
        x = x_ref[...]
        rstd = _rstd(x)
        xn = x * rstd
        dg_ref[...] += jnp.sum(dh * xn, axis=0, keepdims=True)
        dxn = dh * g_ref[...]
        o_ref[...] = dy_ref[...] + rstd * (dxn - xn * jnp.mean(dxn * xn, axis=-1, keepdims=True))


def _dgrad_cs_norm(dz3, wg, x, g, dy, after=()):
    parts, t, _ = dz3.shape
    per = N_DEV // parts
    d, n = wg.shape[1], wg.shape[2]
    tm = _row_tile(t)
    blk = pl.BlockSpec((tm, d), lambda i, s: (i, 0))
    vec = pl.BlockSpec((1, d), lambda i, s: (0, 0))
    return _call(
        functools.partial(_dgrad_cs_norm_body), name="dgrad_cs_norm",
        out_shape=(_sds((t, d), F32), _sds((1, d), F32)), grid=(t // tm, N_DEV),
        in_specs=[pl.BlockSpec((None, tm, n), lambda i, s: (s // per, i, s % per)),
                  pl.BlockSpec((None, d, n), lambda i, s: (s, 0, 0)), blk, vec, blk],
        out_specs=(blk, vec),
        scratch_shapes=[pltpu.VMEM((tm, d), F32)], after=after)(dz3, wg, x, g, dy)


def _pool_bwd_body(p_ref, ph_ref, dy_ref, dyh_ref, w_ref, sc_ref, dp_ref, dw_ref, dsc_ref, *, tm, nt):
    g = pl.program_id(0)
    i = pl.program_id(1)

    @pl.when(i == 0)
    def _():
        dw_ref[...] = jnp.zeros_like(dw_ref)
        dsc_ref[...] = jnp.zeros_like(dsc_ref)

    w = w_ref[...]
    sc = sc_ref[...]
    d = _pool_delta(p_ref, ph_ref, g, i, tm).astype(BF16)
    dyps = dy_ref[...].astype(F32)
    dsc_ref[...] += jnp.sum(dyps * _dot(d, w), axis=0, keepdims=True)
    dyp = (dyps * sc).astype(BF16)
    dw_ref[...] += _dot_tn(d, dyp)
    dyp_after = (jnp.where(i == nt - 1, 0.0, dyh_ref[...].astype(F32)) * sc).astype(BF16)
    dd = jnp.concatenate([_dot_nt(dyp, w), _dot_nt(dyp_after, w)], axis=0)
    rows = tm + POOL_HALO
    t1 = i * tm + lax.broadcasted_iota(jnp.int32, (rows, 1), 0) + 1
    e = dd / jnp.minimum(t1, 2 << g).astype(F32)
    r2 = e + pltpu.roll(e, rows - 1, 0)
    r4 = r2 + pltpu.roll(r2, rows - 2, 0)
    r8 = r4 + pltpu.roll(r4, rows - 4, 0)
    r16 = r8 + pltpu.roll(r8, rows - 8, 0)
    r = jnp.where(g == 0, r2, jnp.where(g == 1, r4, jnp.where(g == 2, r8, r16)))
    dp_ref[...] = (r[:tm] - dd[:tm]).astype(dp_ref.dtype)


def _pool_bwd(proj, dyps, wgrp, scale):
    t = proj.shape[0]
    c = wgrp.shape[1]
    tm = _row_tile(t)
    nt = t // tm
    hb = tm // POOL_HALO
    last = t // POOL_HALO - 1
    main, halo = _pool_specs(t, c, tm)
    halo_after = pl.BlockSpec((POOL_HALO, c), lambda g, i: (jnp.minimum((i + 1) * hb, last), g))
    return _call(
        functools.partial(_pool_bwd_body, tm=tm, nt=nt), name="pool_bwd",
        out_shape=(_sds((t, N_POOL_GROUPS * c), BF16), _sds((N_POOL_GROUPS, c, c), F32),
                   _sds((1, N_POOL_GROUPS * c), F32)),
        grid=(N_POOL_GROUPS, nt),
        in_specs=[main, halo, main, halo_after, pl.BlockSpec((None, c, c), lambda g, i: (g, 0, 0)),
                  pl.BlockSpec((1, c), lambda g, i: (0, g))],
        out_specs=(main, pl.BlockSpec((None, c, c), lambda g, i: (g, 0, 0)),
                   pl.BlockSpec((1, c), lambda g, i: (0, g))))(proj, proj, dyps, dyps, wgrp, scale)


def _sgu_bwd_body(u_ref, v_ref, dq_ref, gain_ref, ws_ref, b_ref, duv_ref, dws_ref, db_ref, dgain_ref,
                  dvn_ref, *, tm, heads):
    i = pl.program_id(0)

    @pl.when(i == 0)
    def _():
        dws_ref[...] = jnp.zeros_like(dws_ref)
        db_ref[...] = jnp.zeros_like(db_ref)
        dgain_ref[...] = jnp.zeros_like(dgain_ref)

    sw = heads * SGU_HEAD
    u = u_ref[...].astype(F32)
    v = v_ref[...].astype(F32)
    ug = _gelu(u)
    vg = _gelu(v)
    rstd = _rstd(vg)
    vgn = vg * rstd
    gain = gain_ref[...]
    vn = (vgn * gain).astype(BF16)
    dq = dq_ref[...].astype(F32)
    for h in range(heads):
        wt = _tril_weights(ws_ref, h).astype(BF16)
        cols = slice(h * SGU_HEAD, (h + 1) * SGU_HEAD)
        dws = jnp.zeros((SGU_CHUNK, SGU_CHUNK), F32)
        dbh = jnp.zeros((SGU_CHUNK, 1), F32)
        for c in range(tm // SGU_CHUNK):
            rows = slice(c * SGU_CHUNK, (c + 1) * SGU_CHUNK)
            vn_c = vn[rows, cols]
            sg = _dot(wt, vn_c) + b_ref[h]
            dq_c = dq[rows, cols]
            dsg = dq_c * ug[rows, cols]
            duv_ref[rows, cols] = (dq_c * sg * _gelu_grad(u[rows, cols])).astype(duv_ref.dtype)
            dsg_b = dsg.astype(BF16)
            dws = dws + _dot_nt(dsg_b, vn_c)
            dbh = dbh + jnp.sum(dsg, axis=1, keepdims=True)
            dvn_ref[rows, cols] = _dot_tn(wt, dsg_b)
        dws_ref[h] += jnp.where(_tril_mask(), dws, 0.0)
        db_ref[h] += dbh
    dvn = dvn_ref[...]
    dgain_ref[...] += jnp.sum(dvn * vgn, axis=0, keepdims=True)
    dvgn = dvn * gain
    dvg = rstd * (dvgn - vgn * jnp.mean(dvgn * vgn, axis=-1, keepdims=True))
    duv_ref[:, sw:] = (dvg * _gelu_grad(v)).astype(duv_ref.dtype)


def _sgu_bwd(proj, pw, dq, gain, ws, b3):
    t = proj.shape[0]
    sw = gain.shape[1]
    heads = sw // SGU_HEAD
    tm = _row_tile(t)
    ub = pw // sw
    wblk = pl.BlockSpec((heads, SGU_CHUNK, SGU_CHUNK), lambda i: (0, 0, 0))
    bblk = pl.BlockSpec((heads, SGU_CHUNK, 1), lambda i: (0, 0, 0))
    vec = pl.BlockSpec((1, sw), lambda i: (0, 0))
    return _call(
        functools.partial(_sgu_bwd_body, tm=tm, heads=heads), name="sgu_bwd",
        out_shape=(_sds((t, 2 * sw), BF16), _sds((heads, SGU_CHUNK, SGU_CHUNK), F32),
                   _sds((heads, SGU_CHUNK, 1), F32), _sds((1, sw), F32)),
        grid=(t // tm,),
        in_specs=[pl.BlockSpec((tm, sw), lambda i: (i, ub)),
                  pl.BlockSpec((tm, sw), lambda i: (i, ub + 1)),
                  pl.BlockSpec((tm, sw), lambda i: (i, 0)), vec, wblk, bblk],
        out_specs=(pl.BlockSpec((tm, 2 * sw), lambda i: (i, 0)), wblk, bblk, vec),
        scratch_shapes=[pltpu.VMEM((tm, sw), F32)])(proj, proj, dq, gain, ws, b3)


def _cast_slot_body(me_ref, x_ref, o_ref):
    o_ref[...] = x_ref[...].astype(o_ref.dtype)


def _cast_to_slot(w, layer, me):
    _, r, c = w.shape
    tr = _slab_rows(r)
    spec = pltpu.PrefetchScalarGridSpec(
        num_scalar_prefetch=1, grid=(r // tr,),
        in_specs=[pl.BlockSpec((None, tr, c), lambda i, me_ref: (layer, i, 0))],
        out_specs=pl.BlockSpec((None, tr, c), lambda i, me_ref: (me_ref[0], i, 0)))
    return _call(functools.partial(_cast_slot_body), name="cast_to_slot", out_shape=_sds((N_DEV, r, c), BF16),
                 grid_spec=spec)(me, w)


def _mesh_position():
    x, y, c = (lax.axis_index(a) for a in MESH_AXES)
    return x, y, c


def _peers(x, y, c):
    out = []
    for m in range(1, N_DEV):
        px = 1 - x if m & 4 else x
        py = 1 - y if m & 2 else y
        pc = 1 - c if m & 1 else c
        out.append(((px, py, pc), 4 * px + 2 * py + pc))
    return out


def _remote(src, dst, send, recv, device):
    return pltpu.make_async_remote_copy(src_ref=src, dst_ref=dst, send_sem=send, recv_sem=recv,
                                        device_id=device, device_id_type=pl.DeviceIdType.MESH)


def _all_gather_body(src, out, send, recv, loc):
    x, y, c = _mesh_position()
    me = 4 * x + 2 * y + c
    peers = _peers(x, y, c)
    own = pltpu.make_async_copy(src, out.at[me], loc)
    own.start()
    for j, (peer, _) in enumerate(peers):
        _remote(src, out.at[me], send.at[j], recv.at[j], peer).start()
    for j, (peer, peer_idx) in enumerate(peers):
        cp = _remote(src, out.at[peer_idx], send.at[j], recv.at[j], peer)
        cp.wait_recv()
        cp.wait_send()
    own.wait()


def _all_gather(part, name):
    return _call(
        functools.partial(_all_gather_body), name=name, out_shape=_sds((N_DEV,) + part.shape, part.dtype),
        in_specs=[ANY_SPEC], out_specs=ANY_SPEC,
        scratch_shapes=[pltpu.SemaphoreType.DMA((N_PEERS,)), pltpu.SemaphoreType.DMA((N_PEERS,)),
                        pltpu.SemaphoreType.DMA(())])(part)


def _other_chips(x, y):
    return [(1 - x, y), (x, 1 - y), (1 - x, 1 - y)]


def _gather_send_own(bufs, send, recv):
    x, y, c = _mesh_position()
    me = 4 * x + 2 * y + c
    targets = [(x, y, 1 - c)] + [(px, py, c) for px, py in _other_chips(x, y)]
    for k, buf in enumerate(bufs):
        for j, dev in enumerate(targets):
            _remote(buf.at[me], buf.at[me], send.at[4 * k + j], recv.at[4 * k + j], dev).start()


def _gather_await_own(bufs, send, recv):
    x, y, c = _mesh_position()
    me = 4 * x + 2 * y + c
    targets = [(x, y, 1 - c)] + [(px, py, c) for px, py in _other_chips(x, y)]
    for k, buf in enumerate(bufs):
        for j, dev in enumerate(targets):
            cp = _remote(buf.at[me], buf.at[4 * dev[0] + 2 * dev[1] + dev[2]], send.at[4 * k + j],
                         recv.at[4 * k + j], dev)
            cp.wait_recv()
            cp.wait_send()


def _gather_relay_send(bufs, send, recv):
    x, y, c = _mesh_position()
    for k, buf in enumerate(bufs):
        for j, (px, py) in enumerate(_other_chips(x, y)):
            blk = buf.at[4 * px + 2 * py + c]
            _remote(blk, blk, send.at[3 * k + j], recv.at[3 * k + j], (x, y, 1 - c)).start()


def _gather_relay_await(bufs, send, recv):
    x, y, c = _mesh_position()
    for k, buf in enumerate(bufs):
        for j, (px, py) in enumerate(_other_chips(x, y)):
            cp = _remote(buf.at[4 * px + 2 * py + c], buf.at[4 * px + 2 * py + 1 - c], send.at[3 * k + j],
                         recv.at[3 * k + j], (x, y, 1 - c))
            cp.wait_recv()
            cp.wait_send()


def _reduce_pair_send(grads, stage, send, recv):
    x, y, c = _mesh_position()
    for k, (g, s) in enumerate(zip(grads, stage)):
        for q in range(4):
            _remote(g.at[2 * q + 1 - c], s.at[q], send.at[4 * k + q], recv.at[4 * k + q], (x, y, 1 - c)).start()


def _reduce_pair_await(grads, stage, send, recv):
    x, y, c = _mesh_position()
    for k, (g, s) in enumerate(zip(grads, stage)):
        for q in range(4):
            cp = _remote(g.at[2 * q + 1 - c], s.at[q], send.at[4 * k + q], recv.at[4 * k + q], (x, y, 1 - c))
            cp.wait_recv()
            cp.wait_send()


def _reduce_chip_send(sums, stage, send, recv):
    x, y, c = _mesh_position()
    for k, (p, s) in enumerate(zip(sums, stage)):
        for j, (px, py) in enumerate(_other_chips(x, y)):
            _remote(p.at[2 * px + py], s.at[2 * x + y], send.at[3 * k + j], recv.at[3 * k + j],
                    (px, py, c)).start()


def _reduce_chip_await(sums, stage, send, recv):
    x, y, c = _mesh_position()
    for k, (p, s) in enumerate(zip(sums, stage)):
        for j, (px, py) in enumerate(_other_chips(x, y)):
            cp = _remote(p.at[2 * px + py], s.at[2 * px + py], send.at[3 * k + j], recv.at[3 * k + j],
                         (px, py, c))
            cp.wait_recv()
            cp.wait_send()


def _split_body(*refs, n, n_sem_in, n_after, n_sem_out, steps, token):
    bufs = refs[:n]
    sems = refs[n:n + n_sem_in] + refs[n + n_sem_in + n_after:n + n_sem_in + n_after + n_sem_out]
    for step, lo in steps:
        step(bufs, sems[lo], sems[lo + 1])
    if token:
        refs[-1][...] = jnp.zeros_like(refs[-1])


def _split_call(steps, *, name, bufs, sems_in=(), sems_out=(), after=(), token=True):
    n = len(bufs)
    operands = [pltpu.with_memory_space_constraint(b, pltpu.HBM) for b in bufs] + list(sems_in) + list(after)
    in_specs = [HBM_SPEC] * n + [SEM_SPEC] * len(sems_in) + [ANY_SPEC] * len(after)
    out_shape = list(sems_out) + [pltpu.HBM(b.shape, b.dtype) for b in bufs]
    out_specs = [SEM_SPEC] * len(sems_out) + [HBM_SPEC] * n
    if token:
        out_shape.append(_sds((8, 128), F32))
        out_specs.append(pl.BlockSpec(memory_space=pltpu.VMEM))
    body = functools.partial(_split_body, n=n, n_sem_in=len(sems_in), n_after=len(after),
                             n_sem_out=len(sems_out), steps=tuple(steps), token=token)
    return pl.pallas_call(
        body, name=name, in_specs=in_specs, out_specs=tuple(out_specs), out_shape=tuple(out_shape),
        input_output_aliases={i: len(sems_out) + i for i in range(n)},
        compiler_params=pltpu.CompilerParams(
            has_side_effects=pltpu.SideEffectType.DATAFLOW_SIDE_EFFECTING))(*operands)


def _sem_pair(n, m):
    return [pltpu.SemaphoreType.DMA((n * m,)), pltpu.SemaphoreType.DMA((n * m,))]


def _gather_start(bufs, after, tag):
    n = len(bufs)
    out = _split_call([(_gather_send_own, 0)], name="gather_start_" + tag, bufs=bufs,
                      sems_out=_sem_pair(n, 4), after=after)
    return out[0], out[1], list(out[2:2 + n]), out[-1]


def _gather_relay(state, after, tag):
    send, recv, bufs, _ = state
    n = len(bufs)
    out = _split_call([(_gather_await_own, 0), (_gather_relay_send, 2)], name="gather_relay_" + tag, bufs=bufs,
                      sems_in=[send, recv], sems_out=_sem_pair(n, 3), after=after)
    return out[0], out[1], list(out[2:2 + n]), out[-1]


def _gather_finish(state, after, tag):
    send, recv, bufs, _ = state
    out = _split_call([(_gather_relay_await, 0)], name="gather_finish_" + tag, bufs=bufs,
                      sems_in=[send, recv], after=after, token=False)
    return list(out)


def _pair_arrays(n):
    def split(step):
        return lambda bufs, send, recv: step(bufs[:n], bufs[n:], send, recv)
    return split


def _reduce_pairs_start(grads, after, tag):
    n = len(grads)
    stage = [lax.empty((4,) + g.shape[1:], g.dtype) for g in grads]
    out = _split_call([(_pair_arrays(n)(_reduce_pair_send), 0)], name="reduce_pairs_start_" + tag,
                      bufs=list(grads) + stage, sems_out=_sem_pair(n, 4), after=after)
    return out[0], out[1], list(out[2:2 + n]), list(out[2 + n:2 + 2 * n]), out[-1]


def _reduce_pairs_finish(state, after, tag):
    send, recv, grads, stage, _ = state
    n = len(grads)
    out = _split_call([(_pair_arrays(n)(_reduce_pair_await), 0)], name="reduce_pairs_finish_" + tag,
                      bufs=list(grads) + list(stage), sems_in=[send, recv], after=after, token=False)
    return list(out[:n]), list(out[n:])


def _reduce_chips_start(sums, after, tag):
    n = len(sums)
    stage = [lax.empty(p.shape, p.dtype) for p in sums]
    out = _split_call([(_pair_arrays(n)(_reduce_chip_send), 0)], name="reduce_chips_start_" + tag,
                      bufs=list(sums) + stage, sems_out=_sem_pair(n, 3), after=after)
    return out[0], out[1], list(out[2:2 + n]), list(out[2 + n:2 + 2 * n]), out[-1]


def _reduce_chips_finish(state, after, tag):
    send, recv, sums, stage, _ = state
    n = len(sums)
    out = _split_call([(_pair_arrays(n)(_reduce_chip_await), 0)], name="reduce_chips_finish_" + tag,
                      bufs=list(sums) + list(stage), sems_in=[send, recv], after=after, token=False)
    return list(out[:n]), list(out[n:])


def _pair_sum_body(c_ref, a_ref, b_ref, o_ref):
    o_ref[...] = (a_ref[...].astype(F32) + b_ref[...].astype(F32)).astype(o_ref.dtype)


def _pair_sum(grad, stage, core):
    _, r, c = grad.shape
    tr = _slab_rows(r)
    blk = pl.BlockSpec((None, tr, c), lambda q, i, c_ref: (q, i, 0))
    spec = pltpu.PrefetchScalarGridSpec(
        num_scalar_prefetch=1, grid=(4, r // tr),
        in_specs=[pl.BlockSpec((None, tr, c), lambda q, i, c_ref: (2 * q + c_ref[0], i, 0)), blk],
        out_specs=blk)
    return _call(functools.partial(_pair_sum_body), name="pair_sum", out_shape=_sds((4, r, c), grad.dtype),
                 grid_spec=spec)(core, grad, stage)


def _adamw_math(w, g, m, v):
    m = ADAM_B1 * m + (1.0 - ADAM_B1) * g
    v = ADAM_B2 * v + (1.0 - ADAM_B2) * (g * g)
    m_hat = m / (1.0 - ADAM_B1 ** ADAM_STEP)
    v_hat = v / (1.0 - ADAM_B2 ** ADAM_STEP)
    delta = -ADAM_LR * (m_hat / (jnp.sqrt(v_hat) + ADAM_EPS) + ADAM_WD * w)
    return delta, m, v


def _adamw_body(q_ref, own_ref, r_ref, w_ref, m_ref, v_ref, *rest):
    g_out, d_out, m_out, v_out = rest[-4:]
    q = q_ref[0]
    g = None
    for j in range(4):
        term = jnp.where(q == j, own_ref[...], r_ref[j]).astype(F32)
        g = term if g is None else g + term
    delta, m, v = _adamw_math(w_ref[...], g, m_ref[...], v_ref[...])
    g_out[...] = g
    d_out[...] = delta
    m_out[...] = m
    v_out[...] = v


def _adamw_layer(sums, recv, w, m, v, prev, layer, chip):
    l, r, c = w.shape
    tr = _slab_rows(r)
    lay = pl.BlockSpec((None, tr, c), lambda i, q_ref: (layer, i, 0))
    spec = pltpu.PrefetchScalarGridSpec(
        num_scalar_prefetch=1, grid=(r // tr,),
        in_specs=[pl.BlockSpec((None, tr, c), lambda i, q_ref: (q_ref[0], i, 0)),
                  pl.BlockSpec((4, tr, c), lambda i, q_ref: (0, i, 0)), lay, lay, lay] + [ANY_SPEC] * 4,
        out_specs=(lay, lay, lay, lay))
    out = _sds((l, r, c), F32)
    return _call(
        functools.partial(_adamw_body), name="adamw", out_shape=(out, out, out, out), grid_spec=spec,
        aliases={6: 0, 7: 1, 8: 2, 9: 3})(chip, sums, recv, w, m, v, *prev)


def _adamw_small_body(r_ref, w_ref, m_ref, v_ref, g_out, d_out, m_out, v_out):
    g = r_ref[0]
    for j in range(1, N_DEV):
        g = g + r_ref[j]
    delta, m, v = _adamw_math(w_ref[...], g, m_ref[...], v_ref[...])
    g_out[...] = g
    d_out[...] = delta
    m_out[...] = m
    v_out[...] = v


def _adamw_small(recv, w, m, v):
    r, c = w.shape
    tr = 8
    for cand in (1024, 512, 256, 128, 64, 32, 16, 8):
        if r % cand == 0:
            tr = cand
            break
    blk = pl.BlockSpec((tr, c), lambda i: (i, 0))
    out = _sds((r, c), F32)
    return _call(
        functools.partial(_adamw_small_body), name="adamw_small", out_shape=(out, out, out, out),
        grid=(r // tr,),
        in_specs=[pl.BlockSpec((N_DEV, tr, c), lambda i: (0, i, 0)), blk, blk, blk],
        out_specs=(blk, blk, blk, blk))(recv, w, m, v)


def _as3d(a):
    return a.reshape(a.shape[0], -1, a.shape[-1])


GROUPS = (("w_ffn1_up", "w_ffn1_down"),
          ("w_in", "pool_group_w", "w_pool_out", "w_sgu_out", "w_out"),
          ("w_ffn2_up", "w_ffn2_down"))


def _ffn_forward(x, g_pre, wu, wd, g_post, after=()):
    h = _rms_fwd(x, g_pre, after=after)
    gu, a = _ffn_up(h, wu)
    x_out, f = _down_norm_res(a, wd, x, g_post, MACARON_WEIGHT)
    return x_out, (x, h, gu, a, f)


def _ffn_backward(dy, saved, g_pre, wu, wd, g_post, after=(), mid=None):
    x, h, gu, a, f = saved
    df, dg_post = _norm_res_bwd(dy, f, g_post, MACARON_WEIGHT, after=after)
    dz = _dgrad_ffn(df, wd, gu)
    tokens = tuple(mid(dz)) if mid else ()
    dwd = _wgrad_rows(a, df)
    dwu = _wgrad_cs(h, dz)
    dy, dg_pre = _dgrad_cs_norm(dz, wu, x, g_pre, dy, after=tokens + (dwd, dwu))
    return dy, dg_pre, dwu, dwd, dg_post


def kernel(x, g_ffn1_pre, w_ffn1_up, w_ffn1_down, g_ffn1_post, g_mix_pre, w_in, pool_group_w, pool_scale, w_pool_out, sgu_v_gain, sgu_w_s, sgu_b_s, w_sgu_out, w_out, g_mix_post, g_ffn2_pre, w_ffn2_up, w_ffn2_down, g_ffn2_post, loss_target, m_g_ffn1_pre, m_w_ffn1_up, m_w_ffn1_down, m_g_ffn1_post, m_g_mix_pre, m_w_in, m_pool_group_w, m_pool_scale, m_w_pool_out, m_sgu_v_gain, m_sgu_w_s, m_sgu_b_s, m_w_sgu_out, m_w_out, m_g_mix_post, m_g_ffn2_pre, m_w_ffn2_up, m_w_ffn2_down, m_g_ffn2_post, v_g_ffn1_pre, v_w_ffn1_up, v_w_ffn1_down, v_g_ffn1_post, v_g_mix_pre, v_w_in, v_pool_group_w, v_pool_scale, v_w_pool_out, v_sgu_v_gain, v_sgu_w_s, v_sgu_b_s, v_w_sgu_out, v_w_out, v_g_mix_post, v_g_ffn2_pre, v_w_ffn2_up, v_w_ffn2_down, v_g_ffn2_post):
    given = dict(locals())
    weights = {n: given[n] for n in ORDER}
    mom_m = {n: given["m_" + n] for n in ORDER}
    mom_v = {n: given["v_" + n] for n in ORDER}
    depth = g_ffn1_pre.shape[0]
    d_model = x.shape[-1]
    pool_w = pool_scale.shape[1]
    sgu_w = sgu_v_gain.shape[1]
    groups, grp_rows, grp_c = pool_group_w.shape[1:]
    gate_off = pool_w + 2 * sgu_w

    xs = x.reshape(x.shape[-2], d_model)
    target = loss_target.reshape(xs.shape)

    px, py, pc = _mesh_position()
    me = (4 * px + 2 * py + pc).astype(jnp.int32).reshape(1)
    core = pc.astype(jnp.int32).reshape(1)
    chip = (2 * px + py).astype(jnp.int32).reshape(1)

    n_groups = len(GROUPS) * depth
    ffn_gains = {0: ("g_ffn1_pre", "g_ffn1_post"), 2: ("g_ffn2_pre", "g_ffn2_post")}

    def start_gather(t, after):
        layer, gi = divmod(t, len(GROUPS))
        bufs = [_cast_to_slot(_as3d(weights[n]), layer, me) for n in GROUPS[gi]]
        return _gather_start(bufs, after, "g%d" % t)

    def vec(name, i):
        return weights[name][i].reshape(1, -1)

    def forward_group(t, cur, bufs, after):
        layer, gi = divmod(t, len(GROUPS))
        gw = dict(zip(GROUPS[gi], bufs))
        if gi != 1:
            pre, post = ffn_gains[gi]
            wts = tuple(gw[n] for n in GROUPS[gi])
            wts = (wts[0], wts[1].reshape(-1, d_model))
            cur, sv = _ffn_forward(cur, vec(pre, layer), wts[0], wts[1], vec(post, layer), after=after)
            return cur, (wts, sv)
        wgrp = gw["pool_group_w"].reshape(N_DEV, groups, grp_rows, grp_c).transpose(1, 0, 2, 3)
        wts = dict(win=gw["w_in"], wgrp=wgrp.reshape(groups, grp_c, grp_c), wpo=gw["w_pool_out"],
                   wso=gw["w_sgu_out"], wout=gw["w_out"].reshape(-1, d_model),
                   ws=weights["sgu_w_s"][layer], b3=weights["sgu_b_s"][layer][:, :, None])
        x1 = cur
        h2 = _rms_fwd(x1, vec("g_mix_pre", layer), after=after)
        proj = _mm_cs(h2, wts["win"])
        yps = _pool_fwd(proj, wts["wgrp"], vec("pool_scale", layer))
        q = _sgu_fwd(proj, pool_w, vec("sgu_v_gain", layer), wts["ws"], wts["b3"])
        m, ya, yb = _mix_gate(yps, q, proj, gate_off, wts["wpo"], wts["wso"])
        cur, o = _down_norm_res(m, wts["wout"], x1, vec("g_mix_post", layer), 1.0)
        return cur, (wts, (x1, h2, proj, yps, q, ya, yb, m, o))

    states = {0: start_gather(0, ())}
    states[1] = start_gather(1, (states[0][-1],))
    relayed = _gather_relay(states[0], (states[1][-1],), "g0")
    bufs = _gather_finish(relayed, (), "g0")
    saved = []
    cur = xs
    for t in range(n_groups):
        after = []
        if t + 1 < n_groups:
            relayed = _gather_relay(states[t + 1], (bufs[0],), "g%d" % (t + 1))
            after.append(relayed[-1])
        if t + 2 < n_groups:
            states[t + 2] = start_gather(t + 2, (bufs[0], relayed[-1]))
            after.append(states[t + 2][-1])
        cur, sv = forward_group(t, cur, bufs, tuple(after))
        saved.append(sv)
        if t + 1 < n_groups:
            bufs = _gather_finish(relayed, (cur,), "g%d" % (t + 1))

    dy, sq = _loss_grad(cur, target)
    loss = lax.psum(0.5 * sq[0, 0] / d_model, MESH_AXES)

    small_grads = {n: [None] * depth for n in SMALL}
    outs = {n: None for n in BIG}

    def chip_sums(pairs, after, tag):
        grads, stage = _reduce_pairs_finish(pairs, after, tag)
        sums = [_pair_sum(g, s, core) for g, s in zip(grads, stage)]
        return _reduce_chips_start(sums, (), tag)

    def update(chips, after, t):
        layer, gi = divmod(t, len(GROUPS))
        sums, recv = _reduce_chips_finish(chips, after, "g%d" % t)
        for n, p, r in zip(GROUPS[gi], sums, recv):
            w3, m3, v3 = _as3d(weights[n]), _as3d(mom_m[n]), _as3d(mom_v[n])
            prev = outs[n] if outs[n] is not None else [lax.empty(w3.shape, F32) for _ in range(4)]
            outs[n] = _adamw_layer(p, r, w3, m3, v3, prev, layer, chip)

    def backward_group(t, dy, sv, after, mid):
        layer, gi = divmod(t, len(GROUPS))
        wts, sv = sv
        if gi != 1:
            pre, post = ffn_gains[gi]
            dy, dg, dwu, dwd, dgp = _ffn_backward(dy, sv, vec(pre, layer), wts[0], wts[1], vec(post, layer),
                                                  after=after, mid=mid)
            small_grads[pre][layer], small_grads[post][layer] = dg, dgp
            return dy, [dwu, dwd.reshape(N_DEV, -1, d_model)]
        x1, h2, proj, yps, q, ya, yb, m, o = sv
        do, small_grads["g_mix_post"][layer] = _norm_res_bwd(dy, o, vec("g_mix_post", layer), 1.0, after=after)
        dya, dyb, dga, dgb = _dgrad_gate(do, wts["wout"], ya, yb, proj, gate_off)
        tokens = tuple(mid(dya)) if mid else ()
        dwout = _wgrad_rows(m, do).reshape(N_DEV, -1, d_model)
        dwpo = _wgrad_cs(yps, dya[None])
        dwso = _wgrad_cs(q, dyb[None])
        dyps = _dgrad_cs(dya[None], wts["wpo"], after=tokens)
        dq = _dgrad_cs(dyb[None], wts["wso"])
        dp, dwgrp, small_grads["pool_scale"][layer] = _pool_bwd(proj, dyps, wts["wgrp"], vec("pool_scale", layer))
        duv, dws, db3, small_grads["sgu_v_gain"][layer] = _sgu_bwd(
            proj, pool_w, dq, vec("sgu_v_gain", layer), wts["ws"], wts["b3"])
        small_grads["sgu_w_s"][layer] = dws
        small_grads["sgu_b_s"][layer] = db3
        dproj = jnp.concatenate([dp, duv, dga, dgb], axis=1)[None]
        dwin = _wgrad_cs(h2, dproj)
        dwgrp8 = dwgrp.reshape(groups, N_DEV, grp_rows, grp_c).transpose(1, 0, 2, 3)
        dwgrp8 = dwgrp8.reshape(N_DEV, groups * grp_rows, grp_c).astype(BF16)
        grads = [dwin, dwgrp8, dwpo, dwso, dwout]
        dy, small_grads["g_mix_pre"][layer] = _dgrad_cs_norm(dproj, wts["win"], x1, vec("g_mix_pre", layer), dy,
                                                             after=tuple(grads))
        return dy, grads

    pairs = None
    for t in reversed(range(n_groups)):
        started = []

        def mid(first, t=t, started=started):
            started.append(chip_sums(pairs, (first,), "g%d" % (t + 1)))
            return (started[0][-1],)

        dy, grads = backward_group(t, dy, saved[t], (pairs[-1],) if pairs else (), mid if pairs else None)
        if started:
            update(started[0], (dy,), t + 1)
        pairs = _reduce_pairs_start(grads, (), "g%d" % t)
    last = chip_sums(pairs, (), "g0")

    def pack(parts):
        return jnp.concatenate([p.reshape(depth, -1) for p in parts], axis=1).reshape(-1, 128)

    part = pack([jnp.stack(small_grads[n]) for n in SMALL])
    all_parts = _all_gather(part, "gather_small_grads")
    small_out = _adamw_small(all_parts, pack([weights[n] for n in SMALL]), pack([mom_m[n] for n in SMALL]),
                             pack([mom_v[n] for n in SMALL]))
    done = [outs[n][0] for n in BIG if outs[n] is not None]
    update(last, (small_out[0], *done), 0)

    results = {}
    sizes = [weights[n][0].size for n in SMALL]
    for kind, packed in zip(("grad", "delta", "new_m", "new_v"), small_out):
        flat = packed.reshape(depth, -1)
        off = 0
        for n, sz in zip(SMALL, sizes):
            results[(kind, n)] = flat[:, off:off + sz].reshape(weights[n].shape)
            off += sz
    for n in BIG:
        for kind, arr in zip(("grad", "delta", "new_m", "new_v"), outs[n]):
            results[(kind, n)] = arr.reshape(weights[n].shape)

    grad_x = dy.reshape(x.shape)
    return (loss, grad_x, *[results[("grad", n)] for n in ORDER], *[results[("delta", n)] for n in ORDER],
            *[results[("new_m", n)] for n in ORDER], *[results[("new_v", n)] for n in ORDER])
```

```python
import functools

import jax
import jax.numpy as jnp
from jax import lax
from jax.experimental import pallas as pl
from jax.experimental.pallas import tpu as pltpu

F32 = jnp.float32
BF16 = jnp.bfloat16

N_DEV = 8
MESH_AXES = ("x", "y", "c")
N_PEERS = N_DEV - 1

EPS = 1e-6
MACARON_WEIGHT = 0.5
N_POOL_GROUPS = 4
POOL_HALO = 16
SGU_HEAD = 128
SGU_CHUNK = 128
GELU_C = 0.7978845608028654
GELU_A = 0.044715

ADAM_LR = 0.001
ADAM_B1 = 0.9
ADAM_B2 = 0.999
ADAM_EPS = 1e-08
ADAM_WD = 0.01
ADAM_STEP = 10

ROW_TILE = 512
VMEM_LIMIT_BYTES = 56 * 1024 * 1024

BIG = ("w_ffn1_up", "w_ffn1_down", "w_in", "pool_group_w", "w_pool_out", "w_sgu_out", "w_out",
       "w_ffn2_up", "w_ffn2_down")
SMALL = ("g_ffn1_pre", "g_ffn1_post", "g_mix_pre", "pool_scale", "sgu_v_gain", "sgu_w_s", "sgu_b_s",
         "g_mix_post", "g_ffn2_pre", "g_ffn2_post")
ORDER = ("g_ffn1_pre", "w_ffn1_up", "w_ffn1_down", "g_ffn1_post", "g_mix_pre", "w_in", "pool_group_w",
         "pool_scale", "w_pool_out", "sgu_v_gain", "sgu_w_s", "sgu_b_s", "w_sgu_out", "w_out",
         "g_mix_post", "g_ffn2_pre", "w_ffn2_up", "w_ffn2_down", "g_ffn2_post")


ANY_SPEC = pl.BlockSpec(memory_space=pl.ANY)
HBM_SPEC = pl.BlockSpec(memory_space=pltpu.HBM)
SEM_SPEC = pl.BlockSpec(memory_space=pltpu.SEMAPHORE)


def _skip_refs(body, n_in, n_skip, *refs):
    return body(*refs[:n_in], *refs[n_in + n_skip:])


def _call(body, *, name, out_shape, grid=(), in_specs=None, out_specs=None, scratch_shapes=(),
          aliases=None, after=(), grid_spec=None):
    extra = {}
    if grid_spec is not None:
        extra["grid_spec"] = grid_spec
    else:
        extra.update(in_specs=list(in_specs) + [ANY_SPEC] * len(after), out_specs=out_specs,
                     scratch_shapes=list(scratch_shapes))
        if grid:
            extra["grid"] = grid
        if after:
            body = functools.partial(_skip_refs, body, len(in_specs), len(after))
    if aliases:
        extra["input_output_aliases"] = aliases
    call = pl.pallas_call(
        body, name=name, out_shape=out_shape,
        compiler_params=pltpu.CompilerParams(vmem_limit_bytes=VMEM_LIMIT_BYTES), **extra)
    if after:
        return lambda *args: call(*args, *after)
    return call


def _sds(shape, dtype):
    return jax.ShapeDtypeStruct(tuple(shape), dtype)


def _dot(a, b):
    return jnp.dot(a, b, preferred_element_type=F32)


def _dot_nt(a, b):
    return lax.dot_general(a, b, (((1,), (1,)), ((), ())), preferred_element_type=F32)


def _dot_tn(a, b):
    return lax.dot_general(a, b, (((0,), (0,)), ((), ())), preferred_element_type=F32)


def _sigmoid(x):
    return 1.0 / (1.0 + jnp.exp(-x))


def _gelu(x):
    return 0.5 * x * (1.0 + jnp.tanh(GELU_C * (x + GELU_A * x * x * x)))


def _gelu_grad(x):
    t = jnp.tanh(GELU_C * (x + GELU_A * x * x * x))
    return 0.5 * (1.0 + t) + 0.5 * x * (1.0 - t * t) * (GELU_C * (1.0 + 3.0 * GELU_A * x * x))


def _rstd(x):
    return lax.rsqrt(jnp.mean(x * x, axis=-1, keepdims=True) + EPS)


def _row_tile(t):
    return min(t, ROW_TILE)


def _big_row_tile(t):
    return 2 * ROW_TILE if t % (2 * ROW_TILE) == 0 else _row_tile(t)


def _slab_rows(r, most=256):
    for cand in range(min(most, r) // 16 * 16, 15, -16):
        if r % cand == 0:
            return cand
    return r


def _rms_fwd_body(x_ref, g_ref, h_ref):
    x = x_ref[...]
    h_ref[...] = ((x * _rstd(x)) * g_ref[...]).astype(h_ref.dtype)


def _rms_fwd(x, g, after=()):
    t, d = x.shape
    tm = _row_tile(t)
    return _call(
        functools.partial(_rms_fwd_body), name="rms_fwd", out_shape=_sds((t, d), BF16),
        grid=(t // tm,),
        in_specs=[pl.BlockSpec((tm, d), lambda i: (i, 0)), pl.BlockSpec((1, d), lambda i: (0, 0))],
        out_specs=pl.BlockSpec((tm, d), lambda i: (i, 0)), after=after)(x, g)


def _mm_cs_body(h_ref, w_ref, o_ref):
    o_ref[...] = _dot(h_ref[...], w_ref[...]).astype(o_ref.dtype)


def _mm_cs(h, wg):
    t, k = h.shape
    n = wg.shape[2]
    tm = _row_tile(t)
    return _call(
        functools.partial(_mm_cs_body), name="mm_cs", out_shape=_sds((t, N_DEV * n), BF16),
        grid=(N_DEV, t // tm),
        in_specs=[pl.BlockSpec((tm, k), lambda j, i: (i, 0)),
                  pl.BlockSpec((None, k, n), lambda j, i: (j, 0, 0))],
        out_specs=pl.BlockSpec((tm, n), lambda j, i: (i, j)))(h, wg)


def _ffn_up_body(h_ref, wg_ref, wu_ref, gu_ref, a_ref):
    h = h_ref[...]
    g = _dot(h, wg_ref[...])
    u = _dot(h, wu_ref[...])
    gu_ref[0] = g.astype(gu_ref.dtype)
    gu_ref[1] = u.astype(gu_ref.dtype)
    a_ref[...] = (g * _sigmoid(g) * u).astype(a_ref.dtype)


def _ffn_up(h, wg):
    t, k = h.shape
    n = wg.shape[2]
    half = N_DEV // 2
    f = half * n
    tm = _row_tile(t)
    return _call(
        functools.partial(_ffn_up_body), name="ffn_up",
        out_shape=(_sds((2, t, f), BF16), _sds((t, f), BF16)),
        grid=(half, t // tm),
        in_specs=[pl.BlockSpec((tm, k), lambda j, i: (i, 0)),
                  pl.BlockSpec((None, k, n), lambda j, i: (j, 0, 0)),
                  pl.BlockSpec((None, k, n), lambda j, i: (j + N_DEV // 2, 0, 0))],
        out_specs=(pl.BlockSpec((2, tm, n), lambda j, i: (0, i, j)),
                   pl.BlockSpec((tm, n), lambda j, i: (i, j))))(h, wg, wg)


EPILOGUE_ROWS = 256


def _for_row_chunks(rows, step):
    chunk = min(rows, EPILOGUE_ROWS)

    def body(r, carry):
        step(pl.ds(pl.multiple_of(r * chunk, chunk), chunk))
        return carry

    lax.fori_loop(0, rows // chunk, body, 0)


def _down_body(a_ref, w_ref, x_ref, g_ref, xo_ref, f_ref, *, scale, nk, tm):
    k = pl.program_id(1)

    @pl.when(k == 0)
    def _():
        xo_ref[...] = jnp.zeros_like(xo_ref)

    xo_ref[...] += _dot(a_ref[...], w_ref[...])

    @pl.when(k == nk - 1)
    def _():
        def finish(rows):
            f = xo_ref[rows, :]
            f_ref[rows, :] = f.astype(f_ref.dtype)
            xo_ref[rows, :] = x_ref[rows, :] + scale * ((f * _rstd(f)) * g_ref[...])

        _for_row_chunks(tm, finish)


def _down_norm_res(a, w, x, g, scale, after=()):
    t, kk = a.shape
    d = w.shape[1]
    tm = _big_row_tile(t)
    tk = 512 if kk % 512 == 0 else 256
    nk = kk // tk
    once = pl.Buffered(1)
    return _call(
        functools.partial(_down_body, scale=scale, nk=nk, tm=tm), name="down_norm_res",
        out_shape=(_sds((t, d), F32), _sds((t, d), BF16)),
        grid=(t // tm, nk),
        in_specs=[pl.BlockSpec((tm, tk), lambda i, k: (i, k)),
                  pl.BlockSpec((tk, d), lambda i, k: (k, 0)),
                  pl.BlockSpec((tm, d), lambda i, k: (i, 0), pipeline_mode=once),
                  pl.BlockSpec((1, d), lambda i, k: (0, 0))],
        out_specs=(pl.BlockSpec((tm, d), lambda i, k: (i, 0), pipeline_mode=once),
                   pl.BlockSpec((tm, d), lambda i, k: (i, 0), pipeline_mode=once)),
        after=after)(a, w, x, g)


def _pool_inv_count(g, i, tm):
    t1 = i * tm + lax.broadcasted_iota(jnp.int32, (tm, 1), 0) + 1
    return 1.0 / jnp.minimum(t1, 2 << g).astype(F32)


def _pool_windows(e, g):
    s2 = e + pltpu.roll(e, 1, 0)
    s4 = s2 + pltpu.roll(s2, 2, 0)
    s8 = s4 + pltpu.roll(s4, 4, 0)
    s16 = s8 + pltpu.roll(s8, 8, 0)
    return jnp.where(g == 0, s2, jnp.where(g == 1, s4, jnp.where(g == 2, s8, s16)))


def _pool_delta(p_ref, ph_ref, g, i, tm):
    pm = p_ref[...].astype(F32)
    ph = jnp.where(i == 0, 0.0, ph_ref[...].astype(F32))
    e = jnp.concatenate([ph, pm], axis=0)
    s = _pool_windows(e, g)[POOL_HALO:]
    return s * _pool_inv_count(g, i, tm) - pm


def _pool_fwd_body(p_ref, ph_ref, w_ref, sc_ref, o_ref, *, tm):
    g = pl.program_id(0)
    i = pl.program_id(1)
    d = _pool_delta(p_ref, ph_ref, g, i, tm)
    o_ref[...] = (_dot(d.astype(BF16), w_ref[...]) * sc_ref[...]).astype(o_ref.dtype)


def _pool_specs(t, c, tm):
    hb = tm // POOL_HALO
    main = pl.BlockSpec((tm, c), lambda g, i: (i, g))
    halo_before = pl.BlockSpec((POOL_HALO, c), lambda g, i: (jnp.maximum(i * hb - 1, 0), g))
    return main, halo_before


def _pool_fwd(proj, wgrp, scale):
    t = proj.shape[0]
    c = wgrp.shape[1]
    tm = _row_tile(t)
    main, halo = _pool_specs(t, c, tm)
    return _call(
        functools.partial(_pool_fwd_body, tm=tm), name="pool_fwd",
        out_shape=_sds((t, N_POOL_GROUPS * c), BF16), grid=(N_POOL_GROUPS, t // tm),
        in_specs=[main, halo, pl.BlockSpec((None, c, c), lambda g, i: (g, 0, 0)),
                  pl.BlockSpec((1, c), lambda g, i: (0, g))],
        out_specs=pl.BlockSpec((tm, c), lambda g, i: (i, g)))(proj, proj, wgrp, scale)


def _tril_mask():
    r = lax.broadcasted_iota(jnp.int32, (SGU_CHUNK, SGU_CHUNK), 0)
    c = lax.broadcasted_iota(jnp.int32, (SGU_CHUNK, SGU_CHUNK), 1)
    return r >= c


def _tril_weights(ws_ref, h):
    return jnp.where(_tril_mask(), ws_ref[h], 0.0)


def _sgu_fwd_body(u_ref, v_ref, gain_ref, ws_ref, b_ref, q_ref, *, tm, heads):
    ug = _gelu(u_ref[...].astype(F32))
    vg = _gelu(v_ref[...].astype(F32))
    vn = ((vg * _rstd(vg)) * gain_ref[...]).astype(BF16)
    for h in range(heads):
        wt = _tril_weights(ws_ref, h).astype(BF16)
        cols = slice(h * SGU_HEAD, (h + 1) * SGU_HEAD)
        for c in range(tm // SGU_CHUNK):
            rows = slice(c * SGU_CHUNK, (c + 1) * SGU_CHUNK)
            sg = _dot(wt, vn[rows, cols]) + b_ref[h]
            q_ref[rows, cols] = (ug[rows, cols] * sg).astype(q_ref.dtype)


def _sgu_fwd(proj, pw, gain, ws, b3):
    t = proj.shape[0]
    sw = gain.shape[1]
    heads = sw // SGU_HEAD
    tm = _row_tile(t)
    ub = pw // sw
    return _call(
        functools.partial(_sgu_fwd_body, tm=tm, heads=heads), name="sgu_fwd",
        out_shape=_sds((t, sw), BF16), grid=(t // tm,),
        in_specs=[pl.BlockSpec((tm, sw), lambda i: (i, ub)),
                  pl.BlockSpec((tm, sw), lambda i: (i, ub + 1)),
                  pl.BlockSpec((1, sw), lambda i: (0, 0)),
                  pl.BlockSpec((heads, SGU_CHUNK, SGU_CHUNK), lambda i: (0, 0, 0)),
                  pl.BlockSpec((heads, SGU_CHUNK, 1), lambda i: (0, 0, 0))],
        out_specs=pl.BlockSpec((tm, sw), lambda i: (i, 0)))(proj, proj, gain, ws, b3)


def _mix_gate_body(yps_ref, q_ref, ga_ref, gb_ref, wp_ref, ws_ref, m_ref, ya_ref, yb_ref):
    ya = _dot(yps_ref[...], wp_ref[...])
    yb = _dot(q_ref[...], ws_ref[...])
    ya_ref[...] = ya.astype(ya_ref.dtype)
    yb_ref[...] = yb.astype(yb_ref.dtype)
    m = _sigmoid(ga_ref[...].astype(F32)) * ya + _sigmoid(gb_ref[...].astype(F32)) * yb
    m_ref[...] = m.astype(m_ref.dtype)


def _mix_gate(yps, q, proj, gate_off, wpo, wso):
    t, pw = yps.shape
    sw = q.shape[1]
    n = wpo.shape[2]
    d = N_DEV * n
    tm = _row_tile(t)
    ga0 = gate_off // n
    gb0 = (gate_off + d) // n
    out = _sds((t, d), BF16)
    blk = pl.BlockSpec((tm, n), lambda j, i: (i, j))
    return _call(
        functools.partial(_mix_gate_body), name="mix_gate", out_shape=(out, out, out),
        grid=(N_DEV, t // tm),
        in_specs=[pl.BlockSpec((tm, pw), lambda j, i: (i, 0)),
                  pl.BlockSpec((tm, sw), lambda j, i: (i, 0)),
                  pl.BlockSpec((tm, n), lambda j, i: (i, ga0 + j)),
                  pl.BlockSpec((tm, n), lambda j, i: (i, gb0 + j)),
                  pl.BlockSpec((None, pw, n), lambda j, i: (j, 0, 0)),
                  pl.BlockSpec((None, sw, n), lambda j, i: (j, 0, 0))],
        out_specs=(blk, blk, blk))(yps, q, proj, proj, wpo, wso)


def _loss_body(y_ref, t_ref, dy_ref, l_ref, *, inv_d):
    i = pl.program_id(0)

    @pl.when(i == 0)
    def _():
        l_ref[...] = jnp.zeros_like(l_ref)

    e = y_ref[...] - t_ref[...]
    dy_ref[...] = e * inv_d
    l_ref[...] += jnp.sum(e * e)


def _loss_grad(y, target):
    t, d = y.shape
    tm = _row_tile(t)
    blk = pl.BlockSpec((tm, d), lambda i: (i, 0))
    return _call(
        functools.partial(_loss_body, inv_d=1.0 / d), name="loss_grad",
        out_shape=(_sds((t, d), F32), _sds((8, 128), F32)), grid=(t // tm,),
        in_specs=[blk, blk],
        out_specs=(blk, pl.BlockSpec((8, 128), lambda i: (0, 0))))(y, target)


def _norm_bwd_body(dy_ref, f_ref, g_ref, df_ref, dg_ref, *, scale):
    i = pl.program_id(0)

    @pl.when(i == 0)
    def _():
        dg_ref[...] = jnp.zeros_like(dg_ref)

    f = f_ref[...].astype(F32)
    rstd = _rstd(f)
    fn = f * rstd
    dr = scale * dy_ref[...]
    dg_ref[...] += jnp.sum(dr * fn, axis=0, keepdims=True)
    dfn = dr * g_ref[...]
    df = rstd * (dfn - fn * jnp.mean(dfn * fn, axis=-1, keepdims=True))
    df_ref[...] = df.astype(df_ref.dtype)


def _norm_res_bwd(dy, f, g, scale, after=()):
    t, d = dy.shape
    tm = _row_tile(t)
    blk = pl.BlockSpec((tm, d), lambda i: (i, 0))
    vec = pl.BlockSpec((1, d), lambda i: (0, 0))
    return _call(
        functools.partial(_norm_bwd_body, scale=scale), name="norm_res_bwd",
        out_shape=(_sds((t, d), BF16), _sds((1, d), F32)), grid=(t // tm,),
        in_specs=[blk, blk, vec], out_specs=(blk, vec), after=after)(dy, f, g)


def _dgrad_ffn_body(df_ref, w_ref, gu_ref, dz_ref):
    da = _dot_nt(df_ref[...], w_ref[...])
    g = gu_ref[0].astype(F32)
    u = gu_ref[1].astype(F32)
    sig = _sigmoid(g)
    dz_ref[0] = (da * u * (sig * (1.0 + g * (1.0 - sig)))).astype(dz_ref.dtype)
    dz_ref[1] = (da * (g * sig)).astype(dz_ref.dtype)


def _dgrad_ffn(df, wd, gu):
    t, d = df.shape
    f = wd.shape[0]
    tm = _row_tile(t)
    tn = f // 4
    blk = pl.BlockSpec((2, tm, tn), lambda j, i: (0, i, j))
    return _call(
        functools.partial(_dgrad_ffn_body), name="dgrad_ffn", out_shape=_sds((2, t, f), BF16),
        grid=(f // tn, t // tm),
        in_specs=[pl.BlockSpec((tm, d), lambda j, i: (i, 0)),
                  pl.BlockSpec((tn, d), lambda j, i: (j, 0)), blk],
        out_specs=blk)(df, wd, gu)


def _dgrad_gate_body(do_ref, w_ref, ya_ref, yb_ref, ga_ref, gb_ref, dya_ref, dyb_ref, dga_ref, dgb_ref):
    dm = _dot_nt(do_ref[...], w_ref[...])
    sa = _sigmoid(ga_ref[...].astype(F32))
    sb = _sigmoid(gb_ref[...].astype(F32))
    dya_ref[...] = (dm * sa).astype(dya_ref.dtype)
    dyb_ref[...] = (dm * sb).astype(dyb_ref.dtype)
    dga_ref[...] = (dm * ya_ref[...].astype(F32) * (sa * (1.0 - sa))).astype(dga_ref.dtype)
    dgb_ref[...] = (dm * yb_ref[...].astype(F32) * (sb * (1.0 - sb))).astype(dgb_ref.dtype)


def _dgrad_gate(do, wout, ya, yb, proj, gate_off):
    t, d = do.shape
    tm = _row_tile(t)
    tn = 512 if d % 512 == 0 else 256
    ga0 = gate_off // tn
    gb0 = (gate_off + d) // tn
    blk = pl.BlockSpec((tm, tn), lambda j, i: (i, j))
    out = _sds((t, d), BF16)
    return _call(
        functools.partial(_dgrad_gate_body), name="dgrad_gate", out_shape=(out, out, out, out),
        grid=(d // tn, t // tm),
        in_specs=[pl.BlockSpec((tm, d), lambda j, i: (i, 0)),
                  pl.BlockSpec((tn, d), lambda j, i: (j, 0)), blk, blk,
                  pl.BlockSpec((tm, tn), lambda j, i: (i, ga0 + j)),
                  pl.BlockSpec((tm, tn), lambda j, i: (i, gb0 + j))],
        out_specs=(blk, blk, blk, blk))(do, wout, ya, yb, proj, proj)


def _wgrad_rows_body(a_ref, b_ref, o_ref, acc_ref, *, nt):
    t = pl.program_id(1)

    @pl.when(t == 0)
    def _():
        acc_ref[...] = jnp.zeros_like(acc_ref)

    acc_ref[...] += _dot_tn(a_ref[...], b_ref[...])

    @pl.when(t == nt - 1)
    def _():
        o_ref[...] = acc_ref[...].astype(o_ref.dtype)


def _wgrad_rows(a, b, after=()):
    t, k = a.shape
    n = b.shape[1]
    tt = _big_row_tile(t)
    tk = k // 4
    nt = t // tt
    return _call(
        functools.partial(_wgrad_rows_body, nt=nt), name="wgrad_rows", out_shape=_sds((k, n), BF16),
        grid=(k // tk, nt),
        in_specs=[pl.BlockSpec((tt, tk), lambda r, s: (s, r)),
                  pl.BlockSpec((tt, n), lambda r, s: (s, 0))],
        out_specs=pl.BlockSpec((tk, n), lambda r, s: (r, 0)),
        scratch_shapes=[pltpu.VMEM((tk, n), F32)], after=after)(a, b)


def _wgrad_cs(a, b3):
    t, k = a.shape
    parts = b3.shape[0]
    per = N_DEV // parts
    n = b3.shape[2] // per
    tt = _big_row_tile(t)
    nt = t // tt
    return _call(
        functools.partial(_wgrad_rows_body, nt=nt), name="wgrad_cs", out_shape=_sds((N_DEV, k, n), BF16),
        grid=(N_DEV, nt),
        in_specs=[pl.BlockSpec((tt, k), lambda r, s: (s, 0)),
                  pl.BlockSpec((None, tt, n), lambda r, s: (r // per, s, r % per))],
        out_specs=pl.BlockSpec((None, k, n), lambda r, s: (r, 0, 0)),
        scratch_shapes=[pltpu.VMEM((k, n), F32)])(a, b3)


def _dgrad_cs_body(dz_ref, w_ref, o_ref, acc_ref):
    s = pl.program_id(1)

    @pl.when(s == 0)
    def _():
        acc_ref[...] = jnp.zeros_like(acc_ref)

    acc_ref[...] += _dot_nt(dz_ref[...], w_ref[...])

    @pl.when(s == N_DEV - 1)
    def _():
        o_ref[...] = acc_ref[...].astype(o_ref.dtype)


def _dgrad_cs(dz3, wg, after=()):
    parts, t, _ = dz3.shape
    per = N_DEV // parts
    k, n = wg.shape[1], wg.shape[2]
    tm = _row_tile(t)
    return _call(
        functools.partial(_dgrad_cs_body), name="dgrad_cs", out_shape=_sds((t, k), BF16),
        grid=(t // tm, N_DEV),
        in_specs=[pl.BlockSpec((None, tm, n), lambda i, s: (s // per, i, s % per)),
                  pl.BlockSpec((None, k, n), lambda i, s: (s, 0, 0))],
        out_specs=pl.BlockSpec((tm, k), lambda i, s: (i, 0)),
        scratch_shapes=[pltpu.VMEM((tm, k), F32)], after=after)(dz3, wg)


def _dgrad_cs_norm_body(dz_ref, w_ref, x_ref, g_ref, dy_ref, o_ref, dg_ref, *, tm):
    i = pl.program_id(0)
    s = pl.program_id(1)

    @pl.when(jnp.logical_and(i == 0, s == 0))
    def _():
        dg_ref[...] = jnp.zeros_like(dg_ref)

    @pl.when(s == 0)
    def _():
        o_ref[...] = jnp.zeros_like(o_ref)

    o_ref[...] += _dot_nt(dz_ref[...], w_ref[...])

    @pl.when(s == N_DEV - 1)
    def _():
        def finish(rows):
            dh = o_ref[rows, :]
            x = x_ref[rows, :]
            rstd = _rstd(x)
            xn = x * rstd
            dg_ref[...] += jnp.sum(dh * xn, axis=0, keepdims=True)
            dxn = dh * g_ref[...]
            o_ref[rows, :] = dy_ref[rows, :] + rstd * (dxn - xn * jnp.mean(dxn * xn, axis=-1, keepdims=True))

        _for_row_chunks(tm, finish)


def _dgrad_cs_norm(dz3, wg, x, g, dy, after=()):
    parts, t, _ = dz3.shape
    per = N_DEV // parts
    d, n = wg.shape[1], wg.shape[2]
    tm = _big_row_tile(t)
    blk = pl.BlockSpec((tm, d), lambda i, s: (i, 0), pipeline_mode=pl.Buffered(1))
    vec = pl.BlockSpec((1, d), lambda i, s: (0, 0))
    return _call(
        functools.partial(_dgrad_cs_norm_body, tm=tm), name="dgrad_cs_norm",
        out_shape=(_sds((t, d), F32), _sds((1, d), F32)), grid=(t // tm, N_DEV),
        in_specs=[pl.BlockSpec((None, tm, n), lambda i, s: (s // per, i, s % per)),
                  pl.BlockSpec((None, d, n), lambda i, s: (s, 0, 0)), blk, vec, blk],
        out_specs=(blk, vec), after=after)(dz3, wg, x, g, dy)


def _pool_bwd_body(p_ref, ph_ref, dy_ref, dyh_ref, w_ref, sc_ref, dp_ref, dw_ref, dsc_ref, *, tm, nt):
    g = pl.program_id(0)
    i = pl.program_id(1)

    @pl.when(i == 0)
    def _():
        dw_ref[...] = jnp.zeros_like(dw_ref)
        dsc_ref[...] = jnp.zeros_like(dsc_ref)

    w = w_ref[...]
    sc = sc_ref[...]
    d = _pool_delta(p_ref, ph_ref, g, i, tm).astype(BF16)
    dyps = dy_ref[...].astype(F32)
    dsc_ref[...] += jnp.sum(dyps * _dot(d, w), axis=0, keepdims=True)
    dyp = (dyps * sc).astype(BF16)
    dw_ref[...] += _dot_tn(d, dyp)
    dyp_after = (jnp.where(i == nt - 1, 0.0, dyh_ref[...].astype(F32)) * sc).astype(BF16)
    dd = jnp.concatenate([_dot_nt(dyp, w), _dot_nt(dyp_after, w)], axis=0)
    rows = tm + POOL_HALO
    t1 = i * tm + lax.broadcasted_iota(jnp.int32, (rows, 1), 0) + 1
    e = dd / jnp.minimum(t1, 2 << g).astype(F32)
    r2 = e + pltpu.roll(e, rows - 1, 0)
    r4 = r2 + pltpu.roll(r2, rows - 2, 0)
    r8 = r4 + pltpu.roll(r4, rows - 4, 0)
    r16 = r8 + pltpu.roll(r8, rows - 8, 0)
    r = jnp.where(g == 0, r2, jnp.where(g == 1, r4, jnp.where(g == 2, r8, r16)))
    dp_ref[...] = (r[:tm] - dd[:tm]).astype(dp_ref.dtype)


def _pool_bwd(proj, dyps, wgrp, scale):
    t = proj.shape[0]
    c = wgrp.shape[1]
    tm = _row_tile(t)
    nt = t // tm
    hb = tm // POOL_HALO
    last = t // POOL_HALO - 1
    main, halo = _pool_specs(t, c, tm)
    halo_after = pl.BlockSpec((POOL_HALO, c), lambda g, i: (jnp.minimum((i + 1) * hb, last), g))
    return _call(
        functools.partial(_pool_bwd_body, tm=tm, nt=nt), name="pool_bwd",
        out_shape=(_sds((t, N_POOL_GROUPS * c), BF16), _sds((N_POOL_GROUPS, c, c), F32),
                   _sds((1, N_POOL_GROUPS * c), F32)),
        grid=(N_POOL_GROUPS, nt),
        in_specs=[main, halo, main, halo_after, pl.BlockSpec((None, c, c), lambda g, i: (g, 0, 0)),
                  pl.BlockSpec((1, c), lambda g, i: (0, g))],
        out_specs=(main, pl.BlockSpec((None, c, c), lambda g, i: (g, 0, 0)),
                   pl.BlockSpec((1, c), lambda g, i: (0, g))))(proj, proj, dyps, dyps, wgrp, scale)


def _sgu_bwd_body(u_ref, v_ref, dq_ref, gain_ref, ws_ref, b_ref, duv_ref, dws_ref, db_ref, dgain_ref,
                  dvn_ref, *, tm, heads):
    i = pl.program_id(0)

    @pl.when(i == 0)
    def _():
        dws_ref[...] = jnp.zeros_like(dws_ref)
        db_ref[...] = jnp.zeros_like(db_ref)
        dgain_ref[...] = jnp.zeros_like(dgain_ref)

    sw = heads * SGU_HEAD
    u = u_ref[...].astype(F32)
    v = v_ref[...].astype(F32)
    ug = _gelu(u)
    vg = _gelu(v)
    rstd = _rstd(vg)
    vgn = vg * rstd
    gain = gain_ref[...]
    vn = (vgn * gain).astype(BF16)
    dq = dq_ref[...].astype(F32)
    for h in range(heads):
        wt = _tril_weights(ws_ref, h).astype(BF16)
        cols = slice(h * SGU_HEAD, (h + 1) * SGU_HEAD)
        dws = jnp.zeros((SGU_CHUNK, SGU_CHUNK), F32)
        dbh = jnp.zeros((SGU_CHUNK, 1), F32)
        for c in range(tm // SGU_CHUNK):
            rows = slice(c * SGU_CHUNK, (c + 1) * SGU_CHUNK)
            vn_c = vn[rows, cols]
            sg = _dot(wt, vn_c) + b_ref[h]
            dq_c = dq[rows, cols]
            dsg = dq_c * ug[rows, cols]
            duv_ref[rows, cols] = (dq_c * sg * _gelu_grad(u[rows, cols])).astype(duv_ref.dtype)
            dsg_b = dsg.astype(BF16)
            dws = dws + _dot_nt(dsg_b, vn_c)
            dbh = dbh + jnp.sum(dsg, axis=1, keepdims=True)
            dvn_ref[rows, cols] = _dot_tn(wt, dsg_b)
        dws_ref[h] += jnp.where(_tril_mask(), dws, 0.0)
        db_ref[h] += dbh
    dvn = dvn_ref[...]
    dgain_ref[...] += jnp.sum(dvn * vgn, axis=0, keepdims=True)
    dvgn = dvn * gain
    dvg = rstd * (dvgn - vgn * jnp.mean(dvgn * vgn, axis=-1, keepdims=True))
    duv_ref[:, sw:] = (dvg * _gelu_grad(v)).astype(duv_ref.dtype)


def _sgu_bwd(proj, pw, dq, gain, ws, b3):
    t = proj.shape[0]
    sw = gain.shape[1]
    heads = sw // SGU_HEAD
    tm = _row_tile(t)
    ub = pw // sw
    wblk = pl.BlockSpec((heads, SGU_CHUNK, SGU_CHUNK), lambda i: (0, 0, 0))
    bblk = pl.BlockSpec((heads, SGU_CHUNK, 1), lambda i: (0, 0, 0))
    vec = pl.BlockSpec((1, sw), lambda i: (0, 0))
    return _call(
        functools.partial(_sgu_bwd_body, tm=tm, heads=heads), name="sgu_bwd",
        out_shape=(_sds((t, 2 * sw), BF16), _sds((heads, SGU_CHUNK, SGU_CHUNK), F32),
                   _sds((heads, SGU_CHUNK, 1), F32), _sds((1, sw), F32)),
        grid=(t // tm,),
        in_specs=[pl.BlockSpec((tm, sw), lambda i: (i, ub)),
                  pl.BlockSpec((tm, sw), lambda i: (i, ub + 1)),
                  pl.BlockSpec((tm, sw), lambda i: (i, 0)), vec, wblk, bblk],
        out_specs=(pl.BlockSpec((tm, 2 * sw), lambda i: (i, 0)), wblk, bblk, vec),
        scratch_shapes=[pltpu.VMEM((tm, sw), F32)])(proj, proj, dq, gain, ws, b3)


def _cast_slot_body(me_ref, x_ref, o_ref):
    o_ref[...] = x_ref[...].astype(o_ref.dtype)


def _cast_to_slot(w, layer, me):
    _, r, c = w.shape
    tr = _slab_rows(r, 1024)
    spec = pltpu.PrefetchScalarGridSpec(
        num_scalar_prefetch=1, grid=(r // tr,),
        in_specs=[pl.BlockSpec((None, tr, c), lambda i, me_ref: (layer, i, 0))],
        out_specs=pl.BlockSpec((None, tr, c), lambda i, me_ref: (me_ref[0], i, 0)))
    return _call(functools.partial(_cast_slot_body), name="cast_to_slot", out_shape=_sds((N_DEV, r, c), BF16),
                 grid_spec=spec)(me, w)


def _mesh_position():
    x, y, c = (lax.axis_index(a) for a in MESH_AXES)
    return x, y, c


def _peers(x, y, c):
    out = []
    for m in range(1, N_DEV):
        px = 1 - x if m & 4 else x
        py = 1 - y if m & 2 else y
        pc = 1 - c if m & 1 else c
        out.append(((px, py, pc), 4 * px + 2 * py + pc))
    return out


def _remote(src, dst, send, recv, device):
    return pltpu.make_async_remote_copy(src_ref=src, dst_ref=dst, send_sem=send, recv_sem=recv,
                                        device_id=device, device_id_type=pl.DeviceIdType.MESH)


def _all_gather_body(src, out, send, recv, loc):
    x, y, c = _mesh_position()
    me = 4 * x + 2 * y + c
    peers = _peers(x, y, c)
    own = pltpu.make_async_copy(src, out.at[me], loc)
    own.start()
    for j, (peer, _) in enumerate(peers):
        _remote(src, out.at[me], send.at[j], recv.at[j], peer).start()
    for j, (peer, peer_idx) in enumerate(peers):
        cp = _remote(src, out.at[peer_idx], send.at[j], recv.at[j], peer)
        cp.wait_recv()
        cp.wait_send()
    own.wait()


def _all_gather(part, name):
    return _call(
        functools.partial(_all_gather_body), name=name, out_shape=_sds((N_DEV,) + part.shape, part.dtype),
        in_specs=[ANY_SPEC], out_specs=ANY_SPEC,
        scratch_shapes=[pltpu.SemaphoreType.DMA((N_PEERS,)), pltpu.SemaphoreType.DMA((N_PEERS,)),
                        pltpu.SemaphoreType.DMA(())])(part)


def _other_chips(x, y):
    return [(1 - x, y), (x, 1 - y), (1 - x, 1 - y)]


def _gather_send_own(bufs, send, recv):
    x, y, c = _mesh_position()
    me = 4 * x + 2 * y + c
    targets = [(x, y, 1 - c)] + [(px, py, c) for px, py in _other_chips(x, y)]
    for k, buf in enumerate(bufs):
        for j, dev in enumerate(targets):
            _remote(buf.at[me], buf.at[me], send.at[4 * k + j], recv.at[4 * k + j], dev).start()


def _gather_await_own(bufs, send, recv):
    x, y, c = _mesh_position()
    me = 4 * x + 2 * y + c
    targets = [(x, y, 1 - c)] + [(px, py, c) for px, py in _other_chips(x, y)]
    for k, buf in enumerate(bufs):
        for j, dev in enumerate(targets):
            cp = _remote(buf.at[me], buf.at[4 * dev[0] + 2 * dev[1] + dev[2]], send.at[4 * k + j],
                         recv.at[4 * k + j], dev)
            cp.wait_recv()
            cp.wait_send()


def _gather_relay_send(bufs, send, recv):
    x, y, c = _mesh_position()
    for k, buf in enumerate(bufs):
        for j, (px, py) in enumerate(_other_chips(x, y)):
            blk = buf.at[4 * px + 2 * py + c]
            _remote(blk, blk, send.at[3 * k + j], recv.at[3 * k + j], (x, y, 1 - c)).start()


def _gather_relay_await(bufs, send, recv):
    x, y, c = _mesh_position()
    for k, buf in enumerate(bufs):
        for j, (px, py) in enumerate(_other_chips(x, y)):
            cp = _remote(buf.at[4 * px + 2 * py + c], buf.at[4 * px + 2 * py + 1 - c], send.at[3 * k + j],
                         recv.at[3 * k + j], (x, y, 1 - c))
            cp.wait_recv()
            cp.wait_send()


def _reduce_pair_send(grads, stage, send, recv):
    x, y, c = _mesh_position()
    for k, (g, s) in enumerate(zip(grads, stage)):
        for q in range(4):
            _remote(g.at[2 * q + 1 - c], s.at[q], send.at[4 * k + q], recv.at[4 * k + q], (x, y, 1 - c)).start()


def _reduce_pair_await(grads, stage, send, recv):
    x, y, c = _mesh_position()
    for k, (g, s) in enumerate(zip(grads, stage)):
        for q in range(4):
            cp = _remote(g.at[2 * q + 1 - c], s.at[q], send.at[4 * k + q], recv.at[4 * k + q], (x, y, 1 - c))
            cp.wait_recv()
            cp.wait_send()


def _reduce_chip_send(sums, stage, send, recv):
    x, y, c = _mesh_position()
    for k, (p, s) in enumerate(zip(sums, stage)):
        for j, (px, py) in enumerate(_other_chips(x, y)):
            _remote(p.at[2 * px + py], s.at[2 * x + y], send.at[3 * k + j], recv.at[3 * k + j],
                    (px, py, c)).start()


def _reduce_chip_await(sums, stage, send, recv):
    x, y, c = _mesh_position()
    for k, (p, s) in enumerate(zip(sums, stage)):
        for j, (px, py) in enumerate(_other_chips(x, y)):
            cp = _remote(p.at[2 * px + py], s.at[2 * px + py], send.at[3 * k + j], recv.at[3 * k + j],
                         (px, py, c))
            cp.wait_recv()
            cp.wait_send()


def _split_body(*refs, n, n_sem_in, n_after, n_sem_out, steps, token):
    bufs = refs[:n]
    sems = refs[n:n + n_sem_in] + refs[n + n_sem_in + n_after:n + n_sem_in + n_after + n_sem_out]
    for step, lo in steps:
        step(bufs, sems[lo], sems[lo + 1])
    if token:
        refs[-1][...] = jnp.zeros_like(refs[-1])


def _split_call(steps, *, name, bufs, sems_in=(), sems_out=(), after=(), token=True):
    n = len(bufs)
    operands = [pltpu.with_memory_space_constraint(b, pltpu.HBM) for b in bufs] + list(sems_in) + list(after)
    in_specs = [HBM_SPEC] * n + [SEM_SPEC] * len(sems_in) + [ANY_SPEC] * len(after)
    out_shape = list(sems_out) + [pltpu.HBM(b.shape, b.dtype) for b in bufs]
    out_specs = [SEM_SPEC] * len(sems_out) + [HBM_SPEC] * n
    if token:
        out_shape.append(_sds((8, 128), F32))
        out_specs.append(pl.BlockSpec(memory_space=pltpu.VMEM))
    body = functools.partial(_split_body, n=n, n_sem_in=len(sems_in), n_after=len(after),
                             n_sem_out=len(sems_out), steps=tuple(steps), token=token)
    return pl.pallas_call(
        body, name=name, in_specs=in_specs, out_specs=tuple(out_specs), out_shape=tuple(out_shape),
        input_output_aliases={i: len(sems_out) + i for i in range(n)},
        compiler_params=pltpu.CompilerParams(
            has_side_effects=pltpu.SideEffectType.DATAFLOW_SIDE_EFFECTING))(*operands)


def _sem_pair(n, m):
    return [pltpu.SemaphoreType.DMA((n * m,)), pltpu.SemaphoreType.DMA((n * m,))]


def _gather_start(bufs, after, tag):
    n = len(bufs)
    out = _split_call([(_gather_send_own, 0)], name="gather_start_" + tag, bufs=bufs,
                      sems_out=_sem_pair(n, 4), after=after)
    return out[0], out[1], list(out[2:2 + n]), out[-1]


def _gather_relay(state, after, tag):
    send, recv, bufs, _ = state
    n = len(bufs)
    out = _split_call([(_gather_await_own, 0), (_gather_relay_send, 2)], name="gather_relay_" + tag, bufs=bufs,
                      sems_in=[send, recv], sems_out=_sem_pair(n, 3), after=after)
    return out[0], out[1], list(out[2:2 + n]), out[-1]


def _gather_finish(state, after, tag):
    send, recv, bufs, _ = state
    out = _split_call([(_gather_relay_await, 0)], name="gather_finish_" + tag, bufs=bufs,
                      sems_in=[send, recv], after=after, token=False)
    return list(out)


def _pair_arrays(n):
    def split(step):
        return lambda bufs, send, recv: step(bufs[:n], bufs[n:], send, recv)
    return split


def _reduce_pairs_start(grads, after, tag):
    n = len(grads)
    stage = [lax.empty((4,) + g.shape[1:], g.dtype) for g in grads]
    out = _split_call([(_pair_arrays(n)(_reduce_pair_send), 0)], name="reduce_pairs_start_" + tag,
                      bufs=list(grads) + stage, sems_out=_sem_pair(n, 4), after=after)
    return out[0], out[1], list(out[2:2 + n]), list(out[2 + n:2 + 2 * n]), out[-1]


def _reduce_pairs_finish(state, after, tag):
    send, recv, grads, stage, _ = state
    n = len(grads)
    out = _split_call([(_pair_arrays(n)(_reduce_pair_await), 0)], name="reduce_pairs_finish_" + tag,
                      bufs=list(grads) + list(stage), sems_in=[send, recv], after=after, token=False)
    return list(out[:n]), list(out[n:])


def _reduce_chips_start(sums, after, tag):
    n = len(sums)
    stage = [lax.empty(p.shape, p.dtype) for p in sums]
    out = _split_call([(_pair_arrays(n)(_reduce_chip_send), 0)], name="reduce_chips_start_" + tag,
                      bufs=list(sums) + stage, sems_out=_sem_pair(n, 3), after=after)
    return out[0], out[1], list(out[2:2 + n]), list(out[2 + n:2 + 2 * n]), out[-1]


def _reduce_chips_finish(state, after, tag):
    send, recv, sums, stage, _ = state
    n = len(sums)
    out = _split_call([(_pair_arrays(n)(_reduce_chip_await), 0)], name="reduce_chips_finish_" + tag,
                      bufs=list(sums) + list(stage), sems_in=[send, recv], after=after, token=False)
    return list(out[:n]), list(out[n:])


def _pair_sum_body(c_ref, a_ref, b_ref, o_ref):
    o_ref[...] = (a_ref[...].astype(F32) + b_ref[...].astype(F32)).astype(o_ref.dtype)


def _pair_sum(grad, stage, core):
    _, r, c = grad.shape
    tr = _slab_rows(r, 1024)
    blk = pl.BlockSpec((None, tr, c), lambda q, i, c_ref: (q, i, 0))
    spec = pltpu.PrefetchScalarGridSpec(
        num_scalar_prefetch=1, grid=(4, r // tr),
        in_specs=[pl.BlockSpec((None, tr, c), lambda q, i, c_ref: (2 * q + c_ref[0], i, 0)), blk],
        out_specs=blk)
    return _call(functools.partial(_pair_sum_body), name="pair_sum", out_shape=_sds((4, r, c), grad.dtype),
                 grid_spec=spec)(core, grad, stage)


def _adamw_math(w, g, m, v):
    m = ADAM_B1 * m + (1.0 - ADAM_B1) * g
    v = ADAM_B2 * v + (1.0 - ADAM_B2) * (g * g)
    m_hat = m / (1.0 - ADAM_B1 ** ADAM_STEP)
    v_hat = v / (1.0 - ADAM_B2 ** ADAM_STEP)
    delta = -ADAM_LR * (m_hat / (jnp.sqrt(v_hat) + ADAM_EPS) + ADAM_WD * w)
    return delta, m, v


def _adamw_body(q_ref, own_ref, r_ref, w_ref, m_ref, v_ref, *rest):
    g_out, d_out, m_out, v_out = rest[-4:]
    q = q_ref[0]
    g = None
    for j in range(4):
        term = jnp.where(q == j, own_ref[...], r_ref[j]).astype(F32)
        g = term if g is None else g + term
    delta, m, v = _adamw_math(w_ref[...], g, m_ref[...], v_ref[...])
    g_out[...] = g
    d_out[...] = delta
    m_out[...] = m
    v_out[...] = v


def _adamw_layer(sums, recv, w, m, v, prev, layer, chip):
    l, r, c = w.shape
    tr = _slab_rows(r)
    lay = pl.BlockSpec((None, tr, c), lambda i, q_ref: (layer, i, 0))
    spec = pltpu.PrefetchScalarGridSpec(
        num_scalar_prefetch=1, grid=(r // tr,),
        in_specs=[pl.BlockSpec((None, tr, c), lambda i, q_ref: (q_ref[0], i, 0)),
                  pl.BlockSpec((4, tr, c), lambda i, q_ref: (0, i, 0)), lay, lay, lay] + [ANY_SPEC] * 4,
        out_specs=(lay, lay, lay, lay))
    out = _sds((l, r, c), F32)
    return _call(
        functools.partial(_adamw_body), name="adamw", out_shape=(out, out, out, out), grid_spec=spec,
        aliases={6: 0, 7: 1, 8: 2, 9: 3})(chip, sums, recv, w, m, v, *prev)


def _adamw_small_body(r_ref, w_ref, m_ref, v_ref, g_out, d_out, m_out, v_out):
    g = r_ref[0]
    for j in range(1, N_DEV):
        g = g + r_ref[j]
    delta, m, v = _adamw_math(w_ref[...], g, m_ref[...], v_ref[...])
    g_out[...] = g
    d_out[...] = delta
    m_out[...] = m
    v_out[...] = v


def _adamw_small(recv, w, m, v):
    r, c = w.shape
    tr = 8
    for cand in (1024, 512, 256, 128, 64, 32, 16, 8):
        if r % cand == 0:
            tr = cand
            break
    blk = pl.BlockSpec((tr, c), lambda i: (i, 0))
    out = _sds((r, c), F32)
    return _call(
        functools.partial(_adamw_small_body), name="adamw_small", out_shape=(out, out, out, out),
        grid=(r // tr,),
        in_specs=[pl.BlockSpec((N_DEV, tr, c), lambda i: (0, i, 0)), blk, blk, blk],
        out_specs=(blk, blk, blk, blk))(recv, w, m, v)


def _as3d(a):
    return a.reshape(a.shape[0], -1, a.shape[-1])


GROUPS = (("w_ffn1_up", "w_ffn1_down"),
          ("w_in", "pool_group_w", "w_pool_out", "w_sgu_out", "w_out"),
          ("w_ffn2_up", "w_ffn2_down"))


def _ffn_forward(x, g_pre, wu, wd, g_post, after=(), mid=None):
    h = _rms_fwd(x, g_pre, after=after)
    gu, a = _ffn_up(h, wu)
    tokens = tuple(mid(a)) if mid else ()
    wd = wd() if callable(wd) else wd
    x_out, f = _down_norm_res(a, wd, x, g_post, MACARON_WEIGHT, after=tokens)
    return x_out, (wd, (x, h, gu, a, f))


def _ffn_backward(dy, saved, g_pre, wu, wd, g_post, mid=None, early=None):
    x, h, gu, a, f = saved
    df, dg_post = _norm_res_bwd(dy, f, g_post, MACARON_WEIGHT)
    dz = _dgrad_ffn(df, wd, gu)
    tokens = tuple(mid(dz)) if mid else ()
    dwd = _wgrad_rows(a, df).reshape(N_DEV, -1, df.shape[1])
    dwu = _wgrad_cs(h, dz)
    tokens += tuple(early([dwu, dwd])) if early else (dwu, dwd)
    dy, dg_pre = _dgrad_cs_norm(dz, wu, x, g_pre, dy, after=tokens)
    return dy, dg_pre, dg_post


def kernel(x, g_ffn1_pre, w_ffn1_up, w_ffn1_down, g_ffn1_post, g_mix_pre, w_in, pool_group_w, pool_scale, w_pool_out, sgu_v_gain, sgu_w_s, sgu_b_s, w_sgu_out, w_out, g_mix_post, g_ffn2_pre, w_ffn2_up, w_ffn2_down, g_ffn2_post, loss_target, m_g_ffn1_pre, m_w_ffn1_up, m_w_ffn1_down, m_g_ffn1_post, m_g_mix_pre, m_w_in, m_pool_group_w, m_pool_scale, m_w_pool_out, m_sgu_v_gain, m_sgu_w_s, m_sgu_b_s, m_w_sgu_out, m_w_out, m_g_mix_post, m_g_ffn2_pre, m_w_ffn2_up, m_w_ffn2_down, m_g_ffn2_post, v_g_ffn1_pre, v_w_ffn1_up, v_w_ffn1_down, v_g_ffn1_post, v_g_mix_pre, v_w_in, v_pool_group_w, v_pool_scale, v_w_pool_out, v_sgu_v_gain, v_sgu_w_s, v_sgu_b_s, v_w_sgu_out, v_w_out, v_g_mix_post, v_g_ffn2_pre, v_w_ffn2_up, v_w_ffn2_down, v_g_ffn2_post):
    given = dict(locals())
    weights = {n: given[n] for n in ORDER}
    mom_m = {n: given["m_" + n] for n in ORDER}
    mom_v = {n: given["v_" + n] for n in ORDER}
    depth = g_ffn1_pre.shape[0]
    d_model = x.shape[-1]
    pool_w = pool_scale.shape[1]
    sgu_w = sgu_v_gain.shape[1]
    groups, grp_rows, grp_c = pool_group_w.shape[1:]
    gate_off = pool_w + 2 * sgu_w

    xs = x.reshape(x.shape[-2], d_model)
    target = loss_target.reshape(xs.shape)

    px, py, pc = _mesh_position()
    me = (4 * px + 2 * py + pc).astype(jnp.int32).reshape(1)
    core = pc.astype(jnp.int32).reshape(1)
    chip = (2 * px + py).astype(jnp.int32).reshape(1)

    n_groups = len(GROUPS) * depth
    ffn_gains = {0: ("g_ffn1_pre", "g_ffn1_post"), 2: ("g_ffn2_pre", "g_ffn2_post")}

    def start_gather(t, after):
        layer, gi = divmod(t, len(GROUPS))
        bufs = [_cast_to_slot(_as3d(weights[n]), layer, me) for n in GROUPS[gi]]
        return _gather_start(bufs, after, "g%d" % t)

    def vec(name, i):
        return weights[name][i].reshape(1, -1)

    def forward_group(t, cur, bufs, after, mid):
        layer, gi = divmod(t, len(GROUPS))
        gw = dict(zip(GROUPS[gi], bufs))
        if gi != 1:
            pre, post = ffn_gains[gi]
            wu, wd = (gw[n] for n in GROUPS[gi])
            wd = wd if callable(wd) else wd.reshape(-1, d_model)
            cur, (wd, sv) = _ffn_forward(cur, vec(pre, layer), wu, wd, vec(post, layer), after=after, mid=mid)
            return cur, ((wu, wd), sv)
        wgrp = gw["pool_group_w"].reshape(N_DEV, groups, grp_rows, grp_c).transpose(1, 0, 2, 3)
        wts = dict(win=gw["w_in"], wgrp=wgrp.reshape(groups, grp_c, grp_c), wpo=gw["w_pool_out"],
                   wso=gw["w_sgu_out"], wout=gw["w_out"].reshape(-1, d_model),
                   ws=weights["sgu_w_s"][layer], b3=weights["sgu_b_s"][layer][:, :, None])
        x1 = cur
        h2 = _rms_fwd(x1, vec("g_mix_pre", layer), after=after)
        proj = _mm_cs(h2, wts["win"])
        tokens = tuple(mid(proj))
        yps = _pool_fwd(proj, wts["wgrp"], vec("pool_scale", layer))
        q = _sgu_fwd(proj, pool_w, vec("sgu_v_gain", layer), wts["ws"], wts["b3"])
        m, ya, yb = _mix_gate(yps, q, proj, gate_off, wts["wpo"], wts["wso"])
        cur, o = _down_norm_res(m, wts["wout"], x1, vec("g_mix_post", layer), 1.0, after=tokens)
        return cur, (wts, (x1, h2, proj, yps, q, ya, yb, m, o))

    first = [_gather_start([_cast_to_slot(_as3d(weights[n]), 0, me)], (), "g0" + n[-2:]) for n in GROUPS[0]]
    states = {1: start_gather(1, (first[0][-1], first[1][-1]))}
    up0 = _gather_finish(_gather_relay(first[0], (states[1][-1],), "g0up"), (), "g0up")
    down0 = []
    bufs = [up0[0], lambda: down0[0][0].reshape(-1, d_model)]
    saved = []
    cur = xs
    for t in range(n_groups):
        after = []
        if t + 2 < n_groups:
            states[t + 2] = start_gather(t + 2, (bufs[0],))
            after.append(states[t + 2][-1])
        relayed = []

        def mid(result, t=t, relayed=relayed):
            if t == 0:
                down0.append(_gather_finish(_gather_relay(first[1], (result,), "g0wn"), (), "g0wn"))
            if t + 1 < n_groups:
                relayed.append(_gather_relay(states[t + 1], (result,), "g%d" % (t + 1)))
            return tuple(r[-1] for r in relayed)

        cur, sv = forward_group(t, cur, bufs, tuple(after), mid)
        saved.append(sv)
        if t + 1 < n_groups:
            bufs = _gather_finish(relayed[0], (cur,), "g%d" % (t + 1))

    dy, sq = _loss_grad(cur, target)
    loss = lax.psum(0.5 * sq[0, 0] / d_model, MESH_AXES)

    small_grads = {n: [None] * depth for n in SMALL}
    outs = {n: None for n in BIG}

    def chip_sums(pairs, after, tag):
        grads, stage = _reduce_pairs_finish(pairs, after, tag)
        sums = [_pair_sum(g, s, core) for g, s in zip(grads, stage)]
        return _reduce_chips_start(sums, (), tag)

    def update(chips, after, t):
        layer, gi = divmod(t, len(GROUPS))
        sums, recv = _reduce_chips_finish(chips, after, "g%d" % t)
        for n, p, r in zip(GROUPS[gi], sums, recv):
            w3, m3, v3 = _as3d(weights[n]), _as3d(mom_m[n]), _as3d(mom_v[n])
            prev = outs[n] if outs[n] is not None else [lax.empty(w3.shape, F32) for _ in range(4)]
            outs[n] = _adamw_layer(p, r, w3, m3, v3, prev, layer, chip)

    def backward_group(t, dy, sv, mid, early):
        layer, gi = divmod(t, len(GROUPS))
        wts, sv = sv
        if gi != 1:
            pre, post = ffn_gains[gi]
            dy, dg, dgp = _ffn_backward(dy, sv, vec(pre, layer), wts[0], wts[1], vec(post, layer),
                                        mid=mid, early=early)
            small_grads[pre][layer], small_grads[post][layer] = dg, dgp
            return dy
        x1, h2, proj, yps, q, ya, yb, m, o = sv
        do, small_grads["g_mix_post"][layer] = _norm_res_bwd(dy, o, vec("g_mix_post", layer), 1.0)
        dya, dyb, dga, dgb = _dgrad_gate(do, wts["wout"], ya, yb, proj, gate_off)
        tokens = tuple(mid(dya)) if mid else ()
        dwout = _wgrad_rows(m, do).reshape(N_DEV, -1, d_model)
        dwpo = _wgrad_cs(yps, dya[None])
        dwso = _wgrad_cs(q, dyb[None])
        dyps = _dgrad_cs(dya[None], wts["wpo"], after=tokens)
        dq = _dgrad_cs(dyb[None], wts["wso"])
        dp, dwgrp, small_grads["pool_scale"][layer] = _pool_bwd(proj, dyps, wts["wgrp"], vec("pool_scale", layer))
        duv, dws, db3, small_grads["sgu_v_gain"][layer] = _sgu_bwd(
            proj, pool_w, dq, vec("sgu_v_gain", layer), wts["ws"], wts["b3"])
        small_grads["sgu_w_s"][layer] = dws
        small_grads["sgu_b_s"][layer] = db3
        dproj = jnp.concatenate([dp, duv, dga, dgb], axis=1)[None]
        dwin = _wgrad_cs(h2, dproj)
        dwgrp8 = dwgrp.reshape(groups, N_DEV, grp_rows, grp_c).transpose(1, 0, 2, 3)
        dwgrp8 = dwgrp8.reshape(N_DEV, groups * grp_rows, grp_c).astype(BF16)
        dy, small_grads["g_mix_pre"][layer] = _dgrad_cs_norm(
            dproj, wts["win"], x1, vec("g_mix_pre", layer), dy,
            after=tuple(early([dwin, dwgrp8, dwpo, dwso, dwout])))
        return dy

    pairs = None
    for t in reversed(range(n_groups)):
        started, mine = [], []

        def mid(first, t=t, started=started):
            started.append(chip_sums(pairs, (first,), "g%d" % (t + 1)))
            return (started[0][-1],)

        def early(grads, t=t, mine=mine):
            mine.append(_reduce_pairs_start(grads, (), "g%d" % t))
            return (mine[0][-1],)

        dy = backward_group(t, dy, saved[t], mid if pairs else None, early)
        if started:
            update(started[0], (dy,), t + 1)
        pairs = mine[0]
    last = chip_sums(pairs, (), "g0")

    def pack(parts):
        return jnp.concatenate([p.reshape(depth, -1) for p in parts], axis=1).reshape(-1, 128)

    part = pack([jnp.stack(small_grads[n]) for n in SMALL])
    all_parts = _all_gather(part, "gather_small_grads")
    small_out = _adamw_small(all_parts, pack([weights[n] for n in SMALL]), pack([mom_m[n] for n in SMALL]),
                             pack([mom_v[n] for n in SMALL]))
    done = [outs[n][0] for n in BIG if outs[n] is not None]
    update(last, (small_out[0], *done), 0)

    results = {}
    sizes = [weights[n][0].size for n in SMALL]
    for kind, packed in zip(("grad", "delta", "new_m", "new_v"), small_out):
        flat = packed.reshape(depth, -1)
        off = 0
        for n, sz in zip(SMALL, sizes):
            results[(kind, n)] = flat[:, off:off + sz].reshape(weights[n].shape)
            off += sz
    for n in BIG:
        for kind, arr in zip(("grad", "delta", "new_m", "new_v"), outs[n]):
            results[(kind, n)] = arr.reshape(weights[n].shape)

    grad_x = dy.reshape(x.shape)
    return (loss, grad_x, *[results[("grad", n)] for n in ORDER], *[results[("delta", n)] for n in ORDER],
            *[results[("new_m", n)] for n in ORDER], *[results[("new_v", n)] for n in ORDER])
```

```python
import functools

import jax
import jax.numpy as jnp
from jax import lax
from jax.experimental import pallas as pl
from jax.experimental.pallas import tpu as pltpu

F32 = jnp.float32
BF16 = jnp.bfloat16

N_DEV = 8
MESH_AXES = ("x", "y", "c")

EPS = 1e-6
MACARON_WEIGHT = 0.5
N_POOL_GROUPS = 4
POOL_HALO = 16
SGU_HEAD = 128
SGU_CHUNK = 128
GELU_C = 0.7978845608028654
GELU_A = 0.044715

ADAM_LR = 0.001
ADAM_B1 = 0.9
ADAM_B2 = 0.999
ADAM_EPS = 1e-08
ADAM_WD = 0.01
ADAM_STEP = 10

ROW_TILE = 512
VMEM_LIMIT_BYTES = 56 * 1024 * 1024

BIG = ("w_ffn1_up", "w_ffn1_down", "w_in", "pool_group_w", "w_pool_out", "w_sgu_out", "w_out",
       "w_ffn2_up", "w_ffn2_down")
SMALL = ("g_ffn1_pre", "g_ffn1_post", "g_mix_pre", "pool_scale", "sgu_v_gain", "sgu_w_s", "sgu_b_s",
         "g_mix_post", "g_ffn2_pre", "g_ffn2_post")
ORDER = ("g_ffn1_pre", "w_ffn1_up", "w_ffn1_down", "g_ffn1_post", "g_mix_pre", "w_in", "pool_group_w",
         "pool_scale", "w_pool_out", "sgu_v_gain", "sgu_w_s", "sgu_b_s", "w_sgu_out", "w_out",
         "g_mix_post", "g_ffn2_pre", "w_ffn2_up", "w_ffn2_down", "g_ffn2_post")


ANY_SPEC = pl.BlockSpec(memory_space=pl.ANY)
HBM_SPEC = pl.BlockSpec(memory_space=pltpu.HBM)
SEM_SPEC = pl.BlockSpec(memory_space=pltpu.SEMAPHORE)


def _skip_refs(body, n_in, n_skip, *refs):
    return body(*refs[:n_in], *refs[n_in + n_skip:])


def _call(body, *, name, out_shape, grid=(), in_specs=None, out_specs=None, scratch_shapes=(),
          aliases=None, after=(), grid_spec=None):
    extra = {}
    if grid_spec is not None:
        extra["grid_spec"] = grid_spec
    else:
        extra.update(in_specs=list(in_specs) + [ANY_SPEC] * len(after), out_specs=out_specs,
                     scratch_shapes=list(scratch_shapes))
        if grid:
            extra["grid"] = grid
        if after:
            body = functools.partial(_skip_refs, body, len(in_specs), len(after))
    if aliases:
        extra["input_output_aliases"] = aliases
    call = pl.pallas_call(
        body, name=name, out_shape=out_shape,
        compiler_params=pltpu.CompilerParams(vmem_limit_bytes=VMEM_LIMIT_BYTES), **extra)
    if after:
        return lambda *args: call(*args, *after)
    return call


def _sds(shape, dtype):
    return jax.ShapeDtypeStruct(tuple(shape), dtype)


def _dot(a, b):
    return jnp.dot(a, b, preferred_element_type=F32)


def _dot_nt(a, b):
    return lax.dot_general(a, b, (((1,), (1,)), ((), ())), preferred_element_type=F32)


def _dot_tn(a, b):
    return lax.dot_general(a, b, (((0,), (0,)), ((), ())), preferred_element_type=F32)


def _sigmoid(x):
    return 1.0 / (1.0 + jnp.exp(-x))


def _gelu(x):
    return 0.5 * x * (1.0 + jnp.tanh(GELU_C * (x + GELU_A * x * x * x)))


def _gelu_grad(x):
    t = jnp.tanh(GELU_C * (x + GELU_A * x * x * x))
    return 0.5 * (1.0 + t) + 0.5 * x * (1.0 - t * t) * (GELU_C * (1.0 + 3.0 * GELU_A * x * x))


def _rstd(x):
    return lax.rsqrt(jnp.mean(x * x, axis=-1, keepdims=True) + EPS)


def _row_tile(t):
    return min(t, ROW_TILE)


def _big_row_tile(t):
    return 2 * ROW_TILE if t % (2 * ROW_TILE) == 0 else _row_tile(t)


def _slab_rows(r, most=256):
    for cand in range(min(most, r) // 16 * 16, 15, -16):
        if r % cand == 0:
            return cand
    return r


def _rms_fwd_body(x_ref, g_ref, h_ref):
    x = x_ref[...]
    h_ref[...] = ((x * _rstd(x)) * g_ref[...]).astype(h_ref.dtype)


def _rms_fwd(x, g, after=()):
    t, d = x.shape
    tm = _row_tile(t)
    return _call(
        functools.partial(_rms_fwd_body), name="rms_fwd", out_shape=_sds((t, d), BF16),
        grid=(t // tm,),
        in_specs=[pl.BlockSpec((tm, d), lambda i: (i, 0)), pl.BlockSpec((1, d), lambda i: (0, 0))],
        out_specs=pl.BlockSpec((tm, d), lambda i: (i, 0)), after=after)(x, g)


def _mm_cs_body(h_ref, w_ref, o_ref):
    o_ref[...] = _dot(h_ref[...], w_ref[...]).astype(o_ref.dtype)


def _mm_cs(h, wg):
    t, k = h.shape
    n = wg.shape[2]
    tm = _row_tile(t)
    return _call(
        functools.partial(_mm_cs_body), name="mm_cs", out_shape=_sds((t, N_DEV * n), BF16),
        grid=(N_DEV, t // tm),
        in_specs=[pl.BlockSpec((tm, k), lambda j, i: (i, 0)),
                  pl.BlockSpec((None, k, n), lambda j, i: (j, 0, 0))],
        out_specs=pl.BlockSpec((tm, n), lambda j, i: (i, j)))(h, wg)


def _ffn_up_body(h_ref, wg_ref, wu_ref, gu_ref, a_ref):
    h = h_ref[...]
    g = _dot(h, wg_ref[...])
    u = _dot(h, wu_ref[...])
    sig = _sigmoid(g)
    silu = g * sig
    gu_ref[0] = (u * (sig * (1.0 + g * (1.0 - sig)))).astype(gu_ref.dtype)
    gu_ref[1] = silu.astype(gu_ref.dtype)
    a_ref[...] = (silu * u).astype(a_ref.dtype)


def _ffn_up(h, wg):
    t, k = h.shape
    n = wg.shape[2]
    half = N_DEV // 2
    f = half * n
    tm = _row_tile(t)
    return _call(
        functools.partial(_ffn_up_body), name="ffn_up",
        out_shape=(_sds((2, t, f), BF16), _sds((t, f), BF16)),
        grid=(half, t // tm),
        in_specs=[pl.BlockSpec((tm, k), lambda j, i: (i, 0)),
                  pl.BlockSpec((None, k, n), lambda j, i: (j, 0, 0)),
                  pl.BlockSpec((None, k, n), lambda j, i: (j + N_DEV // 2, 0, 0))],
        out_specs=(pl.BlockSpec((2, tm, n), lambda j, i: (0, i, j)),
                   pl.BlockSpec((tm, n), lambda j, i: (i, j))))(h, wg, wg)


EPILOGUE_ROWS = 256


def _for_row_chunks(rows, step):
    chunk = min(rows, EPILOGUE_ROWS)

    def body(r, carry):
        step(pl.ds(pl.multiple_of(r * chunk, chunk), chunk))
        return carry

    lax.fori_loop(0, rows // chunk, body, 0)


def _down_body(a_ref, w_ref, x_ref, g_ref, xo_ref, f_ref, *, scale, nk, tm):
    k = pl.program_id(1)

    @pl.when(k == 0)
    def _():
        xo_ref[...] = jnp.zeros_like(xo_ref)

    xo_ref[...] += _dot(a_ref[...], w_ref[...])

    @pl.when(k == nk - 1)
    def _():
        def finish(rows):
            f = xo_ref[rows, :]
            f_ref[rows, :] = f.astype(f_ref.dtype)
            xo_ref[rows, :] = x_ref[rows, :] + scale * ((f * _rstd(f)) * g_ref[...])

        _for_row_chunks(tm, finish)


def _down_norm_res(a, w, x, g, scale, after=()):
    t, kk = a.shape
    d = w.shape[1]
    tm = _big_row_tile(t)
    tk = 512 if kk % 512 == 0 else 256
    nk = kk // tk
    once = pl.Buffered(1)
    return _call(
        functools.partial(_down_body, scale=scale, nk=nk, tm=tm), name="down_norm_res",
        out_shape=(_sds((t, d), F32), _sds((t, d), BF16)),
        grid=(t // tm, nk),
        in_specs=[pl.BlockSpec((tm, tk), lambda i, k: (i, k)),
                  pl.BlockSpec((tk, d), lambda i, k: (k, 0)),
                  pl.BlockSpec((tm, d), lambda i, k: (i, 0), pipeline_mode=once),
                  pl.BlockSpec((1, d), lambda i, k: (0, 0))],
        out_specs=(pl.BlockSpec((tm, d), lambda i, k: (i, 0)), pl.BlockSpec((tm, d), lambda i, k: (i, 0))),
        after=after)(a, w, x, g)


def _pool_inv_count(g, i, tm):
    t1 = i * tm + lax.broadcasted_iota(jnp.int32, (tm, 1), 0) + 1
    return 1.0 / jnp.minimum(t1, 2 << g).astype(F32)


def _pool_windows(e, g):
    s2 = e + pltpu.roll(e, 1, 0)
    s4 = s2 + pltpu.roll(s2, 2, 0)
    s8 = s4 + pltpu.roll(s4, 4, 0)
    s16 = s8 + pltpu.roll(s8, 8, 0)
    return jnp.where(g == 0, s2, jnp.where(g == 1, s4, jnp.where(g == 2, s8, s16)))


def _pool_delta(p_ref, ph_ref, g, i, tm):
    pm = p_ref[...].astype(F32)
    ph = jnp.where(i == 0, 0.0, ph_ref[...].astype(F32))
    e = jnp.concatenate([ph, pm], axis=0)
    s = _pool_windows(e, g)[POOL_HALO:]
    return s * _pool_inv_count(g, i, tm) - pm


def _pool_fwd_body(p_ref, ph_ref, w_ref, sc_ref, o_ref, *, tm):
    g = pl.program_id(0)
    i = pl.program_id(1)
    d = _pool_delta(p_ref, ph_ref, g, i, tm)
    o_ref[...] = (_dot(d.astype(BF16), w_ref[...]) * sc_ref[...]).astype(o_ref.dtype)


def _pool_specs(t, c, tm):
    hb = tm // POOL_HALO
    main = pl.BlockSpec((tm, c), lambda g, i: (i, g))
    halo_before = pl.BlockSpec((POOL_HALO, c), lambda g, i: (jnp.maximum(i * hb - 1, 0), g))
    return main, halo_before


def _pool_fwd(proj, wgrp, scale):
    t = proj.shape[0]
    c = wgrp.shape[1]
    tm = _row_tile(t)
    main, halo = _pool_specs(t, c, tm)
    return _call(
        functools.partial(_pool_fwd_body, tm=tm), name="pool_fwd",
        out_shape=_sds((t, N_POOL_GROUPS * c), BF16), grid=(N_POOL_GROUPS, t // tm),
        in_specs=[main, halo, pl.BlockSpec((None, c, c), lambda g, i: (g, 0, 0)),
                  pl.BlockSpec((1, c), lambda g, i: (0, g))],
        out_specs=pl.BlockSpec((tm, c), lambda g, i: (i, g)))(proj, proj, wgrp, scale)


def _tril_mask():
    r = lax.broadcasted_iota(jnp.int32, (SGU_CHUNK, SGU_CHUNK), 0)
    c = lax.broadcasted_iota(jnp.int32, (SGU_CHUNK, SGU_CHUNK), 1)
    return r >= c


def _tril_weights(ws_ref, h):
    return jnp.where(_tril_mask(), ws_ref[h], 0.0)


def _sgu_fwd_body(u_ref, v_ref, gain_ref, ws_ref, b_ref, q_ref, *, tm, heads):
    ug = _gelu(u_ref[...].astype(F32))
    vg = _gelu(v_ref[...].astype(F32))
    vn = ((vg * _rstd(vg)) * gain_ref[...]).astype(BF16)
    for h in range(heads):
        wt = _tril_weights(ws_ref, h).astype(BF16)
        cols = slice(h * SGU_HEAD, (h + 1) * SGU_HEAD)
        for c in range(tm // SGU_CHUNK):
            rows = slice(c * SGU_CHUNK, (c + 1) * SGU_CHUNK)
            sg = _dot(wt, vn[rows, cols]) + b_ref[h]
            q_ref[rows, cols] = (ug[rows, cols] * sg).astype(q_ref.dtype)


def _sgu_fwd(proj, pw, gain, ws, b3):
    t = proj.shape[0]
    sw = gain.shape[1]
    heads = sw // SGU_HEAD
    tm = _row_tile(t)
    ub = pw // sw
    return _call(
        functools.partial(_sgu_fwd_body, tm=tm, heads=heads), name="sgu_fwd",
        out_shape=_sds((t, sw), BF16), grid=(t // tm,),
        in_specs=[pl.BlockSpec((tm, sw), lambda i: (i, ub)),
                  pl.BlockSpec((tm, sw), lambda i: (i, ub + 1)),
                  pl.BlockSpec((1, sw), lambda i: (0, 0)),
                  pl.BlockSpec((heads, SGU_CHUNK, SGU_CHUNK), lambda i: (0, 0, 0)),
                  pl.BlockSpec((heads, SGU_CHUNK, 1), lambda i: (0, 0, 0))],
        out_specs=pl.BlockSpec((tm, sw), lambda i: (i, 0)))(proj, proj, gain, ws, b3)


def _mix_gate_body(yps_ref, q_ref, ga_ref, gb_ref, wp_ref, ws_ref, m_ref, part_ref):
    ya = _dot(yps_ref[...], wp_ref[...])
    yb = _dot(q_ref[...], ws_ref[...])
    sa = _sigmoid(ga_ref[...].astype(F32))
    sb = _sigmoid(gb_ref[...].astype(F32))
    m_ref[...] = (sa * ya + sb * yb).astype(m_ref.dtype)
    part_ref[0] = sa.astype(part_ref.dtype)
    part_ref[1] = sb.astype(part_ref.dtype)
    part_ref[2] = (ya * (sa * (1.0 - sa))).astype(part_ref.dtype)
    part_ref[3] = (yb * (sb * (1.0 - sb))).astype(part_ref.dtype)


def _mix_gate(yps, q, proj, gate_off, wpo, wso):
    t, pw = yps.shape
    sw = q.shape[1]
    n = wpo.shape[2]
    d = N_DEV * n
    tm = _row_tile(t)
    ga0 = gate_off // n
    gb0 = (gate_off + d) // n
    blk = pl.BlockSpec((tm, n), lambda j, i: (i, j))
    return _call(
        functools.partial(_mix_gate_body), name="mix_gate",
        out_shape=(_sds((t, d), BF16), _sds((4, t, d), BF16)), grid=(N_DEV, t // tm),
        in_specs=[pl.BlockSpec((tm, pw), lambda j, i: (i, 0)),
                  pl.BlockSpec((tm, sw), lambda j, i: (i, 0)),
                  pl.BlockSpec((tm, n), lambda j, i: (i, ga0 + j)),
                  pl.BlockSpec((tm, n), lambda j, i: (i, gb0 + j)),
                  pl.BlockSpec((None, pw, n), lambda j, i: (j, 0, 0)),
                  pl.BlockSpec((None, sw, n), lambda j, i: (j, 0, 0))],
        out_specs=(blk, pl.BlockSpec((4, tm, n), lambda j, i: (0, i, j))))(yps, q, proj, proj, wpo, wso)


def _loss_body(y_ref, t_ref, dy_ref, l_ref, *, inv_d):
    i = pl.program_id(0)

    @pl.when(i == 0)
    def _():
        l_ref[...] = jnp.zeros_like(l_ref)

    e = y_ref[...] - t_ref[...]
    dy_ref[...] = e * inv_d
    l_ref[...] += jnp.sum(e * e)


def _loss_grad(y, target):
    t, d = y.shape
    tm = _row_tile(t)
    blk = pl.BlockSpec((tm, d), lambda i: (i, 0))
    return _call(
        functools.partial(_loss_body, inv_d=1.0 / d), name="loss_grad",
        out_shape=(_sds((t, d), F32), _sds((8, 128), F32)), grid=(t // tm,),
        in_specs=[blk, blk],
        out_specs=(blk, pl.BlockSpec((8, 128), lambda i: (0, 0))))(y, target)


def _norm_bwd_body(dy_ref, f_ref, g_ref, df_ref, dg_ref, *, scale):
    i = pl.program_id(0)

    @pl.when(i == 0)
    def _():
        dg_ref[...] = jnp.zeros_like(dg_ref)

    f = f_ref[...].astype(F32)
    rstd = _rstd(f)
    fn = f * rstd
    dr = scale * dy_ref[...]
    dg_ref[...] += jnp.sum(dr * fn, axis=0, keepdims=True)
    dfn = dr * g_ref[...]
    df = rstd * (dfn - fn * jnp.mean(dfn * fn, axis=-1, keepdims=True))
    df_ref[...] = df.astype(df_ref.dtype)


def _norm_res_bwd(dy, f, g, scale, after=()):
    t, d = dy.shape
    tm = _row_tile(t)
    blk = pl.BlockSpec((tm, d), lambda i: (i, 0))
    vec = pl.BlockSpec((1, d), lambda i: (0, 0))
    return _call(
        functools.partial(_norm_bwd_body, scale=scale), name="norm_res_bwd",
        out_shape=(_sds((t, d), BF16), _sds((1, d), F32)), grid=(t // tm,),
        in_specs=[blk, blk, vec], out_specs=(blk, vec), after=after)(dy, f, g)


def _dgrad_ffn_body(df_ref, w_ref, gu_ref, dz_ref):
    da = _dot_nt(df_ref[...], w_ref[...])
    dz_ref[0] = (da * gu_ref[0].astype(F32)).astype(dz_ref.dtype)
    dz_ref[1] = (da * gu_ref[1].astype(F32)).astype(dz_ref.dtype)


def _dgrad_ffn(df, wd, gu):
    t, d = df.shape
    f = wd.shape[0]
    tm = _row_tile(t)
    tn = f // 4
    blk = pl.BlockSpec((2, tm, tn), lambda j, i: (0, i, j))
    return _call(
        functools.partial(_dgrad_ffn_body), name="dgrad_ffn", out_shape=_sds((2, t, f), BF16),
        grid=(f // tn, t // tm),
        in_specs=[pl.BlockSpec((tm, d), lambda j, i: (i, 0)),
                  pl.BlockSpec((tn, d), lambda j, i: (j, 0)), blk],
        out_specs=blk)(df, wd, gu)


def _dgrad_gate_body(do_ref, w_ref, part_ref, dya_ref, dyb_ref, dga_ref, dgb_ref):
    dm = _dot_nt(do_ref[...], w_ref[...])
    for k, out in enumerate((dya_ref, dyb_ref, dga_ref, dgb_ref)):
        out[...] = (dm * part_ref[k].astype(F32)).astype(out.dtype)


def _dgrad_gate(do, wout, parts):
    t, d = do.shape
    tm = _row_tile(t)
    tn = 512 if d % 512 == 0 else 256
    blk = pl.BlockSpec((tm, tn), lambda j, i: (i, j))
    out = _sds((t, d), BF16)
    return _call(
        functools.partial(_dgrad_gate_body), name="dgrad_gate", out_shape=(out, out, out, out),
        grid=(d // tn, t // tm),
        in_specs=[pl.BlockSpec((tm, d), lambda j, i: (i, 0)),
                  pl.BlockSpec((tn, d), lambda j, i: (j, 0)),
                  pl.BlockSpec((4, tm, tn), lambda j, i: (0, i, j))],
        out_specs=(blk, blk, blk, blk))(do, wout, parts)


def _wgrad_rows_body(a_ref, b_ref, o_ref, acc_ref, *, nt):
    t = pl.program_id(1)

    @pl.when(t == 0)
    def _():
        acc_ref[...] = jnp.zeros_like(acc_ref)

    acc_ref[...] += _dot_tn(a_ref[...], b_ref[...])

    @pl.when(t == nt - 1)
    def _():
        o_ref[...] = acc_ref[...].astype(o_ref.dtype)


def _wgrad_rows(a, b, after=()):
    t, k = a.shape
    n = b.shape[1]
    tt = _big_row_tile(t)
    tk = k // 4
    nt = t // tt
    return _call(
        functools.partial(_wgrad_rows_body, nt=nt), name="wgrad_rows", out_shape=_sds((k, n), BF16),
        grid=(k // tk, nt),
        in_specs=[pl.BlockSpec((tt, tk), lambda r, s: (s, r)),
                  pl.BlockSpec((tt, n), lambda r, s: (s, 0))],
        out_specs=pl.BlockSpec((tk, n), lambda r, s: (r, 0)),
        scratch_shapes=[pltpu.VMEM((tk, n), F32)], after=after)(a, b)


def _wgrad_cs(a, b3):
    t, k = a.shape
    parts = b3.shape[0]
    per = N_DEV // parts
    n = b3.shape[2] // per
    tt = _big_row_tile(t)
    nt = t // tt
    return _call(
        functools.partial(_wgrad_rows_body, nt=nt), name="wgrad_cs", out_shape=_sds((N_DEV, k, n), BF16),
        grid=(N_DEV, nt),
        in_specs=[pl.BlockSpec((tt, k), lambda r, s: (s, 0)),
                  pl.BlockSpec((None, tt, n), lambda r, s: (r // per, s, r % per))],
        out_specs=pl.BlockSpec((None, k, n), lambda r, s: (r, 0, 0)),
        scratch_shapes=[pltpu.VMEM((k, n), F32)])(a, b3)


def _dgrad_cs_body(dz_ref, w_ref, o_ref, *, n):
    acc = None
    for s in range(N_DEV):
        term = _dot_nt(dz_ref[:, s * n:(s + 1) * n], w_ref[s])
        acc = term if acc is None else acc + term
    o_ref[...] = acc.astype(o_ref.dtype)


def _dgrad_cs(dz, wg, after=()):
    t, d = dz.shape
    k, n = wg.shape[1], wg.shape[2]
    tm = _row_tile(t)
    return _call(
        functools.partial(_dgrad_cs_body, n=n), name="dgrad_cs", out_shape=_sds((t, k), BF16),
        grid=(t // tm,),
        in_specs=[pl.BlockSpec((tm, d), lambda i: (i, 0)),
                  pl.BlockSpec((N_DEV, k, n), lambda i: (0, 0, 0))],
        out_specs=pl.BlockSpec((tm, k), lambda i: (i, 0)), after=after)(dz, wg)


def _dgrad_cs_norm_body(dz_ref, w_ref, x_hbm, g_ref, dy_hbm, o_ref, dg_ref, x_ref, dy_ref, sem, *, tm):
    i = pl.program_id(0)
    s = pl.program_id(1)
    tile = pl.ds(pl.multiple_of(i * tm, tm), tm)

    def epilogue_operands():
        return (pltpu.make_async_copy(x_hbm.at[tile], x_ref, sem.at[0]),
                pltpu.make_async_copy(dy_hbm.at[tile], dy_ref, sem.at[1]))

    @pl.when(jnp.logical_and(i == 0, s == 0))
    def _():
        dg_ref[...] = jnp.zeros_like(dg_ref)

    @pl.when(s == 0)
    def _():
        o_ref[...] = jnp.zeros_like(o_ref)
        for cp in epilogue_operands():
            cp.start()

    o_ref[...] += _dot_nt(dz_ref[...], w_ref[...])

    @pl.when(s == N_DEV - 1)
    def _():
        for cp in epilogue_operands():
            cp.wait()

        def finish(rows):
            dh = o_ref[rows, :]
            x = x_ref[rows, :]
            rstd = _rstd(x)
            xn = x * rstd
            dg_ref[...] += jnp.sum(dh * xn, axis=0, keepdims=True)
            dxn = dh * g_ref[...]
            o_ref[rows, :] = dy_ref[rows, :] + rstd * (dxn - xn * jnp.mean(dxn * xn, axis=-1, keepdims=True))

        _for_row_chunks(tm, finish)


def _dgrad_cs_norm(dz3, wg, x, g, dy, after=()):
    parts, t, _ = dz3.shape
    per = N_DEV // parts
    d, n = wg.shape[1], wg.shape[2]
    tm = _big_row_tile(t)
    blk = pl.BlockSpec((tm, d), lambda i, s: (i, 0), pipeline_mode=pl.Buffered(1))
    vec = pl.BlockSpec((1, d), lambda i, s: (0, 0))
    return _call(
        functools.partial(_dgrad_cs_norm_body, tm=tm), name="dgrad_cs_norm",
        out_shape=(_sds((t, d), F32), _sds((1, d), F32)), grid=(t // tm, N_DEV),
        in_specs=[pl.BlockSpec((None, tm, n), lambda i, s: (s // per, i, s % per)),
                  pl.BlockSpec((None, d, n), lambda i, s: (s, 0, 0)), ANY_SPEC, vec, ANY_SPEC],
        out_specs=(blk, vec),
        scratch_shapes=[pltpu.VMEM((tm, d), F32), pltpu.VMEM((tm, d), F32), pltpu.SemaphoreType.DMA((2,))],
        after=after)(dz3, wg, x, g, dy)


def _pool_bwd_body(p_ref, ph_ref, dy_ref, dyh_ref, w_ref, sc_ref, dp_ref, dw_ref, dsc_ref, *, tm, nt):
    g = pl.program_id(0)
    i = pl.program_id(1)

    @pl.when(i == 0)
    def _():
        dw_ref[...] = jnp.zeros_like(dw_ref)
        dsc_ref[...] = jnp.zeros_like(dsc_ref)

    w = w_ref[...]
    sc = sc_ref[...]
    d = _pool_delta(p_ref, ph_ref, g, i, tm).astype(BF16)
    dyps = dy_ref[...].astype(F32)
    dsc_ref[...] += jnp.sum(dyps * _dot(d, w), axis=0, keepdims=True)
    dyp = (dyps * sc).astype(BF16)
    dw_ref[...] += _dot_tn(d, dyp)
    dyp_after = (jnp.where(i == nt - 1, 0.0, dyh_ref[...].astype(F32)) * sc).astype(BF16)
    dd = jnp.concatenate([_dot_nt(dyp, w), _dot_nt(dyp_after, w)], axis=0)
    rows = tm + POOL_HALO
    t1 = i * tm + lax.broadcasted_iota(jnp.int32, (rows, 1), 0) + 1
    e = dd / jnp.minimum(t1, 2 << g).astype(F32)
    r2 = e + pltpu.roll(e, rows - 1, 0)
    r4 = r2 + pltpu.roll(r2, rows - 2, 0)
    r8 = r4 + pltpu.roll(r4, rows - 4, 0)
    r16 = r8 + pltpu.roll(r8, rows - 8, 0)
    r = jnp.where(g == 0, r2, jnp.where(g == 1, r4, jnp.where(g == 2, r8, r16)))
    dp_ref[...] = (r[:tm] - dd[:tm]).astype(dp_ref.dtype)


def _pool_bwd(proj, dyps, wgrp, scale):
    t = proj.shape[0]
    c = wgrp.shape[1]
    tm = _row_tile(t)
    nt = t // tm
    hb = tm // POOL_HALO
    last = t // POOL_HALO - 1
    main, halo = _pool_specs(t, c, tm)
    halo_after = pl.BlockSpec((POOL_HALO, c), lambda g, i: (jnp.minimum((i + 1) * hb, last), g))
    return _call(
        functools.partial(_pool_bwd_body, tm=tm, nt=nt), name="pool_bwd",
        out_shape=(_sds((t, N_POOL_GROUPS * c), BF16), _sds((N_POOL_GROUPS, c, c), F32),
                   _sds((1, N_POOL_GROUPS * c), F32)),
        grid=(N_POOL_GROUPS, nt),
        in_specs=[main, halo, main, halo_after, pl.BlockSpec((None, c, c), lambda g, i: (g, 0, 0)),
                  pl.BlockSpec((1, c), lambda g, i: (0, g))],
        out_specs=(main, pl.BlockSpec((None, c, c), lambda g, i: (g, 0, 0)),
                   pl.BlockSpec((1, c), lambda g, i: (0, g))))(proj, proj, dyps, dyps, wgrp, scale)


def _sgu_bwd_body(u_ref, v_ref, dq_ref, gain_ref, ws_ref, b_ref, duv_ref, dws_ref, db_ref, dgain_ref,
                  dvn_ref, *, tm, heads):
    i = pl.program_id(0)

    @pl.when(i == 0)
    def _():
        dws_ref[...] = jnp.zeros_like(dws_ref)
        db_ref[...] = jnp.zeros_like(db_ref)
        dgain_ref[...] = jnp.zeros_like(dgain_ref)

    sw = heads * SGU_HEAD
    u = u_ref[...].astype(F32)
    v = v_ref[...].astype(F32)
    ug = _gelu(u)
    vg = _gelu(v)
    rstd = _rstd(vg)
    vgn = vg * rstd
    gain = gain_ref[...]
    vn = (vgn * gain).astype(BF16)
    dq = dq_ref[...].astype(F32)
    for h in range(heads):
        wt = _tril_weights(ws_ref, h).astype(BF16)
        cols = slice(h * SGU_HEAD, (h + 1) * SGU_HEAD)
        dws = jnp.zeros((SGU_CHUNK, SGU_CHUNK), F32)
        dbh = jnp.zeros((SGU_CHUNK, 1), F32)
        for c in range(tm // SGU_CHUNK):
            rows = slice(c * SGU_CHUNK, (c + 1) * SGU_CHUNK)
            vn_c = vn[rows, cols]
            sg = _dot(wt, vn_c) + b_ref[h]
            dq_c = dq[rows, cols]
            dsg = dq_c * ug[rows, cols]
            duv_ref[rows, cols] = (dq_c * sg * _gelu_grad(u[rows, cols])).astype(duv_ref.dtype)
            dsg_b = dsg.astype(BF16)
            dws = dws + _dot_nt(dsg_b, vn_c)
            dbh = dbh + jnp.sum(dsg, axis=1, keepdims=True)
            dvn_ref[rows, cols] = _dot_tn(wt, dsg_b)
        dws_ref[h] += jnp.where(_tril_mask(), dws, 0.0)
        db_ref[h] += dbh
    dvn = dvn_ref[...]
    dgain_ref[...] += jnp.sum(dvn * vgn, axis=0, keepdims=True)
    dvgn = dvn * gain
    dvg = rstd * (dvgn - vgn * jnp.mean(dvgn * vgn, axis=-1, keepdims=True))
    duv_ref[:, sw:] = (dvg * _gelu_grad(v)).astype(duv_ref.dtype)


def _sgu_bwd(proj, pw, dq, gain, ws, b3):
    t = proj.shape[0]
    sw = gain.shape[1]
    heads = sw // SGU_HEAD
    tm = _row_tile(t)
    ub = pw // sw
    wblk = pl.BlockSpec((heads, SGU_CHUNK, SGU_CHUNK), lambda i: (0, 0, 0))
    bblk = pl.BlockSpec((heads, SGU_CHUNK, 1), lambda i: (0, 0, 0))
    vec = pl.BlockSpec((1, sw), lambda i: (0, 0))
    return _call(
        functools.partial(_sgu_bwd_body, tm=tm, heads=heads), name="sgu_bwd",
        out_shape=(_sds((t, 2 * sw), BF16), _sds((heads, SGU_CHUNK, SGU_CHUNK), F32),
                   _sds((heads, SGU_CHUNK, 1), F32), _sds((1, sw), F32)),
        grid=(t // tm,),
        in_specs=[pl.BlockSpec((tm, sw), lambda i: (i, ub)),
                  pl.BlockSpec((tm, sw), lambda i: (i, ub + 1)),
                  pl.BlockSpec((tm, sw), lambda i: (i, 0)), vec, wblk, bblk],
        out_specs=(pl.BlockSpec((tm, 2 * sw), lambda i: (i, 0)), wblk, bblk, vec),
        scratch_shapes=[pltpu.VMEM((tm, sw), F32)])(proj, proj, dq, gain, ws, b3)


def _cast_slot_body(me_ref, x_ref, o_ref):
    o_ref[...] = x_ref[...].astype(o_ref.dtype)


def _cast_to_slot(w, layer, me, dtype=BF16):
    _, r, c = w.shape
    tr = _slab_rows(r, 1024)
    spec = pltpu.PrefetchScalarGridSpec(
        num_scalar_prefetch=1, grid=(r // tr,),
        in_specs=[pl.BlockSpec((None, tr, c), lambda i, me_ref: (layer, i, 0))],
        out_specs=pl.BlockSpec((None, tr, c), lambda i, me_ref: (me_ref[0], i, 0)))
    return _call(functools.partial(_cast_slot_body), name="cast_to_slot", out_shape=_sds((N_DEV, r, c), dtype),
                 grid_spec=spec)(me, w)


def _mesh_position():
    x, y, c = (lax.axis_index(a) for a in MESH_AXES)
    return x, y, c


def _remote(src, dst, send, recv, device):
    return pltpu.make_async_remote_copy(src_ref=src, dst_ref=dst, send_sem=send, recv_sem=recv,
                                        device_id=device, device_id_type=pl.DeviceIdType.MESH)


def _other_chips(x, y):
    return [(1 - x, y), (x, 1 - y), (1 - x, 1 - y)]


def _gather_send_own(bufs, send, recv):
    x, y, c = _mesh_position()
    me = 4 * x + 2 * y + c
    targets = [(x, y, 1 - c)] + [(px, py, c) for px, py in _other_chips(x, y)]
    for k, buf in enumerate(bufs):
        for j, dev in enumerate(targets):
            _remote(buf.at[me], buf.at[me], send.at[4 * k + j], recv.at[4 * k + j], dev).start()


def _gather_await_own(bufs, send, recv):
    x, y, c = _mesh_position()
    me = 4 * x + 2 * y + c
    targets = [(x, y, 1 - c)] + [(px, py, c) for px, py in _other_chips(x, y)]
    for k, buf in enumerate(bufs):
        for j, dev in enumerate(targets):
            cp = _remote(buf.at[me], buf.at[4 * dev[0] + 2 * dev[1] + dev[2]], send.at[4 * k + j],
                         recv.at[4 * k + j], dev)
            cp.wait_recv()
            cp.wait_send()


def _gather_relay_send(bufs, send, recv):
    x, y, c = _mesh_position()
    for k, buf in enumerate(bufs):
        for j, (px, py) in enumerate(_other_chips(x, y)):
            blk = buf.at[4 * px + 2 * py + c]
            _remote(blk, blk, send.at[3 * k + j], recv.at[3 * k + j], (x, y, 1 - c)).start()


def _gather_relay_await(bufs, send, recv):
    x, y, c = _mesh_position()
    for k, buf in enumerate(bufs):
        for j, (px, py) in enumerate(_other_chips(x, y)):
            cp = _remote(buf.at[4 * px + 2 * py + c], buf.at[4 * px + 2 * py + 1 - c], send.at[3 * k + j],
                         recv.at[3 * k + j], (x, y, 1 - c))
            cp.wait_recv()
            cp.wait_send()


def _reduce_pair_send(grads, stage, send, recv):
    x, y, c = _mesh_position()
    for k, (g, s) in enumerate(zip(grads, stage)):
        for q in range(4):
            _remote(g.at[2 * q + 1 - c], s.at[q], send.at[4 * k + q], recv.at[4 * k + q], (x, y, 1 - c)).start()


def _reduce_pair_await(grads, stage, send, recv):
    x, y, c = _mesh_position()
    for k, (g, s) in enumerate(zip(grads, stage)):
        for q in range(4):
            cp = _remote(g.at[2 * q + 1 - c], s.at[q], send.at[4 * k + q], recv.at[4 * k + q], (x, y, 1 - c))
            cp.wait_recv()
            cp.wait_send()


def _reduce_chip_send(sums, stage, send, recv):
    x, y, c = _mesh_position()
    for k, (p, s) in enumerate(zip(sums, stage)):
        for j, (px, py) in enumerate(_other_chips(x, y)):
            _remote(p.at[2 * px + py], s.at[2 * x + y], send.at[3 * k + j], recv.at[3 * k + j],
                    (px, py, c)).start()


def _reduce_chip_await(sums, stage, send, recv):
    x, y, c = _mesh_position()
    for k, (p, s) in enumerate(zip(sums, stage)):
        for j, (px, py) in enumerate(_other_chips(x, y)):
            cp = _remote(p.at[2 * px + py], s.at[2 * px + py], send.at[3 * k + j], recv.at[3 * k + j],
                         (px, py, c))
            cp.wait_recv()
            cp.wait_send()


def _split_body(*refs, n, n_sem_in, n_after, n_sem_out, steps, token):
    bufs = refs[:n]
    sems = refs[n:n + n_sem_in] + refs[n + n_sem_in + n_after:n + n_sem_in + n_after + n_sem_out]
    for step, lo in steps:
        step(bufs, sems[lo], sems[lo + 1])
    if token:
        refs[-1][...] = jnp.zeros_like(refs[-1])


def _split_call(steps, *, name, bufs, sems_in=(), sems_out=(), after=(), token=True):
    n = len(bufs)
    operands = [pltpu.with_memory_space_constraint(b, pltpu.HBM) for b in bufs] + list(sems_in) + list(after)
    in_specs = [HBM_SPEC] * n + [SEM_SPEC] * len(sems_in) + [ANY_SPEC] * len(after)
    out_shape = list(sems_out) + [pltpu.HBM(b.shape, b.dtype) for b in bufs]
    out_specs = [SEM_SPEC] * len(sems_out) + [HBM_SPEC] * n
    if token:
        out_shape.append(_sds((8, 128), F32))
        out_specs.append(pl.BlockSpec(memory_space=pltpu.VMEM))
    body = functools.partial(_split_body, n=n, n_sem_in=len(sems_in), n_after=len(after),
                             n_sem_out=len(sems_out), steps=tuple(steps), token=token)
    return pl.pallas_call(
        body, name=name, in_specs=in_specs, out_specs=tuple(out_specs), out_shape=tuple(out_shape),
        input_output_aliases={i: len(sems_out) + i for i in range(n)},
        compiler_params=pltpu.CompilerParams(
            has_side_effects=pltpu.SideEffectType.DATAFLOW_SIDE_EFFECTING))(*operands)


def _sem_pair(n, m):
    return [pltpu.SemaphoreType.DMA((n * m,)), pltpu.SemaphoreType.DMA((n * m,))]


def _gather_start(bufs, after, tag):
    n = len(bufs)
    out = _split_call([(_gather_send_own, 0)], name="gather_start_" + tag, bufs=bufs,
                      sems_out=_sem_pair(n, 4), after=after)
    return out[0], out[1], list(out[2:2 + n]), out[-1]


def _gather_relay(state, after, tag):
    send, recv, bufs, _ = state
    n = len(bufs)
    out = _split_call([(_gather_await_own, 0), (_gather_relay_send, 2)], name="gather_relay_" + tag, bufs=bufs,
                      sems_in=[send, recv], sems_out=_sem_pair(n, 3), after=after)
    return out[0], out[1], list(out[2:2 + n]), out[-1]


def _gather_finish(state, after, tag):
    send, recv, bufs, _ = state
    out = _split_call([(_gather_relay_await, 0)], name="gather_finish_" + tag, bufs=bufs,
                      sems_in=[send, recv], after=after, token=False)
    return list(out)


def _pair_arrays(n):
    def split(step):
        return lambda bufs, send, recv: step(bufs[:n], bufs[n:], send, recv)
    return split


def _reduce_pairs_start(grads, after, tag):
    n = len(grads)
    stage = [lax.empty((4,) + g.shape[1:], g.dtype) for g in grads]
    out = _split_call([(_pair_arrays(n)(_reduce_pair_send), 0)], name="reduce_pairs_start_" + tag,
                      bufs=list(grads) + stage, sems_out=_sem_pair(n, 4), after=after)
    return out[0], out[1], list(out[2:2 + n]), list(out[2 + n:2 + 2 * n]), out[-1]


def _reduce_pairs_finish(state, after, tag):
    send, recv, grads, stage, _ = state
    n = len(grads)
    out = _split_call([(_pair_arrays(n)(_reduce_pair_await), 0)], name="reduce_pairs_finish_" + tag,
                      bufs=list(grads) + list(stage), sems_in=[send, recv], after=after, token=False)
    return list(out[:n]), list(out[n:])


def _reduce_chips_start(sums, after, tag):
    n = len(sums)
    stage = [lax.empty(p.shape, p.dtype) for p in sums]
    out = _split_call([(_pair_arrays(n)(_reduce_chip_send), 0)], name="reduce_chips_start_" + tag,
                      bufs=list(sums) + stage, sems_out=_sem_pair(n, 3), after=after)
    return out[0], out[1], list(out[2:2 + n]), list(out[2 + n:2 + 2 * n]), out[-1]


def _reduce_chips_finish(state, after, tag):
    send, recv, sums, stage, _ = state
    n = len(sums)
    out = _split_call([(_pair_arrays(n)(_reduce_chip_await), 0)], name="reduce_chips_finish_" + tag,
                      bufs=list(sums) + list(stage), sems_in=[send, recv], after=after, token=False)
    return list(out[:n]), list(out[n:])


def _pair_sum_body(c_ref, a_ref, b_ref, o_ref):
    o_ref[...] = (a_ref[...].astype(F32) + b_ref[...].astype(F32)).astype(o_ref.dtype)


def _pair_sum(grad, stage, core):
    _, r, c = grad.shape
    tr = _slab_rows(r, 1024)
    blk = pl.BlockSpec((None, tr, c), lambda q, i, c_ref: (q, i, 0))
    spec = pltpu.PrefetchScalarGridSpec(
        num_scalar_prefetch=1, grid=(4, r // tr),
        in_specs=[pl.BlockSpec((None, tr, c), lambda q, i, c_ref: (2 * q + c_ref[0], i, 0)), blk],
        out_specs=blk)
    return _call(functools.partial(_pair_sum_body), name="pair_sum", out_shape=_sds((4, r, c), grad.dtype),
                 grid_spec=spec)(core, grad, stage)


def _adamw_math(w, g, m, v):
    m = ADAM_B1 * m + (1.0 - ADAM_B1) * g
    v = ADAM_B2 * v + (1.0 - ADAM_B2) * (g * g)
    m_hat = m / (1.0 - ADAM_B1 ** ADAM_STEP)
    v_hat = v / (1.0 - ADAM_B2 ** ADAM_STEP)
    delta = -ADAM_LR * (m_hat / (jnp.sqrt(v_hat) + ADAM_EPS) + ADAM_WD * w)
    return delta, m, v


def _adamw_body(q_ref, own_ref, r_ref, w_ref, m_ref, v_ref, *rest):
    g_out, d_out, m_out, v_out = rest[-4:]
    q = q_ref[0]
    g = None
    for j in range(4):
        term = jnp.where(q == j, own_ref[...], r_ref[j]).astype(F32)
        g = term if g is None else g + term
    delta, m, v = _adamw_math(w_ref[...], g, m_ref[...], v_ref[...])
    g_out[...] = g
    d_out[...] = delta
    m_out[...] = m
    v_out[...] = v


def _adamw_layer(sums, recv, w, m, v, prev, layer, chip):
    l, r, c = w.shape
    tr = _slab_rows(r)
    lay = pl.BlockSpec((None, tr, c), lambda i, q_ref: (layer, i, 0))
    spec = pltpu.PrefetchScalarGridSpec(
        num_scalar_prefetch=1, grid=(r // tr,),
        in_specs=[pl.BlockSpec((None, tr, c), lambda i, q_ref: (q_ref[0], i, 0)),
                  pl.BlockSpec((4, tr, c), lambda i, q_ref: (0, i, 0)), lay, lay, lay] + [ANY_SPEC] * 4,
        out_specs=(lay, lay, lay, lay))
    out = _sds((l, r, c), F32)
    return _call(
        functools.partial(_adamw_body), name="adamw", out_shape=(out, out, out, out), grid_spec=spec,
        aliases={6: 0, 7: 1, 8: 2, 9: 3})(chip, sums, recv, w, m, v, *prev)


def _adamw_small_body(r_ref, w_ref, m_ref, v_ref, g_out, d_out, m_out, v_out):
    g = r_ref[0]
    for j in range(1, N_DEV):
        g = g + r_ref[j]
    delta, m, v = _adamw_math(w_ref[...], g, m_ref[...], v_ref[...])
    g_out[...] = g
    d_out[...] = delta
    m_out[...] = m
    v_out[...] = v


def _adamw_small(recv, w, m, v):
    r, c = w.shape
    tr = 8
    for cand in (1024, 512, 256, 128, 64, 32, 16, 8):
        if r % cand == 0:
            tr = cand
            break
    blk = pl.BlockSpec((tr, c), lambda i: (i, 0))
    out = _sds((r, c), F32)
    return _call(
        functools.partial(_adamw_small_body), name="adamw_small", out_shape=(out, out, out, out),
        grid=(r // tr,),
        in_specs=[pl.BlockSpec((N_DEV, tr, c), lambda i: (0, i, 0)), blk, blk, blk],
        out_specs=(blk, blk, blk, blk))(recv, w, m, v)


def _as3d(a):
    return a.reshape(a.shape[0], -1, a.shape[-1])


GROUPS = (("w_ffn1_up", "w_ffn1_down"),
          ("w_in", "pool_group_w", "w_pool_out", "w_sgu_out", "w_out"),
          ("w_ffn2_up", "w_ffn2_down"))


def _ffn_forward(x, g_pre, wu, wd, g_post, after=(), mid=None):
    h = _rms_fwd(x, g_pre, after=after)
    gu, a = _ffn_up(h, wu)
    tokens = tuple(mid(a)) if mid else ()
    wd = wd() if callable(wd) else wd
    x_out, f = _down_norm_res(a, wd, x, g_post, MACARON_WEIGHT, after=tokens)
    return x_out, (wd, (x, h, gu, a, f))


def _ffn_backward(dy, saved, g_pre, wu, wd, g_post, mid=None, early=None):
    x, h, gu, a, f = saved
    df, dg_post = _norm_res_bwd(dy, f, g_post, MACARON_WEIGHT)
    dz = _dgrad_ffn(df, wd, gu)
    tokens = tuple(mid(dz)) if mid else ()
    dwd = _wgrad_rows(a, df).reshape(N_DEV, -1, df.shape[1])
    dwu = _wgrad_cs(h, dz)
    tokens += tuple(early([dwu, dwd])) if early else (dwu, dwd)
    dy, dg_pre = _dgrad_cs_norm(dz, wu, x, g_pre, dy, after=tokens)
    return dy, dg_pre, dg_post


def kernel(x, g_ffn1_pre, w_ffn1_up, w_ffn1_down, g_ffn1_post, g_mix_pre, w_in, pool_group_w, pool_scale, w_pool_out, sgu_v_gain, sgu_w_s, sgu_b_s, w_sgu_out, w_out, g_mix_post, g_ffn2_pre, w_ffn2_up, w_ffn2_down, g_ffn2_post, loss_target, m_g_ffn1_pre, m_w_ffn1_up, m_w_ffn1_down, m_g_ffn1_post, m_g_mix_pre, m_w_in, m_pool_group_w, m_pool_scale, m_w_pool_out, m_sgu_v_gain, m_sgu_w_s, m_sgu_b_s, m_w_sgu_out, m_w_out, m_g_mix_post, m_g_ffn2_pre, m_w_ffn2_up, m_w_ffn2_down, m_g_ffn2_post, v_g_ffn1_pre, v_w_ffn1_up, v_w_ffn1_down, v_g_ffn1_post, v_g_mix_pre, v_w_in, v_pool_group_w, v_pool_scale, v_w_pool_out, v_sgu_v_gain, v_sgu_w_s, v_sgu_b_s, v_w_sgu_out, v_w_out, v_g_mix_post, v_g_ffn2_pre, v_w_ffn2_up, v_w_ffn2_down, v_g_ffn2_post):
    given = dict(locals())
    weights = {n: given[n] for n in ORDER}
    mom_m = {n: given["m_" + n] for n in ORDER}
    mom_v = {n: given["v_" + n] for n in ORDER}
    depth = g_ffn1_pre.shape[0]
    d_model = x.shape[-1]
    pool_w = pool_scale.shape[1]
    sgu_w = sgu_v_gain.shape[1]
    groups, grp_rows, grp_c = pool_group_w.shape[1:]
    gate_off = pool_w + 2 * sgu_w

    xs = x.reshape(x.shape[-2], d_model)
    target = loss_target.reshape(xs.shape)

    px, py, pc = _mesh_position()
    me = (4 * px + 2 * py + pc).astype(jnp.int32).reshape(1)
    core = pc.astype(jnp.int32).reshape(1)
    chip = (2 * px + py).astype(jnp.int32).reshape(1)

    n_groups = len(GROUPS) * depth
    ffn_gains = {0: ("g_ffn1_pre", "g_ffn1_post"), 2: ("g_ffn2_pre", "g_ffn2_post")}

    def start_gather(t, after):
        layer, gi = divmod(t, len(GROUPS))
        bufs = [_cast_to_slot(_as3d(weights[n]), layer, me) for n in GROUPS[gi]]
        return _gather_start(bufs, after, "g%d" % t)

    def vec(name, i):
        return weights[name][i].reshape(1, -1)

    def forward_group(t, cur, bufs, after, mid):
        layer, gi = divmod(t, len(GROUPS))
        gw = dict(zip(GROUPS[gi], bufs))
        if gi != 1:
            pre, post = ffn_gains[gi]
            wu, wd = (gw[n] for n in GROUPS[gi])
            wd = wd if callable(wd) else wd.reshape(-1, d_model)
            cur, (wd, sv) = _ffn_forward(cur, vec(pre, layer), wu, wd, vec(post, layer), after=after, mid=mid)
            return cur, ((wu, wd), sv)
        wgrp = gw["pool_group_w"].reshape(N_DEV, groups, grp_rows, grp_c).transpose(1, 0, 2, 3)
        wts = dict(win=gw["w_in"], wgrp=wgrp.reshape(groups, grp_c, grp_c), wpo=gw["w_pool_out"],
                   wso=gw["w_sgu_out"], wout=gw["w_out"].reshape(-1, d_model),
                   ws=weights["sgu_w_s"][layer], b3=weights["sgu_b_s"][layer][:, :, None])
        x1 = cur
        h2 = _rms_fwd(x1, vec("g_mix_pre", layer), after=after)
        proj = _mm_cs(h2, wts["win"])
        tokens = tuple(mid(proj))
        yps = _pool_fwd(proj, wts["wgrp"], vec("pool_scale", layer))
        q = _sgu_fwd(proj, pool_w, vec("sgu_v_gain", layer), wts["ws"], wts["b3"])
        m, parts = _mix_gate(yps, q, proj, gate_off, wts["wpo"], wts["wso"])
        cur, o = _down_norm_res(m, wts["wout"], x1, vec("g_mix_post", layer), 1.0, after=tokens)
        return cur, (wts, (x1, h2, proj, yps, q, parts, m, o))

    first = [_gather_start([_cast_to_slot(_as3d(weights[n]), 0, me)], (), "g0" + n[-2:]) for n in GROUPS[0]]
    states = {1: start_gather(1, (first[0][-1], first[1][-1]))}
    up0 = _gather_finish(_gather_relay(first[0], (states[1][-1],), "g0up"), (), "g0up")
    down0 = []
    bufs = [up0[0], lambda: down0[0][0].reshape(-1, d_model)]
    saved = []
    cur = xs
    for t in range(n_groups):
        after = []
        if t + 2 < n_groups:
            states[t + 2] = start_gather(t + 2, (bufs[0],))
            after.append(states[t + 2][-1])
        relayed = []

        def mid(result, t=t, relayed=relayed):
            if t == 0:
                down0.append(_gather_finish(_gather_relay(first[1], (result,), "g0wn"), (), "g0wn"))
            if t + 1 < n_groups:
                relayed.append(_gather_relay(states[t + 1], (result,), "g%d" % (t + 1)))
            return tuple(r[-1] for r in relayed)

        cur, sv = forward_group(t, cur, bufs, tuple(after), mid)
        saved.append(sv)
        if t + 1 < n_groups:
            bufs = _gather_finish(relayed[0], (cur,), "g%d" % (t + 1))

    dy, sq = _loss_grad(cur, target)
    loss = lax.psum(0.5 * sq[0, 0] / d_model, MESH_AXES)

    small_grads = {n: [None] * depth for n in SMALL}
    outs = {n: None for n in BIG}

    def chip_sums(pairs, after, tag):
        grads, stage = _reduce_pairs_finish(pairs, after, tag)
        sums = [_pair_sum(g, s, core) for g, s in zip(grads, stage)]
        return _reduce_chips_start(sums, (), tag)

    def update(chips, after, t):
        layer, gi = divmod(t, len(GROUPS))
        sums, recv = _reduce_chips_finish(chips, after, "g%d" % t)
        for n, p, r in zip(GROUPS[gi], sums, recv):
            w3, m3, v3 = _as3d(weights[n]), _as3d(mom_m[n]), _as3d(mom_v[n])
            prev = outs[n] if outs[n] is not None else [lax.empty(w3.shape, F32) for _ in range(4)]
            outs[n] = _adamw_layer(p, r, w3, m3, v3, prev, layer, chip)

    def backward_group(t, dy, sv, mid, early):
        layer, gi = divmod(t, len(GROUPS))
        wts, sv = sv
        if gi != 1:
            pre, post = ffn_gains[gi]
            dy, dg, dgp = _ffn_backward(dy, sv, vec(pre, layer), wts[0], wts[1], vec(post, layer),
                                        mid=mid, early=early)
            small_grads[pre][layer], small_grads[post][layer] = dg, dgp
            return dy
        x1, h2, proj, yps, q, parts, m, o = sv
        do, small_grads["g_mix_post"][layer] = _norm_res_bwd(dy, o, vec("g_mix_post", layer), 1.0)
        dya, dyb, dga, dgb = _dgrad_gate(do, wts["wout"], parts)
        tokens = tuple(mid(dya)) if mid else ()
        dwout = _wgrad_rows(m, do).reshape(N_DEV, -1, d_model)
        dwpo = _wgrad_cs(yps, dya[None])
        dwso = _wgrad_cs(q, dyb[None])
        dyps = _dgrad_cs(dya, wts["wpo"], after=tokens)
        dq = _dgrad_cs(dyb, wts["wso"])
        dp, dwgrp, small_grads["pool_scale"][layer] = _pool_bwd(proj, dyps, wts["wgrp"], vec("pool_scale", layer))
        duv, dws, db3, small_grads["sgu_v_gain"][layer] = _sgu_bwd(
            proj, pool_w, dq, vec("sgu_v_gain", layer), wts["ws"], wts["b3"])
        small_grads["sgu_w_s"][layer] = dws
        small_grads["sgu_b_s"][layer] = db3
        dproj = jnp.concatenate([dp, duv, dga, dgb], axis=1)[None]
        dwin = _wgrad_cs(h2, dproj)
        dwgrp8 = dwgrp.reshape(groups, N_DEV, grp_rows, grp_c).transpose(1, 0, 2, 3)
        dwgrp8 = dwgrp8.reshape(N_DEV, groups * grp_rows, grp_c).astype(BF16)
        dy, small_grads["g_mix_pre"][layer] = _dgrad_cs_norm(
            dproj, wts["win"], x1, vec("g_mix_pre", layer), dy,
            after=tuple(early([dwin, dwgrp8, dwpo, dwso, dwout])))
        return dy

    pairs = None
    for t in reversed(range(n_groups)):
        started, mine = [], []

        def mid(first, t=t, started=started):
            started.append(chip_sums(pairs, (first,), "g%d" % (t + 1)))
            return (started[0][-1],)

        def early(grads, t=t, mine=mine):
            mine.append(_reduce_pairs_start(grads, (), "g%d" % t))
            return (mine[0][-1],)

        dy = backward_group(t, dy, saved[t], mid if pairs else None, early)
        if started:
            update(started[0], (dy,), t + 1)
        pairs = mine[0]
    last = chip_sums(pairs, (), "g0")

    def pack(parts):
        return jnp.concatenate([p.reshape(depth, -1) for p in parts], axis=1).reshape(-1, 128)

    part = pack([jnp.stack(small_grads[n]) for n in SMALL])
    small = _gather_start([_cast_to_slot(part[None], 0, me, F32)], (last[-1],), "small")
    done = tuple(outs[n][0] for n in BIG if outs[n] is not None)
    all_parts = _gather_finish(_gather_relay(small, done, "small"), (), "small")[0]
    small_out = _adamw_small(all_parts, pack([weights[n] for n in SMALL]), pack([mom_m[n] for n in SMALL]),
                             pack([mom_v[n] for n in SMALL]))
    update(last, (small_out[0],), 0)

    results = {}
    sizes = [weights[n][0].size for n in SMALL]
    for kind, packed in zip(("grad", "delta", "new_m", "new_v"), small_out):
        flat = packed.reshape(depth, -1)
        off = 0
        for n, sz in zip(SMALL, sizes):
            results[(kind, n)] = flat[:, off:off + sz].reshape(weights[n].shape)
            off += sz
    for n in BIG:
        for kind, arr in zip(("grad", "delta", "new_m", "new_v"), outs[n]):
            results[(kind, n)] = arr.reshape(weights[n].shape)

    grad_x = dy.reshape(x.shape)
    return (loss, grad_x, *[results[("grad", n)] for n in ORDER], *[results[("delta", n)] for n in ORDER],
            *[results[("new_m", n)] for n in ORDER], *[results[("new_v", n)] for n in ORDER])
```

```python
import functools

import jax
import jax.numpy as jnp
from jax import lax
from jax.experimental import pallas as pl
from jax.experimental.pallas import tpu as pltpu

F32 = jnp.float32
BF16 = jnp.bfloat16

N_DEV = 8
MESH_AXES = ("x", "y", "c")

EPS = 1e-6
MACARON_WEIGHT = 0.5
N_POOL_GROUPS = 4
POOL_HALO = 16
SGU_HEAD = 128
SGU_CHUNK = 128
GELU_C = 0.7978845608028654
GELU_A = 0.044715

ADAM_LR = 0.001
ADAM_B1 = 0.9
ADAM_B2 = 0.999
ADAM_EPS = 1e-08
ADAM_WD = 0.01
ADAM_STEP = 10

ROW_TILE = 512
VMEM_LIMIT_BYTES = 56 * 1024 * 1024

BIG = ("w_ffn1_up", "w_ffn1_down", "w_in", "pool_group_w", "w_pool_out", "w_sgu_out", "w_out",
       "w_ffn2_up", "w_ffn2_down")
SMALL = ("g_ffn1_pre", "g_ffn1_post", "g_mix_pre", "pool_scale", "sgu_v_gain", "sgu_w_s", "sgu_b_s",
         "g_mix_post", "g_ffn2_pre", "g_ffn2_post")
ORDER = ("g_ffn1_pre", "w_ffn1_up", "w_ffn1_down", "g_ffn1_post", "g_mix_pre", "w_in", "pool_group_w",
         "pool_scale", "w_pool_out", "sgu_v_gain", "sgu_w_s", "sgu_b_s", "w_sgu_out", "w_out",
         "g_mix_post", "g_ffn2_pre", "w_ffn2_up", "w_ffn2_down", "g_ffn2_post")


ANY_SPEC = pl.BlockSpec(memory_space=pl.ANY)
HBM_SPEC = pl.BlockSpec(memory_space=pltpu.HBM)
SEM_SPEC = pl.BlockSpec(memory_space=pltpu.SEMAPHORE)


def _skip_refs(body, n_in, n_skip, *refs):
    return body(*refs[:n_in], *refs[n_in + n_skip:])


def _call(body, *, name, out_shape, grid=(), in_specs=None, out_specs=None, scratch_shapes=(),
          aliases=None, after=(), grid_spec=None):
    extra = {}
    if grid_spec is not None:
        extra["grid_spec"] = grid_spec
    else:
        extra.update(in_specs=list(in_specs) + [ANY_SPEC] * len(after), out_specs=out_specs,
                     scratch_shapes=list(scratch_shapes))
        if grid:
            extra["grid"] = grid
        if after:
            body = functools.partial(_skip_refs, body, len(in_specs), len(after))
    if aliases:
        extra["input_output_aliases"] = aliases
    call = pl.pallas_call(
        body, name=name, out_shape=out_shape,
        compiler_params=pltpu.CompilerParams(vmem_limit_bytes=VMEM_LIMIT_BYTES), **extra)
    if after:
        return lambda *args: call(*args, *after)
    return call


def _sds(shape, dtype):
    return jax.ShapeDtypeStruct(tuple(shape), dtype)


def _dot(a, b):
    return jnp.dot(a, b, preferred_element_type=F32)


def _dot_nt(a, b):
    return lax.dot_general(a, b, (((1,), (1,)), ((), ())), preferred_element_type=F32)


def _dot_tn(a, b):
    return lax.dot_general(a, b, (((0,), (0,)), ((), ())), preferred_element_type=F32)


def _sigmoid(x):
    return 1.0 / (1.0 + jnp.exp(-x))


def _gelu(x):
    return 0.5 * x * (1.0 + jnp.tanh(GELU_C * (x + GELU_A * x * x * x)))


def _gelu_grad(x):
    t = jnp.tanh(GELU_C * (x + GELU_A * x * x * x))
    return 0.5 * (1.0 + t) + 0.5 * x * (1.0 - t * t) * (GELU_C * (1.0 + 3.0 * GELU_A * x * x))


def _rstd(x):
    return lax.rsqrt(jnp.mean(x * x, axis=-1, keepdims=True) + EPS)


def _row_tile(t):
    return min(t, ROW_TILE)


def _big_row_tile(t):
    return 2 * ROW_TILE if t % (2 * ROW_TILE) == 0 else _row_tile(t)


def _slab_rows(r, most=256):
    for cand in range(min(most, r) // 16 * 16, 15, -16):
        if r % cand == 0:
            return cand
    return r


def _rms_fwd_body(x_ref, g_ref, h_ref):
    x = x_ref[...]
    h_ref[...] = ((x * _rstd(x)) * g_ref[...]).astype(h_ref.dtype)


def _rms_fwd(x, g, after=()):
    t, d = x.shape
    tm = _row_tile(t)
    return _call(
        functools.partial(_rms_fwd_body), name="rms_fwd", out_shape=_sds((t, d), BF16),
        grid=(t // tm,),
        in_specs=[pl.BlockSpec((tm, d), lambda i: (i, 0)), pl.BlockSpec((1, d), lambda i: (0, 0))],
        out_specs=pl.BlockSpec((tm, d), lambda i: (i, 0)), after=after)(x, g)


def _mm_cs_body(h_ref, w_ref, o_ref):
    o_ref[...] = _dot(h_ref[...], w_ref[...]).astype(o_ref.dtype)


def _mm_cs(h, wg, after=()):
    t, k = h.shape
    n = wg.shape[2]
    tm = _row_tile(t)
    return _call(
        functools.partial(_mm_cs_body), name="mm_cs", out_shape=_sds((t, N_DEV * n), BF16),
        grid=(N_DEV, t // tm),
        in_specs=[pl.BlockSpec((tm, k), lambda j, i: (i, 0)),
                  pl.BlockSpec((None, k, n), lambda j, i: (j, 0, 0))],
        out_specs=pl.BlockSpec((tm, n), lambda j, i: (i, j)), after=after)(h, wg)


def _ffn_up_body(h_ref, wg_ref, wu_ref, gu_ref, a_ref):
    h = h_ref[...]
    g = _dot(h, wg_ref[...])
    u = _dot(h, wu_ref[...])
    sig = _sigmoid(g)
    silu = g * sig
    gu_ref[0] = (u * (sig * (1.0 + g * (1.0 - sig)))).astype(gu_ref.dtype)
    gu_ref[1] = silu.astype(gu_ref.dtype)
    a_ref[...] = (silu * u).astype(a_ref.dtype)


def _ffn_up(h, wg, after=()):
    t, k = h.shape
    n = wg.shape[2]
    half = N_DEV // 2
    f = half * n
    tm = _row_tile(t)
    return _call(
        functools.partial(_ffn_up_body), name="ffn_up",
        out_shape=(_sds((2, t, f), BF16), _sds((t, f), BF16)),
        grid=(half, t // tm),
        in_specs=[pl.BlockSpec((tm, k), lambda j, i: (i, 0)),
                  pl.BlockSpec((None, k, n), lambda j, i: (j, 0, 0)),
                  pl.BlockSpec((None, k, n), lambda j, i: (j + N_DEV // 2, 0, 0))],
        out_specs=(pl.BlockSpec((2, tm, n), lambda j, i: (0, i, j)),
                   pl.BlockSpec((tm, n), lambda j, i: (i, j))), after=after)(h, wg, wg)


EPILOGUE_ROWS = 256


def _for_row_chunks(rows, step):
    chunk = min(rows, EPILOGUE_ROWS)

    def body(r, carry):
        step(pl.ds(pl.multiple_of(r * chunk, chunk), chunk))
        return carry

    lax.fori_loop(0, rows // chunk, body, 0)


def _down_body(*refs, scale, nk, tm, with_next):
    a_ref, w_ref, x_ref, g_ref = refs[:4]
    xo_ref, f_ref = refs[4 + with_next:6 + with_next]
    k = pl.program_id(1)

    @pl.when(k == 0)
    def _():
        xo_ref[...] = jnp.zeros_like(xo_ref)

    xo_ref[...] += _dot(a_ref[...], w_ref[...])

    @pl.when(k == nk - 1)
    def _():
        def finish(rows):
            f = xo_ref[rows, :]
            f_ref[rows, :] = f.astype(f_ref.dtype)
            xo = x_ref[rows, :] + scale * ((f * _rstd(f)) * g_ref[...])
            xo_ref[rows, :] = xo
            if with_next:
                refs[-1][rows, :] = ((xo * _rstd(xo)) * refs[4][...]).astype(refs[-1].dtype)

        _for_row_chunks(tm, finish)


def _down_norm_res(a, w, x, g, scale, g_next=None, after=()):
    t, kk = a.shape
    d = w.shape[1]
    tm = _big_row_tile(t)
    tk = 512 if kk % 512 == 0 else 256
    nk = kk // tk
    with_next = g_next is not None
    once = pl.Buffered(1)
    vec = pl.BlockSpec((1, d), lambda i, k: (0, 0))
    rows = pl.BlockSpec((tm, d), lambda i, k: (i, 0))
    rows_once = pl.BlockSpec((tm, d), lambda i, k: (i, 0), pipeline_mode=once)
    return _call(
        functools.partial(_down_body, scale=scale, nk=nk, tm=tm, with_next=with_next), name="down_norm_res",
        out_shape=(_sds((t, d), F32), _sds((t, d), BF16)) + ((_sds((t, d), BF16),) if with_next else ()),
        grid=(t // tm, nk),
        in_specs=[pl.BlockSpec((tm, tk), lambda i, k: (i, k)), pl.BlockSpec((tk, d), lambda i, k: (k, 0)),
                  rows_once, vec] + ([vec] if with_next else []),
        out_specs=(rows, rows) + ((rows_once,) if with_next else ()),
        after=after)(a, w, x, g, *((g_next,) if with_next else ()))


def _pool_inv_count(g, i, tm):
    t1 = i * tm + lax.broadcasted_iota(jnp.int32, (tm, 1), 0) + 1
    return 1.0 / jnp.minimum(t1, 2 << g).astype(F32)


def _pool_windows(e, g):
    s2 = e + pltpu.roll(e, 1, 0)
    s4 = s2 + pltpu.roll(s2, 2, 0)
    s8 = s4 + pltpu.roll(s4, 4, 0)
    s16 = s8 + pltpu.roll(s8, 8, 0)
    return jnp.where(g == 0, s2, jnp.where(g == 1, s4, jnp.where(g == 2, s8, s16)))


def _pool_delta(p_ref, ph_ref, g, i, tm):
    pm = p_ref[...].astype(F32)
    ph = jnp.where(i == 0, 0.0, ph_ref[...].astype(F32))
    e = jnp.concatenate([ph, pm], axis=0)
    s = _pool_windows(e, g)[POOL_HALO:]
    return s * _pool_inv_count(g, i, tm) - pm


def _pool_fwd_body(p_ref, ph_ref, w_ref, sc_ref, o_ref, *, tm):
    g = pl.program_id(0)
    i = pl.program_id(1)
    d = _pool_delta(p_ref, ph_ref, g, i, tm)
    o_ref[...] = (_dot(d.astype(BF16), w_ref[...]) * sc_ref[...]).astype(o_ref.dtype)


def _pool_specs(t, c, tm):
    hb = tm // POOL_HALO
    main = pl.BlockSpec((tm, c), lambda g, i: (i, g))
    halo_before = pl.BlockSpec((POOL_HALO, c), lambda g, i: (jnp.maximum(i * hb - 1, 0), g))
    return main, halo_before


def _pool_fwd(proj, wgrp, scale):
    t = proj.shape[0]
    c = wgrp.shape[1]
    tm = _row_tile(t)
    main, halo = _pool_specs(t, c, tm)
    return _call(
        functools.partial(_pool_fwd_body, tm=tm), name="pool_fwd",
        out_shape=_sds((t, N_POOL_GROUPS * c), BF16), grid=(N_POOL_GROUPS, t // tm),
        in_specs=[main, halo, pl.BlockSpec((None, c, c), lambda g, i: (g, 0, 0)),
                  pl.BlockSpec((1, c), lambda g, i: (0, g))],
        out_specs=pl.BlockSpec((tm, c), lambda g, i: (i, g)))(proj, proj, wgrp, scale)


def _tril_mask():
    r = lax.broadcasted_iota(jnp.int32, (SGU_CHUNK, SGU_CHUNK), 0)
    c = lax.broadcasted_iota(jnp.int32, (SGU_CHUNK, SGU_CHUNK), 1)
    return r >= c


def _tril_weights(ws_ref, h):
    return jnp.where(_tril_mask(), ws_ref[h], 0.0)


def _sgu_fwd_body(u_ref, v_ref, gain_ref, ws_ref, b_ref, q_ref, *, tm, heads):
    ug = _gelu(u_ref[...].astype(F32))
    vg = _gelu(v_ref[...].astype(F32))
    vn = ((vg * _rstd(vg)) * gain_ref[...]).astype(BF16)
    for h in range(heads):
        wt = _tril_weights(ws_ref, h).astype(BF16)
        cols = slice(h * SGU_HEAD, (h + 1) * SGU_HEAD)
        for c in range(tm // SGU_CHUNK):
            rows = slice(c * SGU_CHUNK, (c + 1) * SGU_CHUNK)
            sg = _dot(wt, vn[rows, cols]) + b_ref[h]
            q_ref[rows, cols] = (ug[rows, cols] * sg).astype(q_ref.dtype)


def _sgu_fwd(proj, pw, gain, ws, b3):
    t = proj.shape[0]
    sw = gain.shape[1]
    heads = sw // SGU_HEAD
    tm = _row_tile(t)
    ub = pw // sw
    return _call(
        functools.partial(_sgu_fwd_body, tm=tm, heads=heads), name="sgu_fwd",
        out_shape=_sds((t, sw), BF16), grid=(t // tm,),
        in_specs=[pl.BlockSpec((tm, sw), lambda i: (i, ub)),
                  pl.BlockSpec((tm, sw), lambda i: (i, ub + 1)),
                  pl.BlockSpec((1, sw), lambda i: (0, 0)),
                  pl.BlockSpec((heads, SGU_CHUNK, SGU_CHUNK), lambda i: (0, 0, 0)),
                  pl.BlockSpec((heads, SGU_CHUNK, 1), lambda i: (0, 0, 0))],
        out_specs=pl.BlockSpec((tm, sw), lambda i: (i, 0)))(proj, proj, gain, ws, b3)


def _mix_gate_body(*refs, n, n_gate):
    yps_ref, q_ref = refs[:2]
    ga_refs, gb_refs = refs[2:2 + n_gate], refs[2 + n_gate:2 + 2 * n_gate]
    wp_ref, ws_ref, m_ref, part_ref = refs[2 + 2 * n_gate:]
    per = N_DEV // n_gate
    yps = yps_ref[...]
    q = q_ref[...]
    for s in range(N_DEV):
        cols = slice(s * n, (s + 1) * n)
        within = slice((s % per) * n, (s % per + 1) * n)
        ya = _dot(yps, wp_ref[s])
        yb = _dot(q, ws_ref[s])
        sa = _sigmoid(ga_refs[s // per][:, within].astype(F32))
        sb = _sigmoid(gb_refs[s // per][:, within].astype(F32))
        m_ref[:, cols] = (sa * ya + sb * yb).astype(m_ref.dtype)
        part_ref[0, :, cols] = sa.astype(part_ref.dtype)
        part_ref[1, :, cols] = sb.astype(part_ref.dtype)
        part_ref[2, :, cols] = (ya * (sa * (1.0 - sa))).astype(part_ref.dtype)
        part_ref[3, :, cols] = (yb * (sb * (1.0 - sb))).astype(part_ref.dtype)


def _mix_gate(yps, q, proj, gate_off, wpo, wso):
    t, pw = yps.shape
    sw = q.shape[1]
    n = wpo.shape[2]
    d = N_DEV * n
    tm = _row_tile(t) // 2
    gw = d
    while gate_off % gw:
        gw //= 2
    n_gate = d // gw
    gates = [pl.BlockSpec((tm, gw), functools.partial(lambda i, b: (i, b), b=(gate_off + k * gw) // gw))
             for k in range(2 * n_gate)]
    whole = pl.Buffered(1)
    return _call(
        functools.partial(_mix_gate_body, n=n, n_gate=n_gate), name="mix_gate",
        out_shape=(_sds((t, d), BF16), _sds((4, t, d), BF16)), grid=(t // tm,),
        in_specs=[pl.BlockSpec((tm, pw), lambda i: (i, 0)), pl.BlockSpec((tm, sw), lambda i: (i, 0))] + gates
        + [pl.BlockSpec((N_DEV, pw, n), lambda i: (0, 0, 0), pipeline_mode=whole),
           pl.BlockSpec((N_DEV, sw, n), lambda i: (0, 0, 0), pipeline_mode=whole)],
        out_specs=(pl.BlockSpec((tm, d), lambda i: (i, 0)), pl.BlockSpec((4, tm, d), lambda i: (0, i, 0))))(
            yps, q, *([proj] * (2 * n_gate)), wpo, wso)


def _loss_body(y_ref, t_ref, dy_ref, l_ref, *, inv_d):
    i = pl.program_id(0)

    @pl.when(i == 0)
    def _():
        l_ref[...] = jnp.zeros_like(l_ref)

    e = y_ref[...] - t_ref[...]
    dy_ref[...] = e * inv_d
    l_ref[...] += jnp.sum(e * e)


def _loss_grad(y, target):
    t, d = y.shape
    tm = _row_tile(t)
    blk = pl.BlockSpec((tm, d), lambda i: (i, 0))
    return _call(
        functools.partial(_loss_body, inv_d=1.0 / d), name="loss_grad",
        out_shape=(_sds((t, d), F32), _sds((8, 128), F32)), grid=(t // tm,),
        in_specs=[blk, blk],
        out_specs=(blk, pl.BlockSpec((8, 128), lambda i: (0, 0))))(y, target)


def _norm_bwd_body(dy_ref, f_ref, g_ref, df_ref, dg_ref, *, scale):
    i = pl.program_id(0)

    @pl.when(i == 0)
    def _():
        dg_ref[...] = jnp.zeros_like(dg_ref)

    f = f_ref[...].astype(F32)
    rstd = _rstd(f)
    fn = f * rstd
    dr = scale * dy_ref[...]
    dg_ref[...] += jnp.sum(dr * fn, axis=0, keepdims=True)
    dfn = dr * g_ref[...]
    df = rstd * (dfn - fn * jnp.mean(dfn * fn, axis=-1, keepdims=True))
    df_ref[...] = df.astype(df_ref.dtype)


def _norm_res_bwd(dy, f, g, scale, after=()):
    t, d = dy.shape
    tm = _row_tile(t)
    blk = pl.BlockSpec((tm, d), lambda i: (i, 0))
    vec = pl.BlockSpec((1, d), lambda i: (0, 0))
    return _call(
        functools.partial(_norm_bwd_body, scale=scale), name="norm_res_bwd",
        out_shape=(_sds((t, d), BF16), _sds((1, d), F32)), grid=(t // tm,),
        in_specs=[blk, blk, vec], out_specs=(blk, vec), after=after)(dy, f, g)


def _dgrad_ffn_body(df_ref, w_ref, gu_ref, dz_ref):
    da = _dot_nt(df_ref[...], w_ref[...])
    dz_ref[0] = (da * gu_ref[0].astype(F32)).astype(dz_ref.dtype)
    dz_ref[1] = (da * gu_ref[1].astype(F32)).astype(dz_ref.dtype)


def _dgrad_ffn(df, wd, gu):
    t, d = df.shape
    f = wd.shape[0]
    tm = _row_tile(t)
    tn = f // 4
    blk = pl.BlockSpec((2, tm, tn), lambda j, i: (0, i, j))
    return _call(
        functools.partial(_dgrad_ffn_body), name="dgrad_ffn", out_shape=_sds((2, t, f), BF16),
        grid=(f // tn, t // tm),
        in_specs=[pl.BlockSpec((tm, d), lambda j, i: (i, 0)),
                  pl.BlockSpec((tn, d), lambda j, i: (j, 0)), blk],
        out_specs=blk)(df, wd, gu)


def _dgrad_gate_body(do_ref, w_ref, part_ref, dya_ref, dyb_ref, dga_ref, dgb_ref):
    dm = _dot_nt(do_ref[...], w_ref[...])
    for k, out in enumerate((dya_ref, dyb_ref, dga_ref, dgb_ref)):
        out[...] = (dm * part_ref[k].astype(F32)).astype(out.dtype)


def _dgrad_gate(do, wout, parts):
    t, d = do.shape
    tm = _row_tile(t)
    tn = 512 if d % 512 == 0 else 256
    blk = pl.BlockSpec((tm, tn), lambda j, i: (i, j))
    out = _sds((t, d), BF16)
    return _call(
        functools.partial(_dgrad_gate_body), name="dgrad_gate", out_shape=(out, out, out, out),
        grid=(d // tn, t // tm),
        in_specs=[pl.BlockSpec((tm, d), lambda j, i: (i, 0)),
                  pl.BlockSpec((tn, d), lambda j, i: (j, 0)),
                  pl.BlockSpec((4, tm, tn), lambda j, i: (0, i, j))],
        out_specs=(blk, blk, blk, blk))(do, wout, parts)


def _wgrad_rows_body(a_ref, b_ref, o_ref, acc_ref, *, nt):
    t = pl.program_id(1)

    @pl.when(t == 0)
    def _():
        acc_ref[...] = jnp.zeros_like(acc_ref)

    acc_ref[...] += _dot_tn(a_ref[...], b_ref[...])

    @pl.when(t == nt - 1)
    def _():
        o_ref[...] = acc_ref[...].astype(o_ref.dtype)


def _wgrad_rows(a, b, after=()):
    t, k = a.shape
    n = b.shape[1]
    tt = _big_row_tile(t)
    tk = k // 4
    nt = t // tt
    return _call(
        functools.partial(_wgrad_rows_body, nt=nt), name="wgrad_rows", out_shape=_sds((k, n), BF16),
        grid=(k // tk, nt),
        in_specs=[pl.BlockSpec((tt, tk), lambda r, s: (s, r)),
                  pl.BlockSpec((tt, n), lambda r, s: (s, 0))],
        out_specs=pl.BlockSpec((tk, n), lambda r, s: (r, 0)),
        scratch_shapes=[pltpu.VMEM((tk, n), F32)], after=after)(a, b)


def _wgrad_cs(a, b3):
    t, k = a.shape
    parts = b3.shape[0]
    per = N_DEV // parts
    n = b3.shape[2] // per
    tt = _big_row_tile(t)
    nt = t // tt
    return _call(
        functools.partial(_wgrad_rows_body, nt=nt), name="wgrad_cs", out_shape=_sds((N_DEV, k, n), BF16),
        grid=(N_DEV, nt),
        in_specs=[pl.BlockSpec((tt, k), lambda r, s: (s, 0)),
                  pl.BlockSpec((None, tt, n), lambda r, s: (r // per, s, r % per))],
        out_specs=pl.BlockSpec((None, k, n), lambda r, s: (r, 0, 0)),
        scratch_shapes=[pltpu.VMEM((k, n), F32)])(a, b3)


def _wgrad_small_body(a_ref, b_ref, o_ref, acc_ref, *, nt, n):
    t = pl.program_id(0)

    @pl.when(t == 0)
    def _():
        acc_ref[...] = jnp.zeros_like(acc_ref)

    a = a_ref[...]
    for s in range(N_DEV):
        acc_ref[s] += _dot_tn(a, b_ref[:, s * n:(s + 1) * n])

    @pl.when(t == nt - 1)
    def _():
        o_ref[...] = acc_ref[...].astype(o_ref.dtype)


def _wgrad_cs_small(a, b):
    t, k = a.shape
    d = b.shape[1]
    n = d // N_DEV
    tt = _big_row_tile(t)
    nt = t // tt
    return _call(
        functools.partial(_wgrad_small_body, nt=nt, n=n), name="wgrad_cs_small",
        out_shape=_sds((N_DEV, k, n), BF16), grid=(nt,),
        in_specs=[pl.BlockSpec((tt, k), lambda s: (s, 0)), pl.BlockSpec((tt, d), lambda s: (s, 0))],
        out_specs=pl.BlockSpec((N_DEV, k, n), lambda s: (0, 0, 0)),
        scratch_shapes=[pltpu.VMEM((N_DEV, k, n), F32)])(a, b)


def _dgrad_cs_body(dz_ref, w_ref, o_ref, *, n):
    acc = None
    for s in range(N_DEV):
        term = _dot_nt(dz_ref[:, s * n:(s + 1) * n], w_ref[s])
        acc = term if acc is None else acc + term
    o_ref[...] = acc.astype(o_ref.dtype)


def _dgrad_cs(dz, wg, after=()):
    t, d = dz.shape
    k, n = wg.shape[1], wg.shape[2]
    tm = _row_tile(t)
    return _call(
        functools.partial(_dgrad_cs_body, n=n), name="dgrad_cs", out_shape=_sds((t, k), BF16),
        grid=(t // tm,),
        in_specs=[pl.BlockSpec((tm, d), lambda i: (i, 0)),
                  pl.BlockSpec((N_DEV, k, n), lambda i: (0, 0, 0))],
        out_specs=pl.BlockSpec((tm, k), lambda i: (i, 0)), after=after)(dz, wg)


def _dgrad_cs_norm_body(dz_ref, w_ref, x_hbm, g_ref, dy_hbm, o_ref, dg_ref, x_ref, dy_ref, sem, *, tm):
    i = pl.program_id(0)
    s = pl.program_id(1)
    tile = pl.ds(pl.multiple_of(i * tm, tm), tm)

    def epilogue_operands():
        return (pltpu.make_async_copy(x_hbm.at[tile], x_ref, sem.at[0]),
                pltpu.make_async_copy(dy_hbm.at[tile], dy_ref, sem.at[1]))

    @pl.when(jnp.logical_and(i == 0, s == 0))
    def _():
        dg_ref[...] = jnp.zeros_like(dg_ref)

    @pl.when(s == 0)
    def _():
        o_ref[...] = jnp.zeros_like(o_ref)
        for cp in epilogue_operands():
            cp.start()

    o_ref[...] += _dot_nt(dz_ref[...], w_ref[...])

    @pl.when(s == N_DEV - 1)
    def _():
        for cp in epilogue_operands():
            cp.wait()

        def finish(rows):
            dh = o_ref[rows, :]
            x = x_ref[rows, :]
            rstd = _rstd(x)
            xn = x * rstd
            dg_ref[...] += jnp.sum(dh * xn, axis=0, keepdims=True)
            dxn = dh * g_ref[...]
            o_ref[rows, :] = dy_ref[rows, :] + rstd * (dxn - xn * jnp.mean(dxn * xn, axis=-1, keepdims=True))

        _for_row_chunks(tm, finish)


def _dgrad_cs_norm(dz3, wg, x, g, dy, after=()):
    parts, t, _ = dz3.shape
    per = N_DEV // parts
    d, n = wg.shape[1], wg.shape[2]
    tm = _big_row_tile(t)
    blk = pl.BlockSpec((tm, d), lambda i, s: (i, 0), pipeline_mode=pl.Buffered(1))
    vec = pl.BlockSpec((1, d), lambda i, s: (0, 0))
    return _call(
        functools.partial(_dgrad_cs_norm_body, tm=tm), name="dgrad_cs_norm",
        out_shape=(_sds((t, d), F32), _sds((1, d), F32)), grid=(t // tm, N_DEV),
        in_specs=[pl.BlockSpec((None, tm, n), lambda i, s: (s // per, i, s % per)),
                  pl.BlockSpec((None, d, n), lambda i, s: (s, 0, 0)), ANY_SPEC, vec, ANY_SPEC],
        out_specs=(blk, vec),
        scratch_shapes=[pltpu.VMEM((tm, d), F32), pltpu.VMEM((tm, d), F32), pltpu.SemaphoreType.DMA((2,))],
        after=after)(dz3, wg, x, g, dy)


def _pool_bwd_body(p_ref, ph_ref, dy_ref, dyh_ref, w_ref, sc_ref, dp_ref, dw_ref, dsc_ref, *, tm, nt):
    g = pl.program_id(0)
    i = pl.program_id(1)

    @pl.when(i == 0)
    def _():
        dw_ref[...] = jnp.zeros_like(dw_ref)
        dsc_ref[...] = jnp.zeros_like(dsc_ref)

    w = w_ref[...]
    sc = sc_ref[...]
    d = _pool_delta(p_ref, ph_ref, g, i, tm).astype(BF16)
    dyps = dy_ref[...].astype(F32)
    dsc_ref[...] += jnp.sum(dyps * _dot(d, w), axis=0, keepdims=True)
    dyp = (dyps * sc).astype(BF16)
    dw_ref[...] += _dot_tn(d, dyp)
    dyp_after = (jnp.where(i == nt - 1, 0.0, dyh_ref[...].astype(F32)) * sc).astype(BF16)
    dd = jnp.concatenate([_dot_nt(dyp, w), _dot_nt(dyp_after, w)], axis=0)
    rows = tm + POOL_HALO
    t1 = i * tm + lax.broadcasted_iota(jnp.int32, (rows, 1), 0) + 1
    e = dd / jnp.minimum(t1, 2 << g).astype(F32)
    r2 = e + pltpu.roll(e, rows - 1, 0)
    r4 = r2 + pltpu.roll(r2, rows - 2, 0)
    r8 = r4 + pltpu.roll(r4, rows - 4, 0)
    r16 = r8 + pltpu.roll(r8, rows - 8, 0)
    r = jnp.where(g == 0, r2, jnp.where(g == 1, r4, jnp.where(g == 2, r8, r16)))
    dp_ref[...] = (r[:tm] - dd[:tm]).astype(dp_ref.dtype)


def _pool_bwd(proj, dyps, wgrp, scale):
    t = proj.shape[0]
    c = wgrp.shape[1]
    tm = _row_tile(t)
    nt = t // tm
    hb = tm // POOL_HALO
    last = t // POOL_HALO - 1
    main, halo = _pool_specs(t, c, tm)
    halo_after = pl.BlockSpec((POOL_HALO, c), lambda g, i: (jnp.minimum((i + 1) * hb, last), g))
    return _call(
        functools.partial(_pool_bwd_body, tm=tm, nt=nt), name="pool_bwd",
        out_shape=(_sds((t, N_POOL_GROUPS * c), BF16), _sds((N_POOL_GROUPS, c, c), F32),
                   _sds((1, N_POOL_GROUPS * c), F32)),
        grid=(N_POOL_GROUPS, nt),
        in_specs=[main, halo, main, halo_after, pl.BlockSpec((None, c, c), lambda g, i: (g, 0, 0)),
                  pl.BlockSpec((1, c), lambda g, i: (0, g))],
        out_specs=(main, pl.BlockSpec((None, c, c), lambda g, i: (g, 0, 0)),
                   pl.BlockSpec((1, c), lambda g, i: (0, g))))(proj, proj, dyps, dyps, wgrp, scale)


def _sgu_bwd_body(u_ref, v_ref, dq_ref, gain_ref, ws_ref, b_ref, duv_ref, dws_ref, db_ref, dgain_ref,
                  dvn_ref, *, tm, heads):
    i = pl.program_id(0)

    @pl.when(i == 0)
    def _():
        dws_ref[...] = jnp.zeros_like(dws_ref)
        db_ref[...] = jnp.zeros_like(db_ref)
        dgain_ref[...] = jnp.zeros_like(dgain_ref)

    sw = heads * SGU_HEAD
    u = u_ref[...].astype(F32)
    v = v_ref[...].astype(F32)
    ug = _gelu(u)
    vg = _gelu(v)
    rstd = _rstd(vg)
    vgn = vg * rstd
    gain = gain_ref[...]
    vn = (vgn * gain).astype(BF16)
    dq = dq_ref[...].astype(F32)
    for h in range(heads):
        wt = _tril_weights(ws_ref, h).astype(BF16)
        cols = slice(h * SGU_HEAD, (h + 1) * SGU_HEAD)
        dws = jnp.zeros((SGU_CHUNK, SGU_CHUNK), F32)
        dbh = jnp.zeros((SGU_CHUNK, 1), F32)
        for c in range(tm // SGU_CHUNK):
            rows = slice(c * SGU_CHUNK, (c + 1) * SGU_CHUNK)
            vn_c = vn[rows, cols]
            sg = _dot(wt, vn_c) + b_ref[h]
            dq_c = dq[rows, cols]
            dsg = dq_c * ug[rows, cols]
            duv_ref[rows, cols] = (dq_c * sg * _gelu_grad(u[rows, cols])).astype(duv_ref.dtype)
            dsg_b = dsg.astype(BF16)
            dws = dws + _dot_nt(dsg_b, vn_c)
            dbh = dbh + jnp.sum(dsg, axis=1, keepdims=True)
            dvn_ref[rows, cols] = _dot_tn(wt, dsg_b)
        dws_ref[h] += jnp.where(_tril_mask(), dws, 0.0)
        db_ref[h] += dbh
    dvn = dvn_ref[...]
    dgain_ref[...] += jnp.sum(dvn * vgn, axis=0, keepdims=True)
    dvgn = dvn * gain
    dvg = rstd * (dvgn - vgn * jnp.mean(dvgn * vgn, axis=-1, keepdims=True))
    duv_ref[:, sw:] = (dvg * _gelu_grad(v)).astype(duv_ref.dtype)


def _sgu_bwd(proj, pw, dq, gain, ws, b3):
    t = proj.shape[0]
    sw = gain.shape[1]
    heads = sw // SGU_HEAD
    tm = _row_tile(t)
    ub = pw // sw
    wblk = pl.BlockSpec((heads, SGU_CHUNK, SGU_CHUNK), lambda i: (0, 0, 0))
    bblk = pl.BlockSpec((heads, SGU_CHUNK, 1), lambda i: (0, 0, 0))
    vec = pl.BlockSpec((1, sw), lambda i: (0, 0))
    return _call(
        functools.partial(_sgu_bwd_body, tm=tm, heads=heads), name="sgu_bwd",
        out_shape=(_sds((t, 2 * sw), BF16), _sds((heads, SGU_CHUNK, SGU_CHUNK), F32),
                   _sds((heads, SGU_CHUNK, 1), F32), _sds((1, sw), F32)),
        grid=(t // tm,),
        in_specs=[pl.BlockSpec((tm, sw), lambda i: (i, ub)),
                  pl.BlockSpec((tm, sw), lambda i: (i, ub + 1)),
                  pl.BlockSpec((tm, sw), lambda i: (i, 0)), vec, wblk, bblk],
        out_specs=(pl.BlockSpec((tm, 2 * sw), lambda i: (i, 0)), wblk, bblk, vec),
        scratch_shapes=[pltpu.VMEM((tm, sw), F32)])(proj, proj, dq, gain, ws, b3)


def _cast_slot_body(me_ref, x_ref, o_ref):
    o_ref[...] = x_ref[...].astype(o_ref.dtype)


def _cast_to_slot(w, layer, me, dtype=BF16):
    _, r, c = w.shape
    tr = _slab_rows(r, 1024)
    spec = pltpu.PrefetchScalarGridSpec(
        num_scalar_prefetch=1, grid=(r // tr,),
        in_specs=[pl.BlockSpec((None, tr, c), lambda i, me_ref: (layer, i, 0))],
        out_specs=pl.BlockSpec((None, tr, c), lambda i, me_ref: (me_ref[0], i, 0)))
    return _call(functools.partial(_cast_slot_body), name="cast_to_slot", out_shape=_sds((N_DEV, r, c), dtype),
                 grid_spec=spec)(me, w)


def _mesh_position():
    x, y, c = (lax.axis_index(a) for a in MESH_AXES)
    return x, y, c


def _remote(src, dst, send, recv, device):
    return pltpu.make_async_remote_copy(src_ref=src, dst_ref=dst, send_sem=send, recv_sem=recv,
                                        device_id=device, device_id_type=pl.DeviceIdType.MESH)


def _other_chips(x, y):
    return [(1 - x, y), (x, 1 - y), (1 - x, 1 - y)]


def _gather_send_own(bufs, send, recv):
    x, y, c = _mesh_position()
    me = 4 * x + 2 * y + c
    targets = [(x, y, 1 - c)] + [(px, py, c) for px, py in _other_chips(x, y)]
    for k, buf in enumerate(bufs):
        for j, dev in enumerate(targets):
            _remote(buf.at[me], buf.at[me], send.at[4 * k + j], recv.at[4 * k + j], dev).start()


def _gather_await_own(bufs, send, recv):
    x, y, c = _mesh_position()
    me = 4 * x + 2 * y + c
    targets = [(x, y, 1 - c)] + [(px, py, c) for px, py in _other_chips(x, y)]
    for k, buf in enumerate(bufs):
        for j, dev in enumerate(targets):
            cp = _remote(buf.at[me], buf.at[4 * dev[0] + 2 * dev[1] + dev[2]], send.at[4 * k + j],
                         recv.at[4 * k + j], dev)
            cp.wait_recv()
            cp.wait_send()


def _gather_relay_send(bufs, send, recv):
    x, y, c = _mesh_position()
    for k, buf in enumerate(bufs):
        for j, (px, py) in enumerate(_other_chips(x, y)):
            blk = buf.at[4 * px + 2 * py + c]
            _remote(blk, blk, send.at[3 * k + j], recv.at[3 * k + j], (x, y, 1 - c)).start()


def _gather_relay_await(bufs, send, recv):
    x, y, c = _mesh_position()
    for k, buf in enumerate(bufs):
        for j, (px, py) in enumerate(_other_chips(x, y)):
            cp = _remote(buf.at[4 * px + 2 * py + c], buf.at[4 * px + 2 * py + 1 - c], send.at[3 * k + j],
                         recv.at[3 * k + j], (x, y, 1 - c))
            cp.wait_recv()
            cp.wait_send()


def _reduce_pair_send(grads, stage, send, recv):
    x, y, c = _mesh_position()
    for k, (g, s) in enumerate(zip(grads, stage)):
        for q in range(4):
            _remote(g.at[2 * q + 1 - c], s.at[q], send.at[4 * k + q], recv.at[4 * k + q], (x, y, 1 - c)).start()


def _reduce_pair_await(grads, stage, send, recv):
    x, y, c = _mesh_position()
    for k, (g, s) in enumerate(zip(grads, stage)):
        for q in range(4):
            cp = _remote(g.at[2 * q + 1 - c], s.at[q], send.at[4 * k + q], recv.at[4 * k + q], (x, y, 1 - c))
            cp.wait_recv()
            cp.wait_send()


def _reduce_chip_send(sums, stage, send, recv):
    x, y, c = _mesh_position()
    for k, (p, s) in enumerate(zip(sums, stage)):
        for j, (px, py) in enumerate(_other_chips(x, y)):
            _remote(p.at[2 * px + py], s.at[2 * x + y], send.at[3 * k + j], recv.at[3 * k + j],
                    (px, py, c)).start()


def _reduce_chip_await(sums, stage, send, recv):
    x, y, c = _mesh_position()
    for k, (p, s) in enumerate(zip(sums, stage)):
        for j, (px, py) in enumerate(_other_chips(x, y)):
            cp = _remote(p.at[2 * px + py], s.at[2 * px + py], send.at[3 * k + j], recv.at[3 * k + j],
                         (px, py, c))
            cp.wait_recv()
            cp.wait_send()


def _split_body(*refs, n, n_sem_in, n_after, n_sem_out, steps, token):
    bufs = refs[:n]
    sems = refs[n:n + n_sem_in] + refs[n + n_sem_in + n_after:n + n_sem_in + n_after + n_sem_out]
    for step, lo in steps:
        step(bufs, sems[lo], sems[lo + 1])
    if token:
        refs[-1][...] = jnp.zeros_like(refs[-1])


def _split_call(steps, *, name, bufs, sems_in=(), sems_out=(), after=(), token=True):
    n = len(bufs)
    operands = [pltpu.with_memory_space_constraint(b, pltpu.HBM) for b in bufs] + list(sems_in) + list(after)
    in_specs = [HBM_SPEC] * n + [SEM_SPEC] * len(sems_in) + [ANY_SPEC] * len(after)
    out_shape = list(sems_out) + [pltpu.HBM(b.shape, b.dtype) for b in bufs]
    out_specs = [SEM_SPEC] * len(sems_out) + [HBM_SPEC] * n
    if token:
        out_shape.append(_sds((8, 128), F32))
        out_specs.append(pl.BlockSpec(memory_space=pltpu.VMEM))
    body = functools.partial(_split_body, n=n, n_sem_in=len(sems_in), n_after=len(after),
                             n_sem_out=len(sems_out), steps=tuple(steps), token=token)
    return pl.pallas_call(
        body, name=name, in_specs=in_specs, out_specs=tuple(out_specs), out_shape=tuple(out_shape),
        input_output_aliases={i: len(sems_out) + i for i in range(n)},
        compiler_params=pltpu.CompilerParams(
            has_side_effects=pltpu.SideEffectType.DATAFLOW_SIDE_EFFECTING))(*operands)


def _sem_pair(n, m):
    return [pltpu.SemaphoreType.DMA((n * m,)), pltpu.SemaphoreType.DMA((n * m,))]


def _gather_start(bufs, after, tag):
    n = len(bufs)
    out = _split_call([(_gather_send_own, 0)], name="gather_start_" + tag, bufs=bufs,
                      sems_out=_sem_pair(n, 4), after=after)
    return out[0], out[1], list(out[2:2 + n]), out[-1]


def _gather_relay(state, after, tag):
    send, recv, bufs, _ = state
    n = len(bufs)
    out = _split_call([(_gather_await_own, 0), (_gather_relay_send, 2)], name="gather_relay_" + tag, bufs=bufs,
                      sems_in=[send, recv], sems_out=_sem_pair(n, 3), after=after)
    return out[0], out[1], list(out[2:2 + n]), out[-1]


def _gather_finish(state, after, tag):
    send, recv, bufs, _ = state
    out = _split_call([(_gather_relay_await, 0)], name="gather_finish_" + tag, bufs=bufs,
                      sems_in=[send, recv], after=after, token=False)
    return list(out)


def _pair_arrays(n):
    def split(step):
        return lambda bufs, send, recv: step(bufs[:n], bufs[n:], send, recv)
    return split


def _reduce_pairs_start(grads, after, tag):
    n = len(grads)
    stage = [lax.empty((4,) + g.shape[1:], g.dtype) for g in grads]
    out = _split_call([(_pair_arrays(n)(_reduce_pair_send), 0)], name="reduce_pairs_start_" + tag,
                      bufs=list(grads) + stage, sems_out=_sem_pair(n, 4), after=after)
    return out[0], out[1], list(out[2:2 + n]), list(out[2 + n:2 + 2 * n]), out[-1]


def _reduce_pairs_finish(state, after, tag):
    send, recv, grads, stage, _ = state
    n = len(grads)
    out = _split_call([(_pair_arrays(n)(_reduce_pair_await), 0)], name="reduce_pairs_finish_" + tag,
                      bufs=list(grads) + list(stage), sems_in=[send, recv], after=after, token=False)
    return list(out[:n]), list(out[n:])


def _reduce_chips_start(sums, after, tag):
    n = len(sums)
    stage = [lax.empty(p.shape, p.dtype) for p in sums]
    out = _split_call([(_pair_arrays(n)(_reduce_chip_send), 0)], name="reduce_chips_start_" + tag,
                      bufs=list(sums) + stage, sems_out=_sem_pair(n, 3), after=after)
    return out[0], out[1], list(out[2:2 + n]), list(out[2 + n:2 + 2 * n]), out[-1]


def _reduce_chips_finish(state, after, tag):
    send, recv, sums, stage, _ = state
    n = len(sums)
    out = _split_call([(_pair_arrays(n)(_reduce_chip_await), 0)], name="reduce_chips_finish_" + tag,
                      bufs=list(sums) + list(stage), sems_in=[send, recv], after=after, token=False)
    return list(out[:n]), list(out[n:])


def _pair_sum_body(c_ref, a_ref, b_ref, o_ref):
    o_ref[...] = (a_ref[...].astype(F32) + b_ref[...].astype(F32)).astype(o_ref.dtype)


def _pair_sum(grad, stage, core):
    _, r, c = grad.shape
    tr = _slab_rows(r, 1024)
    blk = pl.BlockSpec((None, tr, c), lambda q, i, c_ref: (q, i, 0))
    spec = pltpu.PrefetchScalarGridSpec(
        num_scalar_prefetch=1, grid=(4, r // tr),
        in_specs=[pl.BlockSpec((None, tr, c), lambda q, i, c_ref: (2 * q + c_ref[0], i, 0)), blk],
        out_specs=blk)
    return _call(functools.partial(_pair_sum_body), name="pair_sum", out_shape=_sds((4, r, c), grad.dtype),
                 grid_spec=spec)(core, grad, stage)


def _adamw_math(w, g, m, v):
    m = ADAM_B1 * m + (1.0 - ADAM_B1) * g
    v = ADAM_B2 * v + (1.0 - ADAM_B2) * (g * g)
    m_hat = m / (1.0 - ADAM_B1 ** ADAM_STEP)
    v_hat = v / (1.0 - ADAM_B2 ** ADAM_STEP)
    delta = -ADAM_LR * (m_hat / (jnp.sqrt(v_hat) + ADAM_EPS) + ADAM_WD * w)
    return delta, m, v


def _adamw_body(q_ref, own_ref, r_ref, w_ref, m_ref, v_ref, *rest):
    g_out, d_out, m_out, v_out = rest[-4:]
    q = q_ref[0]
    g = None
    for j in range(4):
        term = jnp.where(q == j, own_ref[...], r_ref[j]).astype(F32)
        g = term if g is None else g + term
    delta, m, v = _adamw_math(w_ref[...], g, m_ref[...], v_ref[...])
    g_out[...] = g
    d_out[...] = delta
    m_out[...] = m
    v_out[...] = v


def _adamw_layer(sums, recv, w, m, v, prev, layer, chip):
    l, r, c = w.shape
    tr = _slab_rows(r)
    lay = pl.BlockSpec((None, tr, c), lambda i, q_ref: (layer, i, 0))
    spec = pltpu.PrefetchScalarGridSpec(
        num_scalar_prefetch=1, grid=(r // tr,),
        in_specs=[pl.BlockSpec((None, tr, c), lambda i, q_ref: (q_ref[0], i, 0)),
                  pl.BlockSpec((4, tr, c), lambda i, q_ref: (0, i, 0)), lay, lay, lay] + [ANY_SPEC] * 4,
        out_specs=(lay, lay, lay, lay))
    out = _sds((l, r, c), F32)
    return _call(
        functools.partial(_adamw_body), name="adamw", out_shape=(out, out, out, out), grid_spec=spec,
        aliases={6: 0, 7: 1, 8: 2, 9: 3})(chip, sums, recv, w, m, v, *prev)


def _adamw_small_body(r_ref, w_ref, m_ref, v_ref, g_out, d_out, m_out, v_out):
    g = r_ref[0]
    for j in range(1, N_DEV):
        g = g + r_ref[j]
    delta, m, v = _adamw_math(w_ref[...], g, m_ref[...], v_ref[...])
    g_out[...] = g
    d_out[...] = delta
    m_out[...] = m
    v_out[...] = v


def _adamw_small(recv, w, m, v):
    r, c = w.shape
    tr = 8
    for cand in (1024, 512, 256, 128, 64, 32, 16, 8):
        if r % cand == 0:
            tr = cand
            break
    blk = pl.BlockSpec((tr, c), lambda i: (i, 0))
    out = _sds((r, c), F32)
    return _call(
        functools.partial(_adamw_small_body), name="adamw_small", out_shape=(out, out, out, out),
        grid=(r // tr,),
        in_specs=[pl.BlockSpec((N_DEV, tr, c), lambda i: (0, i, 0)), blk, blk, blk],
        out_specs=(blk, blk, blk, blk))(recv, w, m, v)


def _as3d(a):
    return a.reshape(a.shape[0], -1, a.shape[-1])


GROUPS = (("w_ffn1_up", "w_ffn1_down"),
          ("w_in", "pool_group_w", "w_pool_out", "w_sgu_out", "w_out"),
          ("w_ffn2_up", "w_ffn2_down"))


def _ffn_forward(x, h, g_pre, wu, wd, g_post, g_next, after=(), mid=None):
    if h is None:
        h, after = _rms_fwd(x, g_pre, after=after), ()
    gu, a = _ffn_up(h, wu, after=after)
    tokens = tuple(mid(a)) if mid else ()
    wd = wd() if callable(wd) else wd
    x_out, f, *h_next = _down_norm_res(a, wd, x, g_post, MACARON_WEIGHT, g_next, after=tokens)
    return x_out, (h_next[0] if h_next else None), (wd, (x, h, gu, a, f))


def _ffn_backward(dy, saved, g_pre, wu, wd, g_post, mid=None, early=None):
    x, h, gu, a, f = saved
    df, dg_post = _norm_res_bwd(dy, f, g_post, MACARON_WEIGHT)
    dz = _dgrad_ffn(df, wd, gu)
    tokens = tuple(mid(dz)) if mid else ()
    dwd = _wgrad_rows(a, df).reshape(N_DEV, -1, df.shape[1])
    dwu = _wgrad_cs(h, dz)
    tokens += tuple(early([dwu, dwd])) if early else (dwu, dwd)
    dy, dg_pre = _dgrad_cs_norm(dz, wu, x, g_pre, dy, after=tokens)
    return dy, dg_pre, dg_post


def kernel(x, g_ffn1_pre, w_ffn1_up, w_ffn1_down, g_ffn1_post, g_mix_pre, w_in, pool_group_w, pool_scale, w_pool_out, sgu_v_gain, sgu_w_s, sgu_b_s, w_sgu_out, w_out, g_mix_post, g_ffn2_pre, w_ffn2_up, w_ffn2_down, g_ffn2_post, loss_target, m_g_ffn1_pre, m_w_ffn1_up, m_w_ffn1_down, m_g_ffn1_post, m_g_mix_pre, m_w_in, m_pool_group_w, m_pool_scale, m_w_pool_out, m_sgu_v_gain, m_sgu_w_s, m_sgu_b_s, m_w_sgu_out, m_w_out, m_g_mix_post, m_g_ffn2_pre, m_w_ffn2_up, m_w_ffn2_down, m_g_ffn2_post, v_g_ffn1_pre, v_w_ffn1_up, v_w_ffn1_down, v_g_ffn1_post, v_g_mix_pre, v_w_in, v_pool_group_w, v_pool_scale, v_w_pool_out, v_sgu_v_gain, v_sgu_w_s, v_sgu_b_s, v_w_sgu_out, v_w_out, v_g_mix_post, v_g_ffn2_pre, v_w_ffn2_up, v_w_ffn2_down, v_g_ffn2_post):
    given = dict(locals())
    weights = {n: given[n] for n in ORDER}
    mom_m = {n: given["m_" + n] for n in ORDER}
    mom_v = {n: given["v_" + n] for n in ORDER}
    depth = g_ffn1_pre.shape[0]
    d_model = x.shape[-1]
    pool_w = pool_scale.shape[1]
    sgu_w = sgu_v_gain.shape[1]
    groups, grp_rows, grp_c = pool_group_w.shape[1:]
    gate_off = pool_w + 2 * sgu_w

    xs = x.reshape(x.shape[-2], d_model)
    target = loss_target.reshape(xs.shape)

    px, py, pc = _mesh_position()
    me = (4 * px + 2 * py + pc).astype(jnp.int32).reshape(1)
    core = pc.astype(jnp.int32).reshape(1)
    chip = (2 * px + py).astype(jnp.int32).reshape(1)

    n_groups = len(GROUPS) * depth
    ffn_gains = {0: ("g_ffn1_pre", "g_ffn1_post"), 2: ("g_ffn2_pre", "g_ffn2_post")}

    def start_gather(t, after):
        layer, gi = divmod(t, len(GROUPS))
        bufs = [_cast_to_slot(_as3d(weights[n]), layer, me) for n in GROUPS[gi]]
        return _gather_start(bufs, after, "g%d" % t)

    def vec(name, i):
        return weights[name][i].reshape(1, -1)

    pre_gains = ("g_ffn1_pre", "g_mix_pre", "g_ffn2_pre")

    def forward_group(t, cur, h, bufs, after, mid):
        layer, gi = divmod(t, len(GROUPS))
        gw = dict(zip(GROUPS[gi], bufs))
        nxt = divmod(t + 1, len(GROUPS))
        g_next = vec(pre_gains[nxt[1]], nxt[0]) if t + 1 < n_groups else None
        if gi != 1:
            pre, post = ffn_gains[gi]
            wu, wd = (gw[n] for n in GROUPS[gi])
            wd = wd if callable(wd) else wd.reshape(-1, d_model)
            cur, h_next, (wd, sv) = _ffn_forward(cur, h, vec(pre, layer), wu, wd, vec(post, layer), g_next,
                                                 after=after, mid=mid)
            return cur, h_next, ((wu, wd), sv)
        wgrp = gw["pool_group_w"].reshape(N_DEV, groups, grp_rows, grp_c).transpose(1, 0, 2, 3)
        wts = dict(win=gw["w_in"], wgrp=wgrp.reshape(groups, grp_c, grp_c), wpo=gw["w_pool_out"],
                   wso=gw["w_sgu_out"], wout=gw["w_out"].reshape(-1, d_model),
                   ws=weights["sgu_w_s"][layer], b3=weights["sgu_b_s"][layer][:, :, None])
        x1 = cur
        if h is None:
            h, after = _rms_fwd(x1, vec("g_mix_pre", layer), after=after), ()
        proj = _mm_cs(h, wts["win"], after=after)
        tokens = tuple(mid(proj))
        yps = _pool_fwd(proj, wts["wgrp"], vec("pool_scale", layer))
        q = _sgu_fwd(proj, pool_w, vec("sgu_v_gain", layer), wts["ws"], wts["b3"])
        m, parts = _mix_gate(yps, q, proj, gate_off, wts["wpo"], wts["wso"])
        cur, o, *h_next = _down_norm_res(m, wts["wout"], x1, vec("g_mix_post", layer), 1.0, g_next, after=tokens)
        return cur, (h_next[0] if h_next else None), (wts, (x1, h, proj, yps, q, parts, m, o))

    first = [_gather_start([_cast_to_slot(_as3d(weights[n]), 0, me)], (), "g0" + n[-2:]) for n in GROUPS[0]]
    states = {1: start_gather(1, (first[0][-1], first[1][-1]))}
    up0 = _gather_finish(_gather_relay(first[0], (states[1][-1],), "g0up"), (), "g0up")
    down0 = []
    bufs = [up0[0], lambda: down0[0][0].reshape(-1, d_model)]
    saved = []
    cur, h = xs, None
    for t in range(n_groups):
        after = []
        if t + 2 < n_groups:
            states[t + 2] = start_gather(t + 2, (bufs[0],))
            after.append(states[t + 2][-1])
        relayed = []

        def mid(result, t=t, relayed=relayed):
            if t == 0:
                down0.append(_gather_finish(_gather_relay(first[1], (result,), "g0wn"), (), "g0wn"))
            if t + 1 < n_groups:
                relayed.append(_gather_relay(states[t + 1], (result,), "g%d" % (t + 1)))
            return tuple(r[-1] for r in relayed)

        cur, h, sv = forward_group(t, cur, h, bufs, tuple(after), mid)
        saved.append(sv)
        if t + 1 < n_groups:
            bufs = _gather_finish(relayed[0], (cur,), "g%d" % (t + 1))

    dy, sq = _loss_grad(cur, target)
    loss = lax.psum(0.5 * sq[0, 0] / d_model, MESH_AXES)

    small_grads = {n: [None] * depth for n in SMALL}
    outs = {n: None for n in BIG}

    def chip_sums(pairs, after, tag):
        grads, stage = _reduce_pairs_finish(pairs, after, tag)
        sums = [_pair_sum(g, s, core) for g, s in zip(grads, stage)]
        return _reduce_chips_start(sums, (), tag)

    def update(chips, after, t):
        layer, gi = divmod(t, len(GROUPS))
        sums, recv = _reduce_chips_finish(chips, after, "g%d" % t)
        for n, p, r in zip(GROUPS[gi], sums, recv):
            w3, m3, v3 = _as3d(weights[n]), _as3d(mom_m[n]), _as3d(mom_v[n])
            prev = outs[n] if outs[n] is not None else [lax.empty(w3.shape, F32) for _ in range(4)]
            outs[n] = _adamw_layer(p, r, w3, m3, v3, prev, layer, chip)

    def backward_group(t, dy, sv, mid, early):
        layer, gi = divmod(t, len(GROUPS))
        wts, sv = sv
        if gi != 1:
            pre, post = ffn_gains[gi]
            dy, dg, dgp = _ffn_backward(dy, sv, vec(pre, layer), wts[0], wts[1], vec(post, layer),
                                        mid=mid, early=early)
            small_grads[pre][layer], small_grads[post][layer] = dg, dgp
            return dy
        x1, h2, proj, yps, q, parts, m, o = sv
        do, small_grads["g_mix_post"][layer] = _norm_res_bwd(dy, o, vec("g_mix_post", layer), 1.0)
        dya, dyb, dga, dgb = _dgrad_gate(do, wts["wout"], parts)
        tokens = tuple(mid(dya)) if mid else ()
        dwout = _wgrad_rows(m, do).reshape(N_DEV, -1, d_model)
        dwpo = _wgrad_cs_small(yps, dya)
        dwso = _wgrad_cs_small(q, dyb)
        dyps = _dgrad_cs(dya, wts["wpo"], after=tokens)
        dq = _dgrad_cs(dyb, wts["wso"])
        dp, dwgrp, small_grads["pool_scale"][layer] = _pool_bwd(proj, dyps, wts["wgrp"], vec("pool_scale", layer))
        duv, dws, db3, small_grads["sgu_v_gain"][layer] = _sgu_bwd(
            proj, pool_w, dq, vec("sgu_v_gain", layer), wts["ws"], wts["b3"])
        small_grads["sgu_w_s"][layer] = dws
        small_grads["sgu_b_s"][layer] = db3
        dproj = jnp.concatenate([dp, duv, dga, dgb], axis=1)[None]
        dwin = _wgrad_cs(h2, dproj)
        dwgrp8 = dwgrp.reshape(groups, N_DEV, grp_rows, grp_c).transpose(1, 0, 2, 3)
        dwgrp8 = dwgrp8.reshape(N_DEV, groups * grp_rows, grp_c).astype(BF16)
        dy, small_grads["g_mix_pre"][layer] = _dgrad_cs_norm(
            dproj, wts["win"], x1, vec("g_mix_pre", layer), dy,
            after=tuple(early([dwin, dwgrp8, dwpo, dwso, dwout])))
        return dy

    pairs = None
    for t in reversed(range(n_groups)):
        started, mine = [], []

        def mid(first, t=t, started=started):
            started.append(chip_sums(pairs, (first,), "g%d" % (t + 1)))
            return (started[0][-1],)

        def early(grads, t=t, mine=mine):
            mine.append(_reduce_pairs_start(grads, (), "g%d" % t))
            return (mine[0][-1],)

        dy = backward_group(t, dy, saved[t], mid if pairs else None, early)
        if started:
            update(started[0], (dy,), t + 1)
        pairs = mine[0]
    last = chip_sums(pairs, (), "g0")

    def pack(parts):
        return jnp.concatenate([p.reshape(depth, -1) for p in parts], axis=1).reshape(-1, 128)

    part = pack([jnp.stack(small_grads[n]) for n in SMALL])
    small = _gather_start([_cast_to_slot(part[None], 0, me, F32)], (last[-1],), "small")
    done = tuple(outs[n][0] for n in BIG if outs[n] is not None)
    all_parts = _gather_finish(_gather_relay(small, done, "small"), (), "small")[0]
    small_out = _adamw_small(all_parts, pack([weights[n] for n in SMALL]), pack([mom_m[n] for n in SMALL]),
                             pack([mom_v[n] for n in SMALL]))
    update(last, (small_out[0],), 0)

    results = {}
    sizes = [weights[n][0].size for n in SMALL]
    for kind, packed in zip(("grad", "delta", "new_m", "new_v"), small_out):
        flat = packed.reshape(depth, -1)
        off = 0
        for n, sz in zip(SMALL, sizes):
            results[(kind, n)] = flat[:, off:off + sz].reshape(weights[n].shape)
            off += sz
    for n in BIG:
        for kind, arr in zip(("grad", "delta", "new_m", "new_v"), outs[n]):
            results[(kind, n)] = arr.reshape(weights[n].shape)

    grad_x = dy.reshape(x.shape)
    return (loss, grad_x, *[results[("grad", n)] for n in ORDER], *[results[("delta", n)] for n in ORDER],
            *[results[("new_m", n)] for n in ORDER], *[results[("new_v", n)] for n in ORDER])
```

```python
import functools

import jax
import jax.numpy as jnp
from jax import lax
from jax.experimental import pallas as pl
from jax.experimental.pallas import tpu as pltpu

F32 = jnp.float32
BF16 = jnp.bfloat16

N_DEV = 8
MESH_AXES = ("x", "y", "c")

EPS = 1e-6
MACARON_WEIGHT = 0.5
N_POOL_GROUPS = 4
POOL_HALO = 16
SGU_HEAD = 128
SGU_CHUNK = 128
GELU_C = 0.7978845608028654
GELU_A = 0.044715

ADAM_LR = 0.001
ADAM_B1 = 0.9
ADAM_B2 = 0.999
ADAM_EPS = 1e-08
ADAM_WD = 0.01
ADAM_STEP = 10

ROW_TILE = 512
VMEM_LIMIT_BYTES = 60 * 1024 * 1024

BIG = ("w_ffn1_up", "w_ffn1_down", "w_in", "pool_group_w", "w_pool_out", "w_sgu_out", "w_out",
       "w_ffn2_up", "w_ffn2_down")
SMALL = ("g_ffn1_pre", "g_ffn1_post", "g_mix_pre", "pool_scale", "sgu_v_gain", "sgu_w_s", "sgu_b_s",
         "g_mix_post", "g_ffn2_pre", "g_ffn2_post")
ORDER = ("g_ffn1_pre", "w_ffn1_up", "w_ffn1_down", "g_ffn1_post", "g_mix_pre", "w_in", "pool_group_w",
         "pool_scale", "w_pool_out", "sgu_v_gain", "sgu_w_s", "sgu_b_s", "w_sgu_out", "w_out",
         "g_mix_post", "g_ffn2_pre", "w_ffn2_up", "w_ffn2_down", "g_ffn2_post")


ANY_SPEC = pl.BlockSpec(memory_space=pl.ANY)
HBM_SPEC = pl.BlockSpec(memory_space=pltpu.HBM)
SEM_SPEC = pl.BlockSpec(memory_space=pltpu.SEMAPHORE)


def _skip_refs(body, n_in, n_skip, *refs):
    return body(*refs[:n_in], *refs[n_in + n_skip:])


def _call(body, *, name, out_shape, grid=(), in_specs=None, out_specs=None, scratch_shapes=(),
          aliases=None, after=(), grid_spec=None):
    extra = {}
    if grid_spec is not None:
        extra["grid_spec"] = grid_spec
    else:
        extra.update(in_specs=list(in_specs) + [ANY_SPEC] * len(after), out_specs=out_specs,
                     scratch_shapes=list(scratch_shapes))
        if grid:
            extra["grid"] = grid
        if after:
            body = functools.partial(_skip_refs, body, len(in_specs), len(after))
    if aliases:
        extra["input_output_aliases"] = aliases
    call = pl.pallas_call(
        body, name=name, out_shape=out_shape,
        compiler_params=pltpu.CompilerParams(vmem_limit_bytes=VMEM_LIMIT_BYTES), **extra)
    if after:
        return lambda *args: call(*args, *after)
    return call


def _sds(shape, dtype):
    return jax.ShapeDtypeStruct(tuple(shape), dtype)


def _dot(a, b):
    return jnp.dot(a, b, preferred_element_type=F32)


def _dot_nt(a, b):
    return lax.dot_general(a, b, (((1,), (1,)), ((), ())), preferred_element_type=F32)


def _dot_tn(a, b):
    return lax.dot_general(a, b, (((0,), (0,)), ((), ())), preferred_element_type=F32)


def _sigmoid(x):
    return 1.0 / (1.0 + jnp.exp(-x))


def _gelu(x):
    return 0.5 * x * (1.0 + jnp.tanh(GELU_C * (x + GELU_A * x * x * x)))


def _gelu_grad(x):
    t = jnp.tanh(GELU_C * (x + GELU_A * x * x * x))
    return 0.5 * (1.0 + t) + 0.5 * x * (1.0 - t * t) * (GELU_C * (1.0 + 3.0 * GELU_A * x * x))


def _rstd(x):
    return lax.rsqrt(jnp.mean(x * x, axis=-1, keepdims=True) + EPS)


def _row_tile(t):
    return min(t, ROW_TILE)


def _big_row_tile(t):
    return 2 * ROW_TILE if t % (2 * ROW_TILE) == 0 else _row_tile(t)


def _slab_rows(r, most=256):
    for cand in range(min(most, r) // 16 * 16, 15, -16):
        if r % cand == 0:
            return cand
    return r


def _rms_fwd_body(x_ref, g_ref, h_ref):
    x = x_ref[...]
    h_ref[...] = ((x * _rstd(x)) * g_ref[...]).astype(h_ref.dtype)


def _rms_fwd(x, g, after=()):
    t, d = x.shape
    tm = _row_tile(t)
    return _call(
        functools.partial(_rms_fwd_body), name="rms_fwd", out_shape=_sds((t, d), BF16),
        grid=(t // tm,),
        in_specs=[pl.BlockSpec((tm, d), lambda i: (i, 0)), pl.BlockSpec((1, d), lambda i: (0, 0))],
        out_specs=pl.BlockSpec((tm, d), lambda i: (i, 0)), after=after)(x, g)


def _mm_cs_body(h_ref, w_ref, o_ref):
    o_ref[...] = _dot(h_ref[...], w_ref[...]).astype(o_ref.dtype)


def _mm_cs(h, wg, after=()):
    t, k = h.shape
    n = wg.shape[2]
    tm = _row_tile(t)
    return _call(
        functools.partial(_mm_cs_body), name="mm_cs", out_shape=_sds((t, N_DEV * n), BF16),
        grid=(N_DEV, t // tm),
        in_specs=[pl.BlockSpec((tm, k), lambda j, i: (i, 0)),
                  pl.BlockSpec((None, k, n), lambda j, i: (j, 0, 0))],
        out_specs=pl.BlockSpec((tm, n), lambda j, i: (i, j)), after=after)(h, wg)


def _ffn_up_body(h_ref, wg_ref, wu_ref, gu_ref, a_ref):
    h = h_ref[...]
    g = _dot(h, wg_ref[...])
    u = _dot(h, wu_ref[...])
    sig = _sigmoid(g)
    silu = g * sig
    gu_ref[0] = (u * (sig * (1.0 + g * (1.0 - sig)))).astype(gu_ref.dtype)
    gu_ref[1] = silu.astype(gu_ref.dtype)
    a_ref[...] = (silu * u).astype(a_ref.dtype)


def _ffn_up(h, wg, after=()):
    t, k = h.shape
    n = wg.shape[2]
    half = N_DEV // 2
    f = half * n
    tm = _row_tile(t)
    return _call(
        functools.partial(_ffn_up_body), name="ffn_up",
        out_shape=(_sds((2, t, f), BF16), _sds((t, f), BF16)),
        grid=(half, t // tm),
        in_specs=[pl.BlockSpec((tm, k), lambda j, i: (i, 0)),
                  pl.BlockSpec((None, k, n), lambda j, i: (j, 0, 0)),
                  pl.BlockSpec((None, k, n), lambda j, i: (j + N_DEV // 2, 0, 0))],
        out_specs=(pl.BlockSpec((2, tm, n), lambda j, i: (0, i, j)),
                   pl.BlockSpec((tm, n), lambda j, i: (i, j))), after=after)(h, wg, wg)


EPILOGUE_ROWS = 256


def _for_row_chunks(rows, step):
    chunk = min(rows, EPILOGUE_ROWS)

    def body(r, carry):
        step(pl.ds(pl.multiple_of(r * chunk, chunk), chunk))
        return carry

    lax.fori_loop(0, rows // chunk, body, 0)


def _down_body(*refs, scale, nk, tm, with_next):
    a_ref, w_ref, x_ref, g_ref = refs[:4]
    xo_ref, f_ref = refs[4 + with_next:6 + with_next]
    k = pl.program_id(1)

    @pl.when(k == 0)
    def _():
        xo_ref[...] = jnp.zeros_like(xo_ref)

    xo_ref[...] += _dot(a_ref[...], w_ref[...])

    @pl.when(k == nk - 1)
    def _():
        def finish(rows):
            f = xo_ref[rows, :]
            f_ref[rows, :] = f.astype(f_ref.dtype)
            xo = x_ref[rows, :] + scale * ((f * _rstd(f)) * g_ref[...])
            xo_ref[rows, :] = xo
            if with_next:
                refs[-1][rows, :] = ((xo * _rstd(xo)) * refs[4][...]).astype(refs[-1].dtype)

        _for_row_chunks(tm, finish)


def _down_norm_res(a, w, x, g, scale, g_next=None, after=()):
    t, kk = a.shape
    d = w.shape[1]
    tm = _big_row_tile(t)
    tk = 512 if kk % 512 == 0 else 256
    nk = kk // tk
    with_next = g_next is not None
    once = pl.Buffered(1)
    vec = pl.BlockSpec((1, d), lambda i, k: (0, 0))
    rows = pl.BlockSpec((tm, d), lambda i, k: (i, 0))
    rows_once = pl.BlockSpec((tm, d), lambda i, k: (i, 0), pipeline_mode=once)
    return _call(
        functools.partial(_down_body, scale=scale, nk=nk, tm=tm, with_next=with_next), name="down_norm_res",
        out_shape=(_sds((t, d), F32), _sds((t, d), BF16)) + ((_sds((t, d), BF16),) if with_next else ()),
        grid=(t // tm, nk),
        in_specs=[pl.BlockSpec((tm, tk), lambda i, k: (i, k)), pl.BlockSpec((tk, d), lambda i, k: (k, 0)),
                  rows_once, vec] + ([vec] if with_next else []),
        out_specs=(rows, rows) + ((rows_once,) if with_next else ()),
        after=after)(a, w, x, g, *((g_next,) if with_next else ()))


def _pool_inv_count(g, i, tm):
    t1 = i * tm + lax.broadcasted_iota(jnp.int32, (tm, 1), 0) + 1
    return 1.0 / jnp.minimum(t1, 2 << g).astype(F32)


def _pool_windows(e, g):
    s2 = e + pltpu.roll(e, 1, 0)
    s4 = s2 + pltpu.roll(s2, 2, 0)
    s8 = s4 + pltpu.roll(s4, 4, 0)
    s16 = s8 + pltpu.roll(s8, 8, 0)
    return jnp.where(g == 0, s2, jnp.where(g == 1, s4, jnp.where(g == 2, s8, s16)))


def _pool_delta(p_ref, ph_ref, g, i, tm):
    pm = p_ref[...].astype(F32)
    ph = jnp.where(i == 0, 0.0, ph_ref[...].astype(F32))
    e = jnp.concatenate([ph, pm], axis=0)
    s = _pool_windows(e, g)[POOL_HALO:]
    return s * _pool_inv_count(g, i, tm) - pm


def _pool_fwd_body(p_ref, ph_ref, w_ref, sc_ref, o_ref, *, tm):
    g = pl.program_id(0)
    i = pl.program_id(1)
    d = _pool_delta(p_ref, ph_ref, g, i, tm)
    o_ref[...] = (_dot(d.astype(BF16), w_ref[...]) * sc_ref[...]).astype(o_ref.dtype)


def _pool_specs(t, c, tm):
    hb = tm // POOL_HALO
    main = pl.BlockSpec((tm, c), lambda g, i: (i, g))
    halo_before = pl.BlockSpec((POOL_HALO, c), lambda g, i: (jnp.maximum(i * hb - 1, 0), g))
    return main, halo_before


def _pool_fwd(proj, wgrp, scale):
    t = proj.shape[0]
    c = wgrp.shape[1]
    tm = _row_tile(t)
    main, halo = _pool_specs(t, c, tm)
    return _call(
        functools.partial(_pool_fwd_body, tm=tm), name="pool_fwd",
        out_shape=_sds((t, N_POOL_GROUPS * c), BF16), grid=(N_POOL_GROUPS, t // tm),
        in_specs=[main, halo, pl.BlockSpec((None, c, c), lambda g, i: (g, 0, 0)),
                  pl.BlockSpec((1, c), lambda g, i: (0, g))],
        out_specs=pl.BlockSpec((tm, c), lambda g, i: (i, g)))(proj, proj, wgrp, scale)


def _tril_mask():
    r = lax.broadcasted_iota(jnp.int32, (SGU_CHUNK, SGU_CHUNK), 0)
    c = lax.broadcasted_iota(jnp.int32, (SGU_CHUNK, SGU_CHUNK), 1)
    return r >= c


def _tril_weights(ws_ref, h):
    return jnp.where(_tril_mask(), ws_ref[h], 0.0)


def _sgu_fwd_body(u_ref, v_ref, gain_ref, ws_ref, b_ref, q_ref, *, tm, heads):
    ug = _gelu(u_ref[...].astype(F32))
    vg = _gelu(v_ref[...].astype(F32))
    vn = ((vg * _rstd(vg)) * gain_ref[...]).astype(BF16)
    for h in range(heads):
        wt = _tril_weights(ws_ref, h).astype(BF16)
        cols = slice(h * SGU_HEAD, (h + 1) * SGU_HEAD)
        for c in range(tm // SGU_CHUNK):
            rows = slice(c * SGU_CHUNK, (c + 1) * SGU_CHUNK)
            sg = _dot(wt, vn[rows, cols]) + b_ref[h]
            q_ref[rows, cols] = (ug[rows, cols] * sg).astype(q_ref.dtype)


def _sgu_fwd(proj, pw, gain, ws, b3):
    t = proj.shape[0]
    sw = gain.shape[1]
    heads = sw // SGU_HEAD
    tm = _row_tile(t)
    ub = pw // sw
    return _call(
        functools.partial(_sgu_fwd_body, tm=tm, heads=heads), name="sgu_fwd",
        out_shape=_sds((t, sw), BF16), grid=(t // tm,),
        in_specs=[pl.BlockSpec((tm, sw), lambda i: (i, ub)),
                  pl.BlockSpec((tm, sw), lambda i: (i, ub + 1)),
                  pl.BlockSpec((1, sw), lambda i: (0, 0)),
                  pl.BlockSpec((heads, SGU_CHUNK, SGU_CHUNK), lambda i: (0, 0, 0)),
                  pl.BlockSpec((heads, SGU_CHUNK, 1), lambda i: (0, 0, 0))],
        out_specs=pl.BlockSpec((tm, sw), lambda i: (i, 0)))(proj, proj, gain, ws, b3)


def _mix_gate_body(*refs, n, n_gate):
    yps_ref, q_ref = refs[:2]
    ga_refs, gb_refs = refs[2:2 + n_gate], refs[2 + n_gate:2 + 2 * n_gate]
    wp_ref, ws_ref, m_ref, part_ref = refs[2 + 2 * n_gate:]
    per = N_DEV // n_gate
    yps = yps_ref[...]
    q = q_ref[...]
    for s in range(N_DEV):
        cols = slice(s * n, (s + 1) * n)
        within = slice((s % per) * n, (s % per + 1) * n)
        ya = _dot(yps, wp_ref[s])
        yb = _dot(q, ws_ref[s])
        sa = _sigmoid(ga_refs[s // per][:, within].astype(F32))
        sb = _sigmoid(gb_refs[s // per][:, within].astype(F32))
        m_ref[:, cols] = (sa * ya + sb * yb).astype(m_ref.dtype)
        part_ref[0, :, cols] = sa.astype(part_ref.dtype)
        part_ref[1, :, cols] = sb.astype(part_ref.dtype)
        part_ref[2, :, cols] = (ya * (sa * (1.0 - sa))).astype(part_ref.dtype)
        part_ref[3, :, cols] = (yb * (sb * (1.0 - sb))).astype(part_ref.dtype)


def _mix_gate(yps, q, proj, gate_off, wpo, wso):
    t, pw = yps.shape
    sw = q.shape[1]
    n = wpo.shape[2]
    d = N_DEV * n
    tm = _row_tile(t) // 2
    gw = d
    while gate_off % gw:
        gw //= 2
    n_gate = d // gw
    gates = [pl.BlockSpec((tm, gw), functools.partial(lambda i, b: (i, b), b=(gate_off + k * gw) // gw))
             for k in range(2 * n_gate)]
    whole = pl.Buffered(1)
    return _call(
        functools.partial(_mix_gate_body, n=n, n_gate=n_gate), name="mix_gate",
        out_shape=(_sds((t, d), BF16), _sds((4, t, d), BF16)), grid=(t // tm,),
        in_specs=[pl.BlockSpec((tm, pw), lambda i: (i, 0)), pl.BlockSpec((tm, sw), lambda i: (i, 0))] + gates
        + [pl.BlockSpec((N_DEV, pw, n), lambda i: (0, 0, 0), pipeline_mode=whole),
           pl.BlockSpec((N_DEV, sw, n), lambda i: (0, 0, 0), pipeline_mode=whole)],
        out_specs=(pl.BlockSpec((tm, d), lambda i: (i, 0)), pl.BlockSpec((4, tm, d), lambda i: (0, i, 0))))(
            yps, q, *([proj] * (2 * n_gate)), wpo, wso)


def _loss_body(y_ref, t_ref, dy_ref, l_ref, *, inv_d):
    i = pl.program_id(0)

    @pl.when(i == 0)
    def _():
        l_ref[...] = jnp.zeros_like(l_ref)

    e = y_ref[...] - t_ref[...]
    dy_ref[...] = e * inv_d
    l_ref[...] += jnp.sum(e * e)


def _loss_grad(y, target):
    t, d = y.shape
    tm = _row_tile(t)
    blk = pl.BlockSpec((tm, d), lambda i: (i, 0))
    return _call(
        functools.partial(_loss_body, inv_d=1.0 / d), name="loss_grad",
        out_shape=(_sds((t, d), F32), _sds((8, 128), F32)), grid=(t // tm,),
        in_specs=[blk, blk],
        out_specs=(blk, pl.BlockSpec((8, 128), lambda i: (0, 0))))(y, target)


def _norm_bwd_body(dy_ref, f_ref, g_ref, df_ref, dg_ref, *, scale):
    i = pl.program_id(0)

    @pl.when(i == 0)
    def _():
        dg_ref[...] = jnp.zeros_like(dg_ref)

    f = f_ref[...].astype(F32)
    rstd = _rstd(f)
    fn = f * rstd
    dr = scale * dy_ref[...]
    dg_ref[...] += jnp.sum(dr * fn, axis=0, keepdims=True)
    dfn = dr * g_ref[...]
    df = rstd * (dfn - fn * jnp.mean(dfn * fn, axis=-1, keepdims=True))
    df_ref[...] = df.astype(df_ref.dtype)


def _norm_res_bwd(dy, f, g, scale, after=()):
    t, d = dy.shape
    tm = _row_tile(t)
    blk = pl.BlockSpec((tm, d), lambda i: (i, 0))
    vec = pl.BlockSpec((1, d), lambda i: (0, 0))
    return _call(
        functools.partial(_norm_bwd_body, scale=scale), name="norm_res_bwd",
        out_shape=(_sds((t, d), BF16), _sds((1, d), F32)), grid=(t // tm,),
        in_specs=[blk, blk, vec], out_specs=(blk, vec), after=after)(dy, f, g)


def _dgrad_ffn_body(df_ref, w_ref, gu_ref, dz_ref):
    da = _dot_nt(df_ref[...], w_ref[...])
    dz_ref[0] = (da * gu_ref[0].astype(F32)).astype(dz_ref.dtype)
    dz_ref[1] = (da * gu_ref[1].astype(F32)).astype(dz_ref.dtype)


def _dgrad_ffn(df, wd, gu):
    t, d = df.shape
    f = wd.shape[0]
    tm = _row_tile(t)
    tn = f // 4
    blk = pl.BlockSpec((2, tm, tn), lambda j, i: (0, i, j))
    return _call(
        functools.partial(_dgrad_ffn_body), name="dgrad_ffn", out_shape=_sds((2, t, f), BF16),
        grid=(f // tn, t // tm),
        in_specs=[pl.BlockSpec((tm, d), lambda j, i: (i, 0)),
                  pl.BlockSpec((tn, d), lambda j, i: (j, 0)), blk],
        out_specs=blk)(df, wd, gu)


def _dgrad_gate_body(do_ref, w_ref, part_ref, dya_ref, dyb_ref, dga_ref, dgb_ref):
    dm = _dot_nt(do_ref[...], w_ref[...])
    for k, out in enumerate((dya_ref, dyb_ref, dga_ref, dgb_ref)):
        out[...] = (dm * part_ref[k].astype(F32)).astype(out.dtype)


def _dgrad_gate(do, wout, parts):
    t, d = do.shape
    tm = _row_tile(t)
    tn = 512 if d % 512 == 0 else 256
    blk = pl.BlockSpec((tm, tn), lambda j, i: (i, j))
    out = _sds((t, d), BF16)
    return _call(
        functools.partial(_dgrad_gate_body), name="dgrad_gate", out_shape=(out, out, out, out),
        grid=(d // tn, t // tm),
        in_specs=[pl.BlockSpec((tm, d), lambda j, i: (i, 0)),
                  pl.BlockSpec((tn, d), lambda j, i: (j, 0)),
                  pl.BlockSpec((4, tm, tn), lambda j, i: (0, i, j))],
        out_specs=(blk, blk, blk, blk))(do, wout, parts)


def _wgrad_rows_body(a_ref, b_ref, o_ref, acc_ref, *, nt):
    t = pl.program_id(1)

    @pl.when(t == 0)
    def _():
        acc_ref[...] = jnp.zeros_like(acc_ref)

    acc_ref[...] += _dot_tn(a_ref[...], b_ref[...])

    @pl.when(t == nt - 1)
    def _():
        o_ref[...] = acc_ref[...].astype(o_ref.dtype)


def _wgrad_rows(a, b, after=()):
    t, k = a.shape
    n = b.shape[1]
    tt = _big_row_tile(t)
    tk = k // 4
    nt = t // tt
    return _call(
        functools.partial(_wgrad_rows_body, nt=nt), name="wgrad_rows", out_shape=_sds((k, n), BF16),
        grid=(k // tk, nt),
        in_specs=[pl.BlockSpec((tt, tk), lambda r, s: (s, r)),
                  pl.BlockSpec((tt, n), lambda r, s: (s, 0))],
        out_specs=pl.BlockSpec((tk, n), lambda r, s: (r, 0)),
        scratch_shapes=[pltpu.VMEM((tk, n), F32)], after=after)(a, b)


def _wgrad_cs(a, b3):
    t, k = a.shape
    parts = b3.shape[0]
    per = N_DEV // parts
    n = b3.shape[2] // per
    tt = _big_row_tile(t)
    nt = t // tt
    return _call(
        functools.partial(_wgrad_rows_body, nt=nt), name="wgrad_cs", out_shape=_sds((N_DEV, k, n), BF16),
        grid=(N_DEV, nt),
        in_specs=[pl.BlockSpec((tt, k), lambda r, s: (s, 0)),
                  pl.BlockSpec((None, tt, n), lambda r, s: (r // per, s, r % per))],
        out_specs=pl.BlockSpec((None, k, n), lambda r, s: (r, 0, 0)),
        scratch_shapes=[pltpu.VMEM((k, n), F32)])(a, b3)


def _wgrad_small_body(a_ref, b_ref, o_ref, acc_ref, *, nt, n):
    t = pl.program_id(0)

    @pl.when(t == 0)
    def _():
        acc_ref[...] = jnp.zeros_like(acc_ref)

    a = a_ref[...]
    for s in range(N_DEV):
        acc_ref[s] += _dot_tn(a, b_ref[:, s * n:(s + 1) * n])

    @pl.when(t == nt - 1)
    def _():
        o_ref[...] = acc_ref[...].astype(o_ref.dtype)


def _wgrad_cs_small(a, b):
    t, k = a.shape
    d = b.shape[1]
    n = d // N_DEV
    tt = _big_row_tile(t)
    nt = t // tt
    return _call(
        functools.partial(_wgrad_small_body, nt=nt, n=n), name="wgrad_cs_small",
        out_shape=_sds((N_DEV, k, n), BF16), grid=(nt,),
        in_specs=[pl.BlockSpec((tt, k), lambda s: (s, 0)), pl.BlockSpec((tt, d), lambda s: (s, 0))],
        out_specs=pl.BlockSpec((N_DEV, k, n), lambda s: (0, 0, 0)),
        scratch_shapes=[pltpu.VMEM((N_DEV, k, n), F32)])(a, b)


def _dgrad_cs_body(dz_ref, w_ref, o_ref, *, n):
    acc = None
    for s in range(N_DEV):
        term = _dot_nt(dz_ref[:, s * n:(s + 1) * n], w_ref[s])
        acc = term if acc is None else acc + term
    o_ref[...] = acc.astype(o_ref.dtype)


def _dgrad_cs(dz, wg, after=()):
    t, d = dz.shape
    k, n = wg.shape[1], wg.shape[2]
    tm = _row_tile(t)
    return _call(
        functools.partial(_dgrad_cs_body, n=n), name="dgrad_cs", out_shape=_sds((t, k), BF16),
        grid=(t // tm,),
        in_specs=[pl.BlockSpec((tm, d), lambda i: (i, 0)),
                  pl.BlockSpec((N_DEV, k, n), lambda i: (0, 0, 0))],
        out_specs=pl.BlockSpec((tm, k), lambda i: (i, 0)), after=after)(dz, wg)


def _dgrad_cs_norm_body(*refs, tm, scale_next):
    fused = scale_next is not None
    dz_ref, w_ref, x_hbm, g_ref, dy_hbm = refs[:5]
    f_hbm, gp_ref = refs[5:7] if fused else (None, None)
    o_ref, dg_ref = refs[5 + 2 * fused:7 + 2 * fused]
    df_ref, dgp_ref = refs[9:11] if fused else (None, None)
    x_ref, dy_ref = refs[7 + 4 * fused:9 + 4 * fused]
    f_ref = refs[13] if fused else None
    sem = refs[-1]
    i = pl.program_id(0)
    s = pl.program_id(1)
    tile = pl.ds(pl.multiple_of(i * tm, tm), tm)

    def epilogue_operands():
        copies = [pltpu.make_async_copy(x_hbm.at[tile], x_ref, sem.at[0]),
                  pltpu.make_async_copy(dy_hbm.at[tile], dy_ref, sem.at[1])]
        if fused:
            copies.append(pltpu.make_async_copy(f_hbm.at[tile], f_ref, sem.at[2]))
        return copies

    @pl.when(jnp.logical_and(i == 0, s == 0))
    def _():
        dg_ref[...] = jnp.zeros_like(dg_ref)
        if fused:
            dgp_ref[...] = jnp.zeros_like(dgp_ref)

    @pl.when(s == 0)
    def _():
        o_ref[...] = jnp.zeros_like(o_ref)
        for cp in epilogue_operands():
            cp.start()

    o_ref[...] += _dot_nt(dz_ref[...], w_ref[...])

    @pl.when(s == N_DEV - 1)
    def _():
        for cp in epilogue_operands():
            cp.wait()

        def finish(rows):
            dh = o_ref[rows, :]
            x = x_ref[rows, :]
            rstd = _rstd(x)
            xn = x * rstd
            dg_ref[...] += jnp.sum(dh * xn, axis=0, keepdims=True)
            dxn = dh * g_ref[...]
            dy = dy_ref[rows, :] + rstd * (dxn - xn * jnp.mean(dxn * xn, axis=-1, keepdims=True))
            o_ref[rows, :] = dy
            if fused:
                f = f_ref[rows, :].astype(F32)
                rstd_f = _rstd(f)
                fn = f * rstd_f
                dr = scale_next * dy
                dgp_ref[...] += jnp.sum(dr * fn, axis=0, keepdims=True)
                dfn = dr * gp_ref[...]
                df = rstd_f * (dfn - fn * jnp.mean(dfn * fn, axis=-1, keepdims=True))
                df_ref[rows, :] = df.astype(df_ref.dtype)

        _for_row_chunks(tm, finish)


def _dgrad_cs_norm(dz3, wg, x, g, dy, norm_next=None, after=()):
    parts, t, _ = dz3.shape
    per = N_DEV // parts
    d, n = wg.shape[1], wg.shape[2]
    tm = _big_row_tile(t)
    blk = pl.BlockSpec((tm, d), lambda i, s: (i, 0), pipeline_mode=pl.Buffered(1))
    vec = pl.BlockSpec((1, d), lambda i, s: (0, 0))
    fused = norm_next is not None
    f, gain, scale = norm_next if fused else (None, None, None)
    return _call(
        functools.partial(_dgrad_cs_norm_body, tm=tm, scale_next=scale), name="dgrad_cs_norm",
        out_shape=(_sds((t, d), F32), _sds((1, d), F32)) + ((_sds((t, d), BF16), _sds((1, d), F32)) if fused else ()),
        grid=(t // tm, N_DEV),
        in_specs=[pl.BlockSpec((None, tm, n), lambda i, s: (s // per, i, s % per)),
                  pl.BlockSpec((None, d, n), lambda i, s: (s, 0, 0)), ANY_SPEC, vec, ANY_SPEC]
        + ([ANY_SPEC, vec] if fused else []),
        out_specs=(blk, vec) + ((blk, vec) if fused else ()),
        scratch_shapes=[pltpu.VMEM((tm, d), F32), pltpu.VMEM((tm, d), F32)]
        + ([pltpu.VMEM((tm, d), BF16)] if fused else []) + [pltpu.SemaphoreType.DMA((3,))],
        after=after)(dz3, wg, x, g, dy, *((f, gain) if fused else ()))


def _pool_bwd_body(p_ref, ph_ref, dy_ref, dyh_ref, w_ref, sc_ref, dp_ref, dw_ref, dsc_ref, *, tm, nt):
    g = pl.program_id(0)
    i = pl.program_id(1)

    @pl.when(i == 0)
    def _():
        dw_ref[...] = jnp.zeros_like(dw_ref)
        dsc_ref[...] = jnp.zeros_like(dsc_ref)

    w = w_ref[...]
    sc = sc_ref[...]
    d = _pool_delta(p_ref, ph_ref, g, i, tm).astype(BF16)
    dyps = dy_ref[...].astype(F32)
    dsc_ref[...] += jnp.sum(dyps * _dot(d, w), axis=0, keepdims=True)
    dyp = (dyps * sc).astype(BF16)
    dw_ref[...] += _dot_tn(d, dyp)
    dyp_after = (jnp.where(i == nt - 1, 0.0, dyh_ref[...].astype(F32)) * sc).astype(BF16)
    dd = jnp.concatenate([_dot_nt(dyp, w), _dot_nt(dyp_after, w)], axis=0)
    rows = tm + POOL_HALO
    t1 = i * tm + lax.broadcasted_iota(jnp.int32, (rows, 1), 0) + 1
    e = dd / jnp.minimum(t1, 2 << g).astype(F32)
    r2 = e + pltpu.roll(e, rows - 1, 0)
    r4 = r2 + pltpu.roll(r2, rows - 2, 0)
    r8 = r4 + pltpu.roll(r4, rows - 4, 0)
    r16 = r8 + pltpu.roll(r8, rows - 8, 0)
    r = jnp.where(g == 0, r2, jnp.where(g == 1, r4, jnp.where(g == 2, r8, r16)))
    dp_ref[...] = (r[:tm] - dd[:tm]).astype(dp_ref.dtype)


def _pool_bwd(proj, dyps, wgrp, scale):
    t = proj.shape[0]
    c = wgrp.shape[1]
    tm = _row_tile(t)
    nt = t // tm
    hb = tm // POOL_HALO
    last = t // POOL_HALO - 1
    main, halo = _pool_specs(t, c, tm)
    halo_after = pl.BlockSpec((POOL_HALO, c), lambda g, i: (jnp.minimum((i + 1) * hb, last), g))
    return _call(
        functools.partial(_pool_bwd_body, tm=tm, nt=nt), name="pool_bwd",
        out_shape=(_sds((t, N_POOL_GROUPS * c), BF16), _sds((N_POOL_GROUPS, c, c), F32),
                   _sds((1, N_POOL_GROUPS * c), F32)),
        grid=(N_POOL_GROUPS, nt),
        in_specs=[main, halo, main, halo_after, pl.BlockSpec((None, c, c), lambda g, i: (g, 0, 0)),
                  pl.BlockSpec((1, c), lambda g, i: (0, g))],
        out_specs=(main, pl.BlockSpec((None, c, c), lambda g, i: (g, 0, 0)),
                   pl.BlockSpec((1, c), lambda g, i: (0, g))))(proj, proj, dyps, dyps, wgrp, scale)


def _sgu_bwd_body(u_ref, v_ref, dq_ref, gain_ref, ws_ref, b_ref, duv_ref, dws_ref, db_ref, dgain_ref,
                  dvn_ref, *, tm, heads):
    i = pl.program_id(0)

    @pl.when(i == 0)
    def _():
        dws_ref[...] = jnp.zeros_like(dws_ref)
        db_ref[...] = jnp.zeros_like(db_ref)
        dgain_ref[...] = jnp.zeros_like(dgain_ref)

    sw = heads * SGU_HEAD
    u = u_ref[...].astype(F32)
    v = v_ref[...].astype(F32)
    ug = _gelu(u)
    vg = _gelu(v)
    rstd = _rstd(vg)
    vgn = vg * rstd
    gain = gain_ref[...]
    vn = (vgn * gain).astype(BF16)
    dq = dq_ref[...].astype(F32)
    for h in range(heads):
        wt = _tril_weights(ws_ref, h).astype(BF16)
        cols = slice(h * SGU_HEAD, (h + 1) * SGU_HEAD)
        dws = jnp.zeros((SGU_CHUNK, SGU_CHUNK), F32)
        dbh = jnp.zeros((SGU_CHUNK, 1), F32)
        for c in range(tm // SGU_CHUNK):
            rows = slice(c * SGU_CHUNK, (c + 1) * SGU_CHUNK)
            vn_c = vn[rows, cols]
            sg = _dot(wt, vn_c) + b_ref[h]
            dq_c = dq[rows, cols]
            dsg = dq_c * ug[rows, cols]
            duv_ref[rows, cols] = (dq_c * sg * _gelu_grad(u[rows, cols])).astype(duv_ref.dtype)
            dsg_b = dsg.astype(BF16)
            dws = dws + _dot_nt(dsg_b, vn_c)
            dbh = dbh + jnp.sum(dsg, axis=1, keepdims=True)
            dvn_ref[rows, cols] = _dot_tn(wt, dsg_b)
        dws_ref[h] += jnp.where(_tril_mask(), dws, 0.0)
        db_ref[h] += dbh
    dvn = dvn_ref[...]
    dgain_ref[...] += jnp.sum(dvn * vgn, axis=0, keepdims=True)
    dvgn = dvn * gain
    dvg = rstd * (dvgn - vgn * jnp.mean(dvgn * vgn, axis=-1, keepdims=True))
    duv_ref[:, sw:] = (dvg * _gelu_grad(v)).astype(duv_ref.dtype)


def _sgu_bwd(proj, pw, dq, gain, ws, b3):
    t = proj.shape[0]
    sw = gain.shape[1]
    heads = sw // SGU_HEAD
    tm = _row_tile(t)
    ub = pw // sw
    wblk = pl.BlockSpec((heads, SGU_CHUNK, SGU_CHUNK), lambda i: (0, 0, 0))
    bblk = pl.BlockSpec((heads, SGU_CHUNK, 1), lambda i: (0, 0, 0))
    vec = pl.BlockSpec((1, sw), lambda i: (0, 0))
    return _call(
        functools.partial(_sgu_bwd_body, tm=tm, heads=heads), name="sgu_bwd",
        out_shape=(_sds((t, 2 * sw), BF16), _sds((heads, SGU_CHUNK, SGU_CHUNK), F32),
                   _sds((heads, SGU_CHUNK, 1), F32), _sds((1, sw), F32)),
        grid=(t // tm,),
        in_specs=[pl.BlockSpec((tm, sw), lambda i: (i, ub)),
                  pl.BlockSpec((tm, sw), lambda i: (i, ub + 1)),
                  pl.BlockSpec((tm, sw), lambda i: (i, 0)), vec, wblk, bblk],
        out_specs=(pl.BlockSpec((tm, 2 * sw), lambda i: (i, 0)), wblk, bblk, vec),
        scratch_shapes=[pltpu.VMEM((tm, sw), F32)])(proj, proj, dq, gain, ws, b3)


def _cast_slot_body(me_ref, x_ref, o_ref):
    o_ref[...] = x_ref[...].astype(o_ref.dtype)


def _cast_to_slot(w, layer, me, dtype=BF16):
    _, r, c = w.shape
    tr = _slab_rows(r, 1024)
    spec = pltpu.PrefetchScalarGridSpec(
        num_scalar_prefetch=1, grid=(r // tr,),
        in_specs=[pl.BlockSpec((None, tr, c), lambda i, me_ref: (layer, i, 0))],
        out_specs=pl.BlockSpec((None, tr, c), lambda i, me_ref: (me_ref[0], i, 0)))
    return _call(functools.partial(_cast_slot_body), name="cast_to_slot", out_shape=_sds((N_DEV, r, c), dtype),
                 grid_spec=spec)(me, w)


def _mesh_position():
    x, y, c = (lax.axis_index(a) for a in MESH_AXES)
    return x, y, c


def _remote(src, dst, send, recv, device):
    return pltpu.make_async_remote_copy(src_ref=src, dst_ref=dst, send_sem=send, recv_sem=recv,
                                        device_id=device, device_id_type=pl.DeviceIdType.MESH)


def _other_chips(x, y):
    return [(1 - x, y), (x, 1 - y), (1 - x, 1 - y)]


def _gather_send_own(bufs, send, recv):
    x, y, c = _mesh_position()
    me = 4 * x + 2 * y + c
    targets = [(x, y, 1 - c)] + [(px, py, c) for px, py in _other_chips(x, y)]
    for k, buf in enumerate(bufs):
        for j, dev in enumerate(targets):
            _remote(buf.at[me], buf.at[me], send.at[4 * k + j], recv.at[4 * k + j], dev).start()


def _gather_await_own(bufs, send, recv):
    x, y, c = _mesh_position()
    me = 4 * x + 2 * y + c
    targets = [(x, y, 1 - c)] + [(px, py, c) for px, py in _other_chips(x, y)]
    for k, buf in enumerate(bufs):
        for j, dev in enumerate(targets):
            cp = _remote(buf.at[me], buf.at[4 * dev[0] + 2 * dev[1] + dev[2]], send.at[4 * k + j],
                         recv.at[4 * k + j], dev)
            cp.wait_recv()
            cp.wait_send()


def _gather_relay_send(bufs, send, recv):
    x, y, c = _mesh_position()
    for k, buf in enumerate(bufs):
        for j, (px, py) in enumerate(_other_chips(x, y)):
            blk = buf.at[4 * px + 2 * py + c]
            _remote(blk, blk, send.at[3 * k + j], recv.at[3 * k + j], (x, y, 1 - c)).start()


def _gather_relay_await(bufs, send, recv):
    x, y, c = _mesh_position()
    for k, buf in enumerate(bufs):
        for j, (px, py) in enumerate(_other_chips(x, y)):
            cp = _remote(buf.at[4 * px + 2 * py + c], buf.at[4 * px + 2 * py + 1 - c], send.at[3 * k + j],
                         recv.at[3 * k + j], (x, y, 1 - c))
            cp.wait_recv()
            cp.wait_send()


def _reduce_pair_send(grads, stage, send, recv):
    x, y, c = _mesh_position()
    for k, (g, s) in enumerate(zip(grads, stage)):
        for q in range(4):
            _remote(g.at[2 * q + 1 - c], s.at[q], send.at[4 * k + q], recv.at[4 * k + q], (x, y, 1 - c)).start()


def _reduce_pair_await(grads, stage, send, recv):
    x, y, c = _mesh_position()
    for k, (g, s) in enumerate(zip(grads, stage)):
        for q in range(4):
            cp = _remote(g.at[2 * q + 1 - c], s.at[q], send.at[4 * k + q], recv.at[4 * k + q], (x, y, 1 - c))
            cp.wait_recv()
            cp.wait_send()


def _reduce_chip_send(sums, stage, send, recv):
    x, y, c = _mesh_position()
    for k, (p, s) in enumerate(zip(sums, stage)):
        for j, (px, py) in enumerate(_other_chips(x, y)):
            _remote(p.at[2 * px + py], s.at[2 * x + y], send.at[3 * k + j], recv.at[3 * k + j],
                    (px, py, c)).start()


def _reduce_chip_await(sums, stage, send, recv):
    x, y, c = _mesh_position()
    for k, (p, s) in enumerate(zip(sums, stage)):
        for j, (px, py) in enumerate(_other_chips(x, y)):
            cp = _remote(p.at[2 * px + py], s.at[2 * px + py], send.at[3 * k + j], recv.at[3 * k + j],
                         (px, py, c))
            cp.wait_recv()
            cp.wait_send()


def _split_body(*refs, n, n_sem_in, n_after, n_sem_out, steps, token):
    bufs = refs[:n]
    sems = refs[n:n + n_sem_in] + refs[n + n_sem_in + n_after:n + n_sem_in + n_after + n_sem_out]
    for step, lo in steps:
        step(bufs, sems[lo], sems[lo + 1])
    if token:
        refs[-1][...] = jnp.zeros_like(refs[-1])


def _split_call(steps, *, name, bufs, sems_in=(), sems_out=(), after=(), token=True):
    n = len(bufs)
    operands = [pltpu.with_memory_space_constraint(b, pltpu.HBM) for b in bufs] + list(sems_in) + list(after)
    in_specs = [HBM_SPEC] * n + [SEM_SPEC] * len(sems_in) + [ANY_SPEC] * len(after)
    out_shape = list(sems_out) + [pltpu.HBM(b.shape, b.dtype) for b in bufs]
    out_specs = [SEM_SPEC] * len(sems_out) + [HBM_SPEC] * n
    if token:
        out_shape.append(_sds((8, 128), F32))
        out_specs.append(pl.BlockSpec(memory_space=pltpu.VMEM))
    body = functools.partial(_split_body, n=n, n_sem_in=len(sems_in), n_after=len(after),
                             n_sem_out=len(sems_out), steps=tuple(steps), token=token)
    return pl.pallas_call(
        body, name=name, in_specs=in_specs, out_specs=tuple(out_specs), out_shape=tuple(out_shape),
        input_output_aliases={i: len(sems_out) + i for i in range(n)},
        compiler_params=pltpu.CompilerParams(
            has_side_effects=pltpu.SideEffectType.DATAFLOW_SIDE_EFFECTING))(*operands)


def _sem_pair(n, m):
    return [pltpu.SemaphoreType.DMA((n * m,)), pltpu.SemaphoreType.DMA((n * m,))]


def _gather_start(bufs, after, tag):
    n = len(bufs)
    out = _split_call([(_gather_send_own, 0)], name="gather_start_" + tag, bufs=bufs,
                      sems_out=_sem_pair(n, 4), after=after)
    return out[0], out[1], list(out[2:2 + n]), out[-1]


def _gather_relay(state, after, tag):
    send, recv, bufs, _ = state
    n = len(bufs)
    out = _split_call([(_gather_await_own, 0), (_gather_relay_send, 2)], name="gather_relay_" + tag, bufs=bufs,
                      sems_in=[send, recv], sems_out=_sem_pair(n, 3), after=after)
    return out[0], out[1], list(out[2:2 + n]), out[-1]


def _gather_finish(state, after, tag):
    send, recv, bufs, _ = state
    out = _split_call([(_gather_relay_await, 0)], name="gather_finish_" + tag, bufs=bufs,
                      sems_in=[send, recv], after=after, token=False)
    return list(out)


def _pair_arrays(n):
    def split(step):
        return lambda bufs, send, recv: step(bufs[:n], bufs[n:], send, recv)
    return split


def _reduce_pairs_start(grads, after, tag):
    n = len(grads)
    stage = [lax.empty((4,) + g.shape[1:], g.dtype) for g in grads]
    out = _split_call([(_pair_arrays(n)(_reduce_pair_send), 0)], name="reduce_pairs_start_" + tag,
                      bufs=list(grads) + stage, sems_out=_sem_pair(n, 4), after=after)
    return out[0], out[1], list(out[2:2 + n]), list(out[2 + n:2 + 2 * n]), out[-1]


def _reduce_pairs_finish(state, after, tag):
    send, recv, grads, stage, _ = state
    n = len(grads)
    out = _split_call([(_pair_arrays(n)(_reduce_pair_await), 0)], name="reduce_pairs_finish_" + tag,
                      bufs=list(grads) + list(stage), sems_in=[send, recv], after=after, token=False)
    return list(out[:n]), list(out[n:])


def _reduce_chips_start(sums, after, tag):
    n = len(sums)
    stage = [lax.empty(p.shape, p.dtype) for p in sums]
    out = _split_call([(_pair_arrays(n)(_reduce_chip_send), 0)], name="reduce_chips_start_" + tag,
                      bufs=list(sums) + stage, sems_out=_sem_pair(n, 3), after=after)
    return out[0], out[1], list(out[2:2 + n]), list(out[2 + n:2 + 2 * n]), out[-1]


def _reduce_chips_finish(state, after, tag):
    send, recv, sums, stage, _ = state
    n = len(sums)
    out = _split_call([(_pair_arrays(n)(_reduce_chip_await), 0)], name="reduce_chips_finish_" + tag,
                      bufs=list(sums) + list(stage), sems_in=[send, recv], after=after, token=False)
    return list(out[:n]), list(out[n:])


def _pair_sum_body(c_ref, a_ref, b_ref, o_ref):
    o_ref[...] = (a_ref[...].astype(F32) + b_ref[...].astype(F32)).astype(o_ref.dtype)


def _pair_sum(grad, stage, core):
    _, r, c = grad.shape
    tr = _slab_rows(r, 1024)
    blk = pl.BlockSpec((None, tr, c), lambda q, i, c_ref: (q, i, 0))
    spec = pltpu.PrefetchScalarGridSpec(
        num_scalar_prefetch=1, grid=(4, r // tr),
        in_specs=[pl.BlockSpec((None, tr, c), lambda q, i, c_ref: (2 * q + c_ref[0], i, 0)), blk],
        out_specs=blk)
    return _call(functools.partial(_pair_sum_body), name="pair_sum", out_shape=_sds((4, r, c), grad.dtype),
                 grid_spec=spec)(core, grad, stage)


def _adamw_math(w, g, m, v):
    m = ADAM_B1 * m + (1.0 - ADAM_B1) * g
    v = ADAM_B2 * v + (1.0 - ADAM_B2) * (g * g)
    m_hat = m / (1.0 - ADAM_B1 ** ADAM_STEP)
    v_hat = v / (1.0 - ADAM_B2 ** ADAM_STEP)
    delta = -ADAM_LR * (m_hat / (jnp.sqrt(v_hat) + ADAM_EPS) + ADAM_WD * w)
    return delta, m, v


def _adamw_body(q_ref, own_ref, r_ref, w_ref, m_ref, v_ref, *rest):
    g_out, d_out, m_out, v_out = rest[-4:]
    q = q_ref[0]
    g = None
    for j in range(4):
        term = jnp.where(q == j, own_ref[...], r_ref[j]).astype(F32)
        g = term if g is None else g + term
    delta, m, v = _adamw_math(w_ref[...], g, m_ref[...], v_ref[...])
    g_out[...] = g
    d_out[...] = delta
    m_out[...] = m
    v_out[...] = v


def _adamw_layer(sums, recv, w, m, v, prev, layer, chip):
    l, r, c = w.shape
    tr = _slab_rows(r)
    lay = pl.BlockSpec((None, tr, c), lambda i, q_ref: (layer, i, 0))
    spec = pltpu.PrefetchScalarGridSpec(
        num_scalar_prefetch=1, grid=(r // tr,),
        in_specs=[pl.BlockSpec((None, tr, c), lambda i, q_ref: (q_ref[0], i, 0)),
                  pl.BlockSpec((4, tr, c), lambda i, q_ref: (0, i, 0)), lay, lay, lay] + [ANY_SPEC] * 4,
        out_specs=(lay, lay, lay, lay))
    out = _sds((l, r, c), F32)
    return _call(
        functools.partial(_adamw_body), name="adamw", out_shape=(out, out, out, out), grid_spec=spec,
        aliases={6: 0, 7: 1, 8: 2, 9: 3})(chip, sums, recv, w, m, v, *prev)


def _adamw_small_body(r_ref, w_ref, m_ref, v_ref, g_out, d_out, m_out, v_out):
    g = r_ref[0]
    for j in range(1, N_DEV):
        g = g + r_ref[j]
    delta, m, v = _adamw_math(w_ref[...], g, m_ref[...], v_ref[...])
    g_out[...] = g
    d_out[...] = delta
    m_out[...] = m
    v_out[...] = v


def _adamw_small(recv, w, m, v):
    r, c = w.shape
    tr = 8
    for cand in (1024, 512, 256, 128, 64, 32, 16, 8):
        if r % cand == 0:
            tr = cand
            break
    blk = pl.BlockSpec((tr, c), lambda i: (i, 0))
    out = _sds((r, c), F32)
    return _call(
        functools.partial(_adamw_small_body), name="adamw_small", out_shape=(out, out, out, out),
        grid=(r // tr,),
        in_specs=[pl.BlockSpec((N_DEV, tr, c), lambda i: (0, i, 0)), blk, blk, blk],
        out_specs=(blk, blk, blk, blk))(recv, w, m, v)


def _as3d(a):
    return a.reshape(a.shape[0], -1, a.shape[-1])


GROUPS = (("w_ffn1_up", "w_ffn1_down"),
          ("w_in", "pool_group_w", "w_pool_out", "w_sgu_out", "w_out"),
          ("w_ffn2_up", "w_ffn2_down"))


def _ffn_forward(x, h, g_pre, wu, wd, g_post, g_next, after=(), mid=None):
    if h is None:
        h, after = _rms_fwd(x, g_pre, after=after), ()
    gu, a = _ffn_up(h, wu, after=after)
    tokens = tuple(mid(a)) if mid else ()
    wd = wd() if callable(wd) else wd
    x_out, f, *h_next = _down_norm_res(a, wd, x, g_post, MACARON_WEIGHT, g_next, after=tokens)
    return x_out, (h_next[0] if h_next else None), (wd, (x, h, gu, a, f))


def _ffn_backward(dy, saved, g_pre, wu, wd, g_post, norm_in=None, norm_next=None, mid=None, early=None):
    x, h, gu, a, f = saved
    df, dg_post = norm_in if norm_in else _norm_res_bwd(dy, f, g_post, MACARON_WEIGHT)
    dz = _dgrad_ffn(df, wd, gu)
    tokens = tuple(mid(dz)) if mid else ()
    dwd = _wgrad_rows(a, df).reshape(N_DEV, -1, df.shape[1])
    dwu = _wgrad_cs(h, dz)
    tokens += tuple(early([dwu, dwd])) if early else (dwu, dwd)
    dy, dg_pre, *norm_out = _dgrad_cs_norm(dz, wu, x, g_pre, dy, norm_next, after=tokens)
    return dy, dg_pre, dg_post, (tuple(norm_out) if norm_out else None)


def kernel(x, g_ffn1_pre, w_ffn1_up, w_ffn1_down, g_ffn1_post, g_mix_pre, w_in, pool_group_w, pool_scale, w_pool_out, sgu_v_gain, sgu_w_s, sgu_b_s, w_sgu_out, w_out, g_mix_post, g_ffn2_pre, w_ffn2_up, w_ffn2_down, g_ffn2_post, loss_target, m_g_ffn1_pre, m_w_ffn1_up, m_w_ffn1_down, m_g_ffn1_post, m_g_mix_pre, m_w_in, m_pool_group_w, m_pool_scale, m_w_pool_out, m_sgu_v_gain, m_sgu_w_s, m_sgu_b_s, m_w_sgu_out, m_w_out, m_g_mix_post, m_g_ffn2_pre, m_w_ffn2_up, m_w_ffn2_down, m_g_ffn2_post, v_g_ffn1_pre, v_w_ffn1_up, v_w_ffn1_down, v_g_ffn1_post, v_g_mix_pre, v_w_in, v_pool_group_w, v_pool_scale, v_w_pool_out, v_sgu_v_gain, v_sgu_w_s, v_sgu_b_s, v_w_sgu_out, v_w_out, v_g_mix_post, v_g_ffn2_pre, v_w_ffn2_up, v_w_ffn2_down, v_g_ffn2_post):
    given = dict(locals())
    weights = {n: given[n] for n in ORDER}
    mom_m = {n: given["m_" + n] for n in ORDER}
    mom_v = {n: given["v_" + n] for n in ORDER}
    depth = g_ffn1_pre.shape[0]
    d_model = x.shape[-1]
    pool_w = pool_scale.shape[1]
    sgu_w = sgu_v_gain.shape[1]
    groups, grp_rows, grp_c = pool_group_w.shape[1:]
    gate_off = pool_w + 2 * sgu_w

    xs = x.reshape(x.shape[-2], d_model)
    target = loss_target.reshape(xs.shape)

    px, py, pc = _mesh_position()
    me = (4 * px + 2 * py + pc).astype(jnp.int32).reshape(1)
    core = pc.astype(jnp.int32).reshape(1)
    chip = (2 * px + py).astype(jnp.int32).reshape(1)

    n_groups = len(GROUPS) * depth
    ffn_gains = {0: ("g_ffn1_pre", "g_ffn1_post"), 2: ("g_ffn2_pre", "g_ffn2_post")}

    def start_gather(t, after):
        layer, gi = divmod(t, len(GROUPS))
        bufs = [_cast_to_slot(_as3d(weights[n]), layer, me) for n in GROUPS[gi]]
        return _gather_start(bufs, after, "g%d" % t)

    def vec(name, i):
        return weights[name][i].reshape(1, -1)

    pre_gains = ("g_ffn1_pre", "g_mix_pre", "g_ffn2_pre")

    def forward_group(t, cur, h, bufs, after, mid):
        layer, gi = divmod(t, len(GROUPS))
        gw = dict(zip(GROUPS[gi], bufs))
        nxt = divmod(t + 1, len(GROUPS))
        g_next = vec(pre_gains[nxt[1]], nxt[0]) if t + 1 < n_groups else None
        if gi != 1:
            pre, post = ffn_gains[gi]
            wu, wd = (gw[n] for n in GROUPS[gi])
            wd = wd if callable(wd) else wd.reshape(-1, d_model)
            cur, h_next, (wd, sv) = _ffn_forward(cur, h, vec(pre, layer), wu, wd, vec(post, layer), g_next,
                                                 after=after, mid=mid)
            return cur, h_next, ((wu, wd), sv)
        wgrp = gw["pool_group_w"].reshape(N_DEV, groups, grp_rows, grp_c).transpose(1, 0, 2, 3)
        wts = dict(win=gw["w_in"], wgrp=wgrp.reshape(groups, grp_c, grp_c), wpo=gw["w_pool_out"],
                   wso=gw["w_sgu_out"], wout=gw["w_out"].reshape(-1, d_model),
                   ws=weights["sgu_w_s"][layer], b3=weights["sgu_b_s"][layer][:, :, None])
        x1 = cur
        if h is None:
            h, after = _rms_fwd(x1, vec("g_mix_pre", layer), after=after), ()
        proj = _mm_cs(h, wts["win"], after=after)
        tokens = tuple(mid(proj))
        yps = _pool_fwd(proj, wts["wgrp"], vec("pool_scale", layer))
        q = _sgu_fwd(proj, pool_w, vec("sgu_v_gain", layer), wts["ws"], wts["b3"])
        m, parts = _mix_gate(yps, q, proj, gate_off, wts["wpo"], wts["wso"])
        cur, o, *h_next = _down_norm_res(m, wts["wout"], x1, vec("g_mix_post", layer), 1.0, g_next, after=tokens)
        return cur, (h_next[0] if h_next else None), (wts, (x1, h, proj, yps, q, parts, m, o))

    first = [_gather_start([_cast_to_slot(_as3d(weights[n]), 0, me)], (), "g0" + n[-2:]) for n in GROUPS[0]]
    states = {1: start_gather(1, (first[0][-1], first[1][-1]))}
    up0 = _gather_finish(_gather_relay(first[0], (states[1][-1],), "g0up"), (), "g0up")
    down0 = []
    bufs = [up0[0], lambda: down0[0][0].reshape(-1, d_model)]
    saved = []
    cur, h = xs, None
    for t in range(n_groups):
        after = []
        if t + 2 < n_groups:
            states[t + 2] = start_gather(t + 2, (bufs[0],))
            after.append(states[t + 2][-1])
        relayed = []

        def mid(result, t=t, relayed=relayed):
            if t == 0:
                down0.append(_gather_finish(_gather_relay(first[1], (result,), "g0wn"), (), "g0wn"))
            if t + 1 < n_groups:
                relayed.append(_gather_relay(states[t + 1], (result,), "g%d" % (t + 1)))
            return tuple(r[-1] for r in relayed)

        cur, h, sv = forward_group(t, cur, h, bufs, tuple(after), mid)
        saved.append(sv)
        if t + 1 < n_groups:
            bufs = _gather_finish(relayed[0], (cur,), "g%d" % (t + 1))

    dy, sq = _loss_grad(cur, target)
    loss = lax.psum(0.5 * sq[0, 0] / d_model, MESH_AXES)

    small_grads = {n: [None] * depth for n in SMALL}
    outs = {n: None for n in BIG}

    def chip_sums(pairs, after, tag):
        grads, stage = _reduce_pairs_finish(pairs, after, tag)
        sums = [_pair_sum(g, s, core) for g, s in zip(grads, stage)]
        return _reduce_chips_start(sums, (), tag)

    def update(chips, after, t):
        layer, gi = divmod(t, len(GROUPS))
        sums, recv = _reduce_chips_finish(chips, after, "g%d" % t)
        for n, p, r in zip(GROUPS[gi], sums, recv):
            w3, m3, v3 = _as3d(weights[n]), _as3d(mom_m[n]), _as3d(mom_v[n])
            prev = outs[n] if outs[n] is not None else [lax.empty(w3.shape, F32) for _ in range(4)]
            outs[n] = _adamw_layer(p, r, w3, m3, v3, prev, layer, chip)

    def sandwich_norm(t):
        layer, gi = divmod(t, len(GROUPS))
        sv = saved[t][1]
        if gi != 1:
            return sv[4], vec(ffn_gains[gi][1], layer), MACARON_WEIGHT
        return sv[7], vec("g_mix_post", layer), 1.0

    def backward_group(t, dy, sv, norm_in, mid, early):
        layer, gi = divmod(t, len(GROUPS))
        norm_next = sandwich_norm(t - 1) if t > 0 else None
        wts, sv = sv
        if gi != 1:
            pre, post = ffn_gains[gi]
            dy, dg, dgp, norm_out = _ffn_backward(dy, sv, vec(pre, layer), wts[0], wts[1], vec(post, layer),
                                                  norm_in=norm_in, norm_next=norm_next, mid=mid, early=early)
            small_grads[pre][layer], small_grads[post][layer] = dg, dgp
            return dy, norm_out
        x1, h2, proj, yps, q, parts, m, o = sv
        do, small_grads["g_mix_post"][layer] = norm_in if norm_in else _norm_res_bwd(
            dy, o, vec("g_mix_post", layer), 1.0)
        dya, dyb, dga, dgb = _dgrad_gate(do, wts["wout"], parts)
        tokens = tuple(mid(dya)) if mid else ()
        dwout = _wgrad_rows(m, do).reshape(N_DEV, -1, d_model)
        dwpo = _wgrad_cs_small(yps, dya)
        dwso = _wgrad_cs_small(q, dyb)
        dyps = _dgrad_cs(dya, wts["wpo"], after=tokens)
        dq = _dgrad_cs(dyb, wts["wso"])
        dp, dwgrp, small_grads["pool_scale"][layer] = _pool_bwd(proj, dyps, wts["wgrp"], vec("pool_scale", layer))
        duv, dws, db3, small_grads["sgu_v_gain"][layer] = _sgu_bwd(
            proj, pool_w, dq, vec("sgu_v_gain", layer), wts["ws"], wts["b3"])
        small_grads["sgu_w_s"][layer] = dws
        small_grads["sgu_b_s"][layer] = db3
        dproj = jnp.concatenate([dp, duv, dga, dgb], axis=1)[None]
        dwin = _wgrad_cs(h2, dproj)
        dwgrp8 = dwgrp.reshape(groups, N_DEV, grp_rows, grp_c).transpose(1, 0, 2, 3)
        dwgrp8 = dwgrp8.reshape(N_DEV, groups * grp_rows, grp_c).astype(BF16)
        dy, small_grads["g_mix_pre"][layer], *norm_out = _dgrad_cs_norm(
            dproj, wts["win"], x1, vec("g_mix_pre", layer), dy, norm_next,
            after=tuple(early([dwin, dwgrp8, dwpo, dwso, dwout])))
        return dy, (tuple(norm_out) if norm_out else None)

    pairs, norm_in = None, None
    for t in reversed(range(n_groups)):
        started, mine = [], []

        def mid(first, t=t, started=started):
            started.append(chip_sums(pairs, (first,), "g%d" % (t + 1)))
            return (started[0][-1],)

        def early(grads, t=t, mine=mine):
            mine.append(_reduce_pairs_start(grads, (), "g%d" % t))
            return (mine[0][-1],)

        dy, norm_in = backward_group(t, dy, saved[t], norm_in, mid if pairs else None, early)
        if started:
            update(started[0], (dy,), t + 1)
        pairs = mine[0]
    last = chip_sums(pairs, (), "g0")

    def pack(parts):
        return jnp.concatenate([p.reshape(depth, -1) for p in parts], axis=1).reshape(-1, 128)

    part = pack([jnp.stack(small_grads[n]) for n in SMALL])
    small = _gather_start([_cast_to_slot(part[None], 0, me, F32)], (last[-1],), "small")
    done = tuple(outs[n][0] for n in BIG if outs[n] is not None)
    all_parts = _gather_finish(_gather_relay(small, done, "small"), (), "small")[0]
    small_out = _adamw_small(all_parts, pack([weights[n] for n in SMALL]), pack([mom_m[n] for n in SMALL]),
                             pack([mom_v[n] for n in SMALL]))
    update(last, (small_out[0],), 0)

    results = {}
    sizes = [weights[n][0].size for n in SMALL]
    for kind, packed in zip(("grad", "delta", "new_m", "new_v"), small_out):
        flat = packed.reshape(depth, -1)
        off = 0
        for n, sz in zip(SMALL, sizes):
            results[(kind, n)] = flat[:, off:off + sz].reshape(weights[n].shape)
            off += sz
    for n in BIG:
        for kind, arr in zip(("grad", "delta", "new_m", "new_v"), outs[n]):
            results[(kind, n)] = arr.reshape(weights[n].shape)

    grad_x = dy.reshape(x.shape)
    return (loss, grad_x, *[results[("grad", n)] for n in ORDER], *[results[("delta", n)] for n in ORDER],
            *[results[("new_m", n)] for n in ORDER], *[results[("new_v", n)] for n in ORDER])
```

```python
import functools

import jax
import jax.numpy as jnp
from jax import lax
from jax.experimental import pallas as pl
from jax.experimental.pallas import tpu as pltpu

F32 = jnp.float32
BF16 = jnp.bfloat16

N_DEV = 8
MESH_AXES = ("x", "y", "c")

EPS = 1e-6
MACARON_WEIGHT = 0.5
N_POOL_GROUPS = 4
POOL_HALO = 16
SGU_HEAD = 128
SGU_CHUNK = 128
GELU_C = 0.7978845608028654
GELU_A = 0.044715

ADAM_LR = 0.001
ADAM_B1 = 0.9
ADAM_B2 = 0.999
ADAM_EPS = 1e-08
ADAM_WD = 0.01
ADAM_STEP = 10

ROW_TILE = 512
VMEM_LIMIT_BYTES = 60 * 1024 * 1024

BIG = ("w_ffn1_up", "w_ffn1_down", "w_in", "pool_group_w", "w_pool_out", "w_sgu_out", "w_out",
       "w_ffn2_up", "w_ffn2_down")
SMALL = ("g_ffn1_pre", "g_ffn1_post", "g_mix_pre", "pool_scale", "sgu_v_gain", "sgu_w_s", "sgu_b_s",
         "g_mix_post", "g_ffn2_pre", "g_ffn2_post")
ORDER = ("g_ffn1_pre", "w_ffn1_up", "w_ffn1_down", "g_ffn1_post", "g_mix_pre", "w_in", "pool_group_w",
         "pool_scale", "w_pool_out", "sgu_v_gain", "sgu_w_s", "sgu_b_s", "w_sgu_out", "w_out",
         "g_mix_post", "g_ffn2_pre", "w_ffn2_up", "w_ffn2_down", "g_ffn2_post")


ANY_SPEC = pl.BlockSpec(memory_space=pl.ANY)
HBM_SPEC = pl.BlockSpec(memory_space=pltpu.HBM)
SEM_SPEC = pl.BlockSpec(memory_space=pltpu.SEMAPHORE)


def _skip_refs(body, n_in, n_skip, *refs):
    return body(*refs[:n_in], *refs[n_in + n_skip:])


def _call(body, *, name, out_shape, grid=(), in_specs=None, out_specs=None, scratch_shapes=(),
          aliases=None, after=(), grid_spec=None):
    extra = {}
    if grid_spec is not None:
        extra["grid_spec"] = grid_spec
    else:
        extra.update(in_specs=list(in_specs) + [ANY_SPEC] * len(after), out_specs=out_specs,
                     scratch_shapes=list(scratch_shapes))
        if grid:
            extra["grid"] = grid
        if after:
            body = functools.partial(_skip_refs, body, len(in_specs), len(after))
    if aliases:
        extra["input_output_aliases"] = aliases
    call = pl.pallas_call(
        body, name=name, out_shape=out_shape,
        compiler_params=pltpu.CompilerParams(vmem_limit_bytes=VMEM_LIMIT_BYTES), **extra)
    if after:
        return lambda *args: call(*args, *after)
    return call


def _sds(shape, dtype):
    return jax.ShapeDtypeStruct(tuple(shape), dtype)


def _dot(a, b):
    return jnp.dot(a, b, preferred_element_type=F32)


def _dot_nt(a, b):
    return lax.dot_general(a, b, (((1,), (1,)), ((), ())), preferred_element_type=F32)


def _dot_tn(a, b):
    return lax.dot_general(a, b, (((0,), (0,)), ((), ())), preferred_element_type=F32)


def _sigmoid(x):
    return 1.0 / (1.0 + jnp.exp(-x))


def _gelu(x):
    return 0.5 * x * (1.0 + jnp.tanh(GELU_C * (x + GELU_A * x * x * x)))


def _gelu_grad(x):
    t = jnp.tanh(GELU_C * (x + GELU_A * x * x * x))
    return 0.5 * (1.0 + t) + 0.5 * x * (1.0 - t * t) * (GELU_C * (1.0 + 3.0 * GELU_A * x * x))


def _rstd(x):
    return lax.rsqrt(jnp.mean(x * x, axis=-1, keepdims=True) + EPS)


def _row_tile(t):
    return min(t, ROW_TILE)


def _big_row_tile(t):
    return 2 * ROW_TILE if t % (2 * ROW_TILE) == 0 else _row_tile(t)


def _slab_rows(r, most=256):
    for cand in range(min(most, r) // 16 * 16, 15, -16):
        if r % cand == 0:
            return cand
    return r


def _rms_fwd_body(x_ref, g_ref, h_ref):
    x = x_ref[...]
    h_ref[...] = ((x * _rstd(x)) * g_ref[...]).astype(h_ref.dtype)


def _rms_fwd(x, g, after=()):
    t, d = x.shape
    tm = _row_tile(t)
    return _call(
        functools.partial(_rms_fwd_body), name="rms_fwd", out_shape=_sds((t, d), BF16),
        grid=(t // tm,),
        in_specs=[pl.BlockSpec((tm, d), lambda i: (i, 0)), pl.BlockSpec((1, d), lambda i: (0, 0))],
        out_specs=pl.BlockSpec((tm, d), lambda i: (i, 0)), after=after)(x, g)


def _mm_cs_body(h_ref, w_ref, o_ref):
    o_ref[...] = _dot(h_ref[...], w_ref[...]).astype(o_ref.dtype)


def _mm_cs(h, wg, after=()):
    t, k = h.shape
    n = wg.shape[2]
    tm = _row_tile(t)
    return _call(
        functools.partial(_mm_cs_body), name="mm_cs", out_shape=_sds((t, N_DEV * n), BF16),
        grid=(N_DEV, t // tm),
        in_specs=[pl.BlockSpec((tm, k), lambda j, i: (i, 0)),
                  pl.BlockSpec((None, k, n), lambda j, i: (j, 0, 0))],
        out_specs=pl.BlockSpec((tm, n), lambda j, i: (i, j)), after=after)(h, wg)


def _ffn_up_body(h_ref, wg_ref, wu_ref, gu_ref, a_ref):
    h = h_ref[...]
    g = _dot(h, wg_ref[...])
    u = _dot(h, wu_ref[...])
    sig = _sigmoid(g)
    silu = g * sig
    gu_ref[0] = (u * (sig * (1.0 + g * (1.0 - sig)))).astype(gu_ref.dtype)
    gu_ref[1] = silu.astype(gu_ref.dtype)
    a_ref[...] = (silu * u).astype(a_ref.dtype)


def _ffn_up(h, wg, after=()):
    t, k = h.shape
    n = wg.shape[2]
    half = N_DEV // 2
    f = half * n
    tm = _row_tile(t)
    return _call(
        functools.partial(_ffn_up_body), name="ffn_up",
        out_shape=(_sds((2, t, f), BF16), _sds((t, f), BF16)),
        grid=(half, t // tm),
        in_specs=[pl.BlockSpec((tm, k), lambda j, i: (i, 0)),
                  pl.BlockSpec((None, k, n), lambda j, i: (j, 0, 0)),
                  pl.BlockSpec((None, k, n), lambda j, i: (j + N_DEV // 2, 0, 0))],
        out_specs=(pl.BlockSpec((2, tm, n), lambda j, i: (0, i, j)),
                   pl.BlockSpec((tm, n), lambda j, i: (i, j))), after=after)(h, wg, wg)


EPILOGUE_ROWS = 256


def _for_row_chunks(rows, step):
    chunk = min(rows, EPILOGUE_ROWS)

    def body(r, carry):
        step(pl.ds(pl.multiple_of(r * chunk, chunk), chunk))
        return carry

    lax.fori_loop(0, rows // chunk, body, 0)


def _down_body(*refs, scale, nk, tm, with_next):
    a_ref, w_ref, x_ref, g_ref = refs[:4]
    xo_ref, f_ref = refs[4 + with_next:6 + with_next]
    k = pl.program_id(1)

    @pl.when(k == 0)
    def _():
        xo_ref[...] = jnp.zeros_like(xo_ref)

    xo_ref[...] += _dot(a_ref[...], w_ref[...])

    @pl.when(k == nk - 1)
    def _():
        def finish(rows):
            f = xo_ref[rows, :]
            f_ref[rows, :] = f.astype(f_ref.dtype)
            xo = x_ref[rows, :] + scale * ((f * _rstd(f)) * g_ref[...])
            xo_ref[rows, :] = xo
            if with_next:
                refs[-1][rows, :] = ((xo * _rstd(xo)) * refs[4][...]).astype(refs[-1].dtype)

        _for_row_chunks(tm, finish)


def _down_norm_res(a, w, x, g, scale, g_next=None, after=()):
    t, kk = a.shape
    d = w.shape[1]
    tm = _big_row_tile(t)
    tk = next(c for c in (1408, 1024, 512, 256) if kk % c == 0)
    nk = kk // tk
    with_next = g_next is not None
    once = pl.Buffered(1)
    vec = pl.BlockSpec((1, d), lambda i, k: (0, 0))
    rows = pl.BlockSpec((tm, d), lambda i, k: (i, 0))
    rows_once = pl.BlockSpec((tm, d), lambda i, k: (i, 0), pipeline_mode=once)
    return _call(
        functools.partial(_down_body, scale=scale, nk=nk, tm=tm, with_next=with_next), name="down_norm_res",
        out_shape=(_sds((t, d), F32), _sds((t, d), BF16)) + ((_sds((t, d), BF16),) if with_next else ()),
        grid=(t // tm, nk),
        in_specs=[pl.BlockSpec((tm, tk), lambda i, k: (i, k)), pl.BlockSpec((tk, d), lambda i, k: (k, 0)),
                  rows_once, vec] + ([vec] if with_next else []),
        out_specs=(rows_once, rows) + ((rows_once,) if with_next else ()),
        after=after)(a, w, x, g, *((g_next,) if with_next else ()))


def _pool_inv_count(g, i, tm):
    t1 = i * tm + lax.broadcasted_iota(jnp.int32, (tm, 1), 0) + 1
    return 1.0 / jnp.minimum(t1, 2 << g).astype(F32)


def _pool_windows(e, g):
    s2 = e + pltpu.roll(e, 1, 0)
    s4 = s2 + pltpu.roll(s2, 2, 0)
    s8 = s4 + pltpu.roll(s4, 4, 0)
    s16 = s8 + pltpu.roll(s8, 8, 0)
    return jnp.where(g == 0, s2, jnp.where(g == 1, s4, jnp.where(g == 2, s8, s16)))


def _pool_delta(p_ref, ph_ref, g, i, tm):
    pm = p_ref[...].astype(F32)
    ph = jnp.where(i == 0, 0.0, ph_ref[...].astype(F32))
    e = jnp.concatenate([ph, pm], axis=0)
    s = _pool_windows(e, g)[POOL_HALO:]
    return s * _pool_inv_count(g, i, tm) - pm


def _pool_fwd_body(p_ref, ph_ref, w_ref, sc_ref, o_ref, *, tm):
    g = pl.program_id(0)
    i = pl.program_id(1)
    d = _pool_delta(p_ref, ph_ref, g, i, tm)
    o_ref[...] = (_dot(d.astype(BF16), w_ref[...]) * sc_ref[...]).astype(o_ref.dtype)


def _pool_specs(t, c, tm):
    hb = tm // POOL_HALO
    main = pl.BlockSpec((tm, c), lambda g, i: (i, g))
    halo_before = pl.BlockSpec((POOL_HALO, c), lambda g, i: (jnp.maximum(i * hb - 1, 0), g))
    return main, halo_before


def _pool_fwd(proj, wgrp, scale):
    t = proj.shape[0]
    c = wgrp.shape[1]
    tm = _row_tile(t)
    main, halo = _pool_specs(t, c, tm)
    return _call(
        functools.partial(_pool_fwd_body, tm=tm), name="pool_fwd",
        out_shape=_sds((t, N_POOL_GROUPS * c), BF16), grid=(N_POOL_GROUPS, t // tm),
        in_specs=[main, halo, pl.BlockSpec((None, c, c), lambda g, i: (g, 0, 0)),
                  pl.BlockSpec((1, c), lambda g, i: (0, g))],
        out_specs=pl.BlockSpec((tm, c), lambda g, i: (i, g)))(proj, proj, wgrp, scale)


def _tril_mask():
    r = lax.broadcasted_iota(jnp.int32, (SGU_CHUNK, SGU_CHUNK), 0)
    c = lax.broadcasted_iota(jnp.int32, (SGU_CHUNK, SGU_CHUNK), 1)
    return r >= c


def _tril_weights(ws_ref, h):
    return jnp.where(_tril_mask(), ws_ref[h], 0.0)


def _sgu_fwd_body(u_ref, v_ref, gain_ref, ws_ref, b_ref, q_ref, *, tm, heads):
    ug = _gelu(u_ref[...].astype(F32))
    vg = _gelu(v_ref[...].astype(F32))
    vn = ((vg * _rstd(vg)) * gain_ref[...]).astype(BF16)
    for h in range(heads):
        wt = _tril_weights(ws_ref, h).astype(BF16)
        cols = slice(h * SGU_HEAD, (h + 1) * SGU_HEAD)
        for c in range(tm // SGU_CHUNK):
            rows = slice(c * SGU_CHUNK, (c + 1) * SGU_CHUNK)
            sg = _dot(wt, vn[rows, cols]) + b_ref[h]
            q_ref[rows, cols] = (ug[rows, cols] * sg).astype(q_ref.dtype)


def _sgu_fwd(proj, pw, gain, ws, b3):
    t = proj.shape[0]
    sw = gain.shape[1]
    heads = sw // SGU_HEAD
    tm = _row_tile(t)
    ub = pw // sw
    return _call(
        functools.partial(_sgu_fwd_body, tm=tm, heads=heads), name="sgu_fwd",
        out_shape=_sds((t, sw), BF16), grid=(t // tm,),
        in_specs=[pl.BlockSpec((tm, sw), lambda i: (i, ub)),
                  pl.BlockSpec((tm, sw), lambda i: (i, ub + 1)),
                  pl.BlockSpec((1, sw), lambda i: (0, 0)),
                  pl.BlockSpec((heads, SGU_CHUNK, SGU_CHUNK), lambda i: (0, 0, 0)),
                  pl.BlockSpec((heads, SGU_CHUNK, 1), lambda i: (0, 0, 0))],
        out_specs=pl.BlockSpec((tm, sw), lambda i: (i, 0)))(proj, proj, gain, ws, b3)


def _mix_gate_body(*refs, n, n_gate):
    yps_ref, q_ref = refs[:2]
    ga_refs, gb_refs = refs[2:2 + n_gate], refs[2 + n_gate:2 + 2 * n_gate]
    wp_ref, ws_ref, m_ref, part_ref = refs[2 + 2 * n_gate:]
    per = N_DEV // n_gate
    yps = yps_ref[...]
    q = q_ref[...]
    for s in range(N_DEV):
        cols = slice(s * n, (s + 1) * n)
        within = slice((s % per) * n, (s % per + 1) * n)
        ya = _dot(yps, wp_ref[s])
        yb = _dot(q, ws_ref[s])
        sa = _sigmoid(ga_refs[s // per][:, within].astype(F32))
        sb = _sigmoid(gb_refs[s // per][:, within].astype(F32))
        m_ref[:, cols] = (sa * ya + sb * yb).astype(m_ref.dtype)
        part_ref[0, :, cols] = sa.astype(part_ref.dtype)
        part_ref[1, :, cols] = sb.astype(part_ref.dtype)
        part_ref[2, :, cols] = (ya * (sa * (1.0 - sa))).astype(part_ref.dtype)
        part_ref[3, :, cols] = (yb * (sb * (1.0 - sb))).astype(part_ref.dtype)


def _mix_gate(yps, q, proj, gate_off, wpo, wso):
    t, pw = yps.shape
    sw = q.shape[1]
    n = wpo.shape[2]
    d = N_DEV * n
    tm = _row_tile(t) // 2
    gw = d
    while gate_off % gw:
        gw //= 2
    n_gate = d // gw
    gates = [pl.BlockSpec((tm, gw), functools.partial(lambda i, b: (i, b), b=(gate_off + k * gw) // gw))
             for k in range(2 * n_gate)]
    whole = pl.Buffered(1)
    return _call(
        functools.partial(_mix_gate_body, n=n, n_gate=n_gate), name="mix_gate",
        out_shape=(_sds((t, d), BF16), _sds((4, t, d), BF16)), grid=(t // tm,),
        in_specs=[pl.BlockSpec((tm, pw), lambda i: (i, 0)), pl.BlockSpec((tm, sw), lambda i: (i, 0))] + gates
        + [pl.BlockSpec((N_DEV, pw, n), lambda i: (0, 0, 0), pipeline_mode=whole),
           pl.BlockSpec((N_DEV, sw, n), lambda i: (0, 0, 0), pipeline_mode=whole)],
        out_specs=(pl.BlockSpec((tm, d), lambda i: (i, 0)), pl.BlockSpec((4, tm, d), lambda i: (0, i, 0))))(
            yps, q, *([proj] * (2 * n_gate)), wpo, wso)


def _loss_body(y_ref, t_ref, dy_ref, l_ref, *, inv_d):
    i = pl.program_id(0)

    @pl.when(i == 0)
    def _():
        l_ref[...] = jnp.zeros_like(l_ref)

    e = y_ref[...] - t_ref[...]
    dy_ref[...] = e * inv_d
    l_ref[...] += jnp.sum(e * e)


def _loss_grad(y, target):
    t, d = y.shape
    tm = _row_tile(t)
    blk = pl.BlockSpec((tm, d), lambda i: (i, 0))
    return _call(
        functools.partial(_loss_body, inv_d=1.0 / d), name="loss_grad",
        out_shape=(_sds((t, d), F32), _sds((8, 128), F32)), grid=(t // tm,),
        in_specs=[blk, blk],
        out_specs=(blk, pl.BlockSpec((8, 128), lambda i: (0, 0))))(y, target)


def _norm_bwd_body(dy_ref, f_ref, g_ref, df_ref, dg_ref, *, scale):
    i = pl.program_id(0)

    @pl.when(i == 0)
    def _():
        dg_ref[...] = jnp.zeros_like(dg_ref)

    f = f_ref[...].astype(F32)
    rstd = _rstd(f)
    fn = f * rstd
    dr = scale * dy_ref[...]
    dg_ref[...] += jnp.sum(dr * fn, axis=0, keepdims=True)
    dfn = dr * g_ref[...]
    df = rstd * (dfn - fn * jnp.mean(dfn * fn, axis=-1, keepdims=True))
    df_ref[...] = df.astype(df_ref.dtype)


def _norm_res_bwd(dy, f, g, scale, after=()):
    t, d = dy.shape
    tm = _row_tile(t)
    blk = pl.BlockSpec((tm, d), lambda i: (i, 0))
    vec = pl.BlockSpec((1, d), lambda i: (0, 0))
    return _call(
        functools.partial(_norm_bwd_body, scale=scale), name="norm_res_bwd",
        out_shape=(_sds((t, d), BF16), _sds((1, d), F32)), grid=(t // tm,),
        in_specs=[blk, blk, vec], out_specs=(blk, vec), after=after)(dy, f, g)


def _dgrad_ffn_body(df_ref, w_ref, gu_ref, dz_ref):
    da = _dot_nt(df_ref[...], w_ref[...])
    dz_ref[0] = (da * gu_ref[0].astype(F32)).astype(dz_ref.dtype)
    dz_ref[1] = (da * gu_ref[1].astype(F32)).astype(dz_ref.dtype)


def _dgrad_ffn(df, wd, gu):
    t, d = df.shape
    f = wd.shape[0]
    tm = _row_tile(t)
    tn = f // 4
    blk = pl.BlockSpec((2, tm, tn), lambda j, i: (0, i, j))
    return _call(
        functools.partial(_dgrad_ffn_body), name="dgrad_ffn", out_shape=_sds((2, t, f), BF16),
        grid=(f // tn, t // tm),
        in_specs=[pl.BlockSpec((tm, d), lambda j, i: (i, 0)),
                  pl.BlockSpec((tn, d), lambda j, i: (j, 0)), blk],
        out_specs=blk)(df, wd, gu)


def _dgrad_gate_body(do_ref, w_ref, part_ref, dya_ref, dyb_ref, dga_ref, dgb_ref):
    dm = _dot_nt(do_ref[...], w_ref[...])
    for k, out in enumerate((dya_ref, dyb_ref, dga_ref, dgb_ref)):
        out[...] = (dm * part_ref[k].astype(F32)).astype(out.dtype)


def _dgrad_gate(do, wout, parts):
    t, d = do.shape
    tm = _row_tile(t)
    tn = 512 if d % 512 == 0 else 256
    blk = pl.BlockSpec((tm, tn), lambda j, i: (i, j))
    out = _sds((t, d), BF16)
    return _call(
        functools.partial(_dgrad_gate_body), name="dgrad_gate", out_shape=(out, out, out, out),
        grid=(d // tn, t // tm),
        in_specs=[pl.BlockSpec((tm, d), lambda j, i: (i, 0)),
                  pl.BlockSpec((tn, d), lambda j, i: (j, 0)),
                  pl.BlockSpec((4, tm, tn), lambda j, i: (0, i, j))],
        out_specs=(blk, blk, blk, blk))(do, wout, parts)


def _wgrad_rows_body(a_ref, b_ref, o_ref, acc_ref, *, nt):
    t = pl.program_id(1)

    @pl.when(t == 0)
    def _():
        acc_ref[...] = jnp.zeros_like(acc_ref)

    acc_ref[...] += _dot_tn(a_ref[...], b_ref[...])

    @pl.when(t == nt - 1)
    def _():
        o_ref[...] = acc_ref[...].astype(o_ref.dtype)


def _wgrad_rows(a, b, after=()):
    t, k = a.shape
    n = b.shape[1]
    tt = _big_row_tile(t)
    tk = k // 4
    nt = t // tt
    return _call(
        functools.partial(_wgrad_rows_body, nt=nt), name="wgrad_rows", out_shape=_sds((k, n), BF16),
        grid=(k // tk, nt),
        in_specs=[pl.BlockSpec((tt, tk), lambda r, s: (s, r)),
                  pl.BlockSpec((tt, n), lambda r, s: (s, 0))],
        out_specs=pl.BlockSpec((tk, n), lambda r, s: (r, 0)),
        scratch_shapes=[pltpu.VMEM((tk, n), F32)], after=after)(a, b)


def _wgrad_cs(a, b3):
    t, k = a.shape
    parts = b3.shape[0]
    per = N_DEV // parts
    n = b3.shape[2] // per
    tt = _big_row_tile(t)
    nt = t // tt
    return _call(
        functools.partial(_wgrad_rows_body, nt=nt), name="wgrad_cs", out_shape=_sds((N_DEV, k, n), BF16),
        grid=(N_DEV, nt),
        in_specs=[pl.BlockSpec((tt, k), lambda r, s: (s, 0)),
                  pl.BlockSpec((None, tt, n), lambda r, s: (r // per, s, r % per))],
        out_specs=pl.BlockSpec((None, k, n), lambda r, s: (r, 0, 0)),
        scratch_shapes=[pltpu.VMEM((k, n), F32)])(a, b3)


def _wgrad_small_body(a_ref, b_ref, o_ref, acc_ref, *, nt, n):
    t = pl.program_id(0)

    @pl.when(t == 0)
    def _():
        acc_ref[...] = jnp.zeros_like(acc_ref)

    a = a_ref[...]
    for s in range(N_DEV):
        acc_ref[s] += _dot_tn(a, b_ref[:, s * n:(s + 1) * n])

    @pl.when(t == nt - 1)
    def _():
        o_ref[...] = acc_ref[...].astype(o_ref.dtype)


def _wgrad_cs_small(a, b):
    t, k = a.shape
    d = b.shape[1]
    n = d // N_DEV
    tt = _big_row_tile(t)
    nt = t // tt
    return _call(
        functools.partial(_wgrad_small_body, nt=nt, n=n), name="wgrad_cs_small",
        out_shape=_sds((N_DEV, k, n), BF16), grid=(nt,),
        in_specs=[pl.BlockSpec((tt, k), lambda s: (s, 0)), pl.BlockSpec((tt, d), lambda s: (s, 0))],
        out_specs=pl.BlockSpec((N_DEV, k, n), lambda s: (0, 0, 0)),
        scratch_shapes=[pltpu.VMEM((N_DEV, k, n), F32)])(a, b)


def _dgrad_cs_body(dz_ref, w_ref, o_ref, *, n):
    acc = None
    for s in range(N_DEV):
        term = _dot_nt(dz_ref[:, s * n:(s + 1) * n], w_ref[s])
        acc = term if acc is None else acc + term
    o_ref[...] = acc.astype(o_ref.dtype)


def _dgrad_cs(dz, wg, after=()):
    t, d = dz.shape
    k, n = wg.shape[1], wg.shape[2]
    tm = _row_tile(t)
    return _call(
        functools.partial(_dgrad_cs_body, n=n), name="dgrad_cs", out_shape=_sds((t, k), BF16),
        grid=(t // tm,),
        in_specs=[pl.BlockSpec((tm, d), lambda i: (i, 0)),
                  pl.BlockSpec((N_DEV, k, n), lambda i: (0, 0, 0))],
        out_specs=pl.BlockSpec((tm, k), lambda i: (i, 0)), after=after)(dz, wg)


def _dgrad_cs_norm_body(*refs, tm, scale_next):
    fused = scale_next is not None
    dz_ref, w_ref, x_hbm, g_ref, dy_hbm = refs[:5]
    f_hbm, gp_ref = refs[5:7] if fused else (None, None)
    o_ref, dg_ref = refs[5 + 2 * fused:7 + 2 * fused]
    df_ref, dgp_ref = refs[9:11] if fused else (None, None)
    x_ref, dy_ref = refs[7 + 4 * fused:9 + 4 * fused]
    f_ref = refs[13] if fused else None
    sem = refs[-1]
    i = pl.program_id(0)
    s = pl.program_id(1)
    tile = pl.ds(pl.multiple_of(i * tm, tm), tm)

    def epilogue_operands():
        copies = [pltpu.make_async_copy(x_hbm.at[tile], x_ref, sem.at[0]),
                  pltpu.make_async_copy(dy_hbm.at[tile], dy_ref, sem.at[1])]
        if fused:
            copies.append(pltpu.make_async_copy(f_hbm.at[tile], f_ref, sem.at[2]))
        return copies

    @pl.when(jnp.logical_and(i == 0, s == 0))
    def _():
        dg_ref[...] = jnp.zeros_like(dg_ref)
        if fused:
            dgp_ref[...] = jnp.zeros_like(dgp_ref)

    @pl.when(s == 0)
    def _():
        o_ref[...] = jnp.zeros_like(o_ref)
        for cp in epilogue_operands():
            cp.start()

    o_ref[...] += _dot_nt(dz_ref[...], w_ref[...])

    @pl.when(s == N_DEV - 1)
    def _():
        for cp in epilogue_operands():
            cp.wait()

        def finish(rows):
            dh = o_ref[rows, :]
            x = x_ref[rows, :]
            rstd = _rstd(x)
            xn = x * rstd
            dg_ref[...] += jnp.sum(dh * xn, axis=0, keepdims=True)
            dxn = dh * g_ref[...]
            dy = dy_ref[rows, :] + rstd * (dxn - xn * jnp.mean(dxn * xn, axis=-1, keepdims=True))
            o_ref[rows, :] = dy
            if fused:
                f = f_ref[rows, :].astype(F32)
                rstd_f = _rstd(f)
                fn = f * rstd_f
                dr = scale_next * dy
                dgp_ref[...] += jnp.sum(dr * fn, axis=0, keepdims=True)
                dfn = dr * gp_ref[...]
                df = rstd_f * (dfn - fn * jnp.mean(dfn * fn, axis=-1, keepdims=True))
                df_ref[rows, :] = df.astype(df_ref.dtype)

        _for_row_chunks(tm, finish)


def _dgrad_cs_norm(dz3, wg, x, g, dy, norm_next=None, after=()):
    parts, t, _ = dz3.shape
    per = N_DEV // parts
    d, n = wg.shape[1], wg.shape[2]
    tm = _big_row_tile(t)
    blk = pl.BlockSpec((tm, d), lambda i, s: (i, 0), pipeline_mode=pl.Buffered(1))
    vec = pl.BlockSpec((1, d), lambda i, s: (0, 0))
    fused = norm_next is not None
    f, gain, scale = norm_next if fused else (None, None, None)
    return _call(
        functools.partial(_dgrad_cs_norm_body, tm=tm, scale_next=scale), name="dgrad_cs_norm",
        out_shape=(_sds((t, d), F32), _sds((1, d), F32)) + ((_sds((t, d), BF16), _sds((1, d), F32)) if fused else ()),
        grid=(t // tm, N_DEV),
        in_specs=[pl.BlockSpec((None, tm, n), lambda i, s: (s // per, i, s % per)),
                  pl.BlockSpec((None, d, n), lambda i, s: (s, 0, 0)), ANY_SPEC, vec, ANY_SPEC]
        + ([ANY_SPEC, vec] if fused else []),
        out_specs=(blk, vec) + ((blk, vec) if fused else ()),
        scratch_shapes=[pltpu.VMEM((tm, d), F32), pltpu.VMEM((tm, d), F32)]
        + ([pltpu.VMEM((tm, d), BF16)] if fused else []) + [pltpu.SemaphoreType.DMA((3,))],
        after=after)(dz3, wg, x, g, dy, *((f, gain) if fused else ()))


def _pool_bwd_body(p_ref, ph_ref, dy_ref, dyh_ref, w_ref, sc_ref, dp_ref, dw_ref, dsc_ref, *, tm, nt):
    g = pl.program_id(0)
    i = pl.program_id(1)

    @pl.when(i == 0)
    def _():
        dw_ref[...] = jnp.zeros_like(dw_ref)
        dsc_ref[...] = jnp.zeros_like(dsc_ref)

    w = w_ref[...]
    sc = sc_ref[...]
    d = _pool_delta(p_ref, ph_ref, g, i, tm).astype(BF16)
    dyps = dy_ref[...].astype(F32)
    dsc_ref[...] += jnp.sum(dyps * _dot(d, w), axis=0, keepdims=True)
    dyp = (dyps * sc).astype(BF16)
    dw_ref[...] += _dot_tn(d, dyp)
    dyp_after = (jnp.where(i == nt - 1, 0.0, dyh_ref[...].astype(F32)) * sc).astype(BF16)
    dd = jnp.concatenate([_dot_nt(dyp, w), _dot_nt(dyp_after, w)], axis=0)
    rows = tm + POOL_HALO
    t1 = i * tm + lax.broadcasted_iota(jnp.int32, (rows, 1), 0) + 1
    e = dd / jnp.minimum(t1, 2 << g).astype(F32)
    r2 = e + pltpu.roll(e, rows - 1, 0)
    r4 = r2 + pltpu.roll(r2, rows - 2, 0)
    r8 = r4 + pltpu.roll(r4, rows - 4, 0)
    r16 = r8 + pltpu.roll(r8, rows - 8, 0)
    r = jnp.where(g == 0, r2, jnp.where(g == 1, r4, jnp.where(g == 2, r8, r16)))
    dp_ref[...] = (r[:tm] - dd[:tm]).astype(dp_ref.dtype)


def _pool_bwd(proj, dyps, wgrp, scale):
    t = proj.shape[0]
    c = wgrp.shape[1]
    tm = _row_tile(t)
    nt = t // tm
    hb = tm // POOL_HALO
    last = t // POOL_HALO - 1
    main, halo = _pool_specs(t, c, tm)
    halo_after = pl.BlockSpec((POOL_HALO, c), lambda g, i: (jnp.minimum((i + 1) * hb, last), g))
    return _call(
        functools.partial(_pool_bwd_body, tm=tm, nt=nt), name="pool_bwd",
        out_shape=(_sds((t, N_POOL_GROUPS * c), BF16), _sds((N_POOL_GROUPS, c, c), F32),
                   _sds((1, N_POOL_GROUPS * c), F32)),
        grid=(N_POOL_GROUPS, nt),
        in_specs=[main, halo, main, halo_after, pl.BlockSpec((None, c, c), lambda g, i: (g, 0, 0)),
                  pl.BlockSpec((1, c), lambda g, i: (0, g))],
        out_specs=(main, pl.BlockSpec((None, c, c), lambda g, i: (g, 0, 0)),
                   pl.BlockSpec((1, c), lambda g, i: (0, g))))(proj, proj, dyps, dyps, wgrp, scale)


def _sgu_bwd_body(u_ref, v_ref, dq_ref, gain_ref, ws_ref, b_ref, duv_ref, dws_ref, db_ref, dgain_ref,
                  dvn_ref, *, tm, heads):
    i = pl.program_id(0)

    @pl.when(i == 0)
    def _():
        dws_ref[...] = jnp.zeros_like(dws_ref)
        db_ref[...] = jnp.zeros_like(db_ref)
        dgain_ref[...] = jnp.zeros_like(dgain_ref)

    sw = heads * SGU_HEAD
    u = u_ref[...].astype(F32)
    v = v_ref[...].astype(F32)
    ug = _gelu(u)
    vg = _gelu(v)
    rstd = _rstd(vg)
    vgn = vg * rstd
    gain = gain_ref[...]
    vn = (vgn * gain).astype(BF16)
    dq = dq_ref[...].astype(F32)
    for h in range(heads):
        wt = _tril_weights(ws_ref, h).astype(BF16)
        cols = slice(h * SGU_HEAD, (h + 1) * SGU_HEAD)
        dws = jnp.zeros((SGU_CHUNK, SGU_CHUNK), F32)
        dbh = jnp.zeros((SGU_CHUNK, 1), F32)
        for c in range(tm // SGU_CHUNK):
            rows = slice(c * SGU_CHUNK, (c + 1) * SGU_CHUNK)
            vn_c = vn[rows, cols]
            sg = _dot(wt, vn_c) + b_ref[h]
            dq_c = dq[rows, cols]
            dsg = dq_c * ug[rows, cols]
            duv_ref[rows, cols] = (dq_c * sg * _gelu_grad(u[rows, cols])).astype(duv_ref.dtype)
            dsg_b = dsg.astype(BF16)
            dws = dws + _dot_nt(dsg_b, vn_c)
            dbh = dbh + jnp.sum(dsg, axis=1, keepdims=True)
            dvn_ref[rows, cols] = _dot_tn(wt, dsg_b)
        dws_ref[h] += jnp.where(_tril_mask(), dws, 0.0)
        db_ref[h] += dbh
    dvn = dvn_ref[...]
    dgain_ref[...] += jnp.sum(dvn * vgn, axis=0, keepdims=True)
    dvgn = dvn * gain
    dvg = rstd * (dvgn - vgn * jnp.mean(dvgn * vgn, axis=-1, keepdims=True))
    duv_ref[:, sw:] = (dvg * _gelu_grad(v)).astype(duv_ref.dtype)


def _sgu_bwd(proj, pw, dq, gain, ws, b3):
    t = proj.shape[0]
    sw = gain.shape[1]
    heads = sw // SGU_HEAD
    tm = _row_tile(t)
    ub = pw // sw
    wblk = pl.BlockSpec((heads, SGU_CHUNK, SGU_CHUNK), lambda i: (0, 0, 0))
    bblk = pl.BlockSpec((heads, SGU_CHUNK, 1), lambda i: (0, 0, 0))
    vec = pl.BlockSpec((1, sw), lambda i: (0, 0))
    return _call(
        functools.partial(_sgu_bwd_body, tm=tm, heads=heads), name="sgu_bwd",
        out_shape=(_sds((t, 2 * sw), BF16), _sds((heads, SGU_CHUNK, SGU_CHUNK), F32),
                   _sds((heads, SGU_CHUNK, 1), F32), _sds((1, sw), F32)),
        grid=(t // tm,),
        in_specs=[pl.BlockSpec((tm, sw), lambda i: (i, ub)),
                  pl.BlockSpec((tm, sw), lambda i: (i, ub + 1)),
                  pl.BlockSpec((tm, sw), lambda i: (i, 0)), vec, wblk, bblk],
        out_specs=(pl.BlockSpec((tm, 2 * sw), lambda i: (i, 0)), wblk, bblk, vec),
        scratch_shapes=[pltpu.VMEM((tm, sw), F32)])(proj, proj, dq, gain, ws, b3)


def _cast_slot_body(me_ref, x_ref, o_ref):
    o_ref[...] = x_ref[...].astype(o_ref.dtype)


def _cast_to_slot(w, layer, me, dtype=BF16):
    _, r, c = w.shape
    tr = _slab_rows(r, 1024)
    spec = pltpu.PrefetchScalarGridSpec(
        num_scalar_prefetch=1, grid=(r // tr,),
        in_specs=[pl.BlockSpec((None, tr, c), lambda i, me_ref: (layer, i, 0))],
        out_specs=pl.BlockSpec((None, tr, c), lambda i, me_ref: (me_ref[0], i, 0)))
    return _call(functools.partial(_cast_slot_body), name="cast_to_slot", out_shape=_sds((N_DEV, r, c), dtype),
                 grid_spec=spec)(me, w)


def _mesh_position():
    x, y, c = (lax.axis_index(a) for a in MESH_AXES)
    return x, y, c


def _remote(src, dst, send, recv, device):
    return pltpu.make_async_remote_copy(src_ref=src, dst_ref=dst, send_sem=send, recv_sem=recv,
                                        device_id=device, device_id_type=pl.DeviceIdType.MESH)


def _other_chips(x, y):
    return [(1 - x, y), (x, 1 - y), (1 - x, 1 - y)]


def _gather_send_own(bufs, send, recv):
    x, y, c = _mesh_position()
    me = 4 * x + 2 * y + c
    targets = [(x, y, 1 - c)] + [(px, py, c) for px, py in _other_chips(x, y)]
    for k, buf in enumerate(bufs):
        for j, dev in enumerate(targets):
            _remote(buf.at[me], buf.at[me], send.at[4 * k + j], recv.at[4 * k + j], dev).start()


def _gather_await_own(bufs, send, recv):
    x, y, c = _mesh_position()
    me = 4 * x + 2 * y + c
    targets = [(x, y, 1 - c)] + [(px, py, c) for px, py in _other_chips(x, y)]
    for k, buf in enumerate(bufs):
        for j, dev in enumerate(targets):
            cp = _remote(buf.at[me], buf.at[4 * dev[0] + 2 * dev[1] + dev[2]], send.at[4 * k + j],
                         recv.at[4 * k + j], dev)
            cp.wait_recv()
            cp.wait_send()


def _gather_relay_send(bufs, send, recv):
    x, y, c = _mesh_position()
    for k, buf in enumerate(bufs):
        for j, (px, py) in enumerate(_other_chips(x, y)):
            blk = buf.at[4 * px + 2 * py + c]
            _remote(blk, blk, send.at[3 * k + j], recv.at[3 * k + j], (x, y, 1 - c)).start()


def _gather_relay_await(bufs, send, recv):
    x, y, c = _mesh_position()
    for k, buf in enumerate(bufs):
        for j, (px, py) in enumerate(_other_chips(x, y)):
            cp = _remote(buf.at[4 * px + 2 * py + c], buf.at[4 * px + 2 * py + 1 - c], send.at[3 * k + j],
                         recv.at[3 * k + j], (x, y, 1 - c))
            cp.wait_recv()
            cp.wait_send()


def _reduce_pair_send(grads, stage, send, recv):
    x, y, c = _mesh_position()
    for k, (g, s) in enumerate(zip(grads, stage)):
        for q in range(4):
            _remote(g.at[2 * q + 1 - c], s.at[q], send.at[4 * k + q], recv.at[4 * k + q], (x, y, 1 - c)).start()


def _reduce_pair_await(grads, stage, send, recv):
    x, y, c = _mesh_position()
    for k, (g, s) in enumerate(zip(grads, stage)):
        for q in range(4):
            cp = _remote(g.at[2 * q + 1 - c], s.at[q], send.at[4 * k + q], recv.at[4 * k + q], (x, y, 1 - c))
            cp.wait_recv()
            cp.wait_send()


def _reduce_chip_send(sums, stage, send, recv):
    x, y, c = _mesh_position()
    for k, (p, s) in enumerate(zip(sums, stage)):
        for j, (px, py) in enumerate(_other_chips(x, y)):
            _remote(p.at[2 * px + py], s.at[2 * x + y], send.at[3 * k + j], recv.at[3 * k + j],
                    (px, py, c)).start()


def _reduce_chip_await(sums, stage, send, recv):
    x, y, c = _mesh_position()
    for k, (p, s) in enumerate(zip(sums, stage)):
        for j, (px, py) in enumerate(_other_chips(x, y)):
            cp = _remote(p.at[2 * px + py], s.at[2 * px + py], send.at[3 * k + j], recv.at[3 * k + j],
                         (px, py, c))
            cp.wait_recv()
            cp.wait_send()


def _split_body(*refs, n, n_sem_in, n_after, n_sem_out, steps, token):
    bufs = refs[:n]
    sems = refs[n:n + n_sem_in] + refs[n + n_sem_in + n_after:n + n_sem_in + n_after + n_sem_out]
    for step, lo in steps:
        step(bufs, sems[lo], sems[lo + 1])
    if token:
        refs[-1][...] = jnp.zeros_like(refs[-1])


def _split_call(steps, *, name, bufs, sems_in=(), sems_out=(), after=(), token=True):
    n = len(bufs)
    operands = [pltpu.with_memory_space_constraint(b, pltpu.HBM) for b in bufs] + list(sems_in) + list(after)
    in_specs = [HBM_SPEC] * n + [SEM_SPEC] * len(sems_in) + [ANY_SPEC] * len(after)
    out_shape = list(sems_out) + [pltpu.HBM(b.shape, b.dtype) for b in bufs]
    out_specs = [SEM_SPEC] * len(sems_out) + [HBM_SPEC] * n
    if token:
        out_shape.append(_sds((8, 128), F32))
        out_specs.append(pl.BlockSpec(memory_space=pltpu.VMEM))
    body = functools.partial(_split_body, n=n, n_sem_in=len(sems_in), n_after=len(after),
                             n_sem_out=len(sems_out), steps=tuple(steps), token=token)
    return pl.pallas_call(
        body, name=name, in_specs=in_specs, out_specs=tuple(out_specs), out_shape=tuple(out_shape),
        input_output_aliases={i: len(sems_out) + i for i in range(n)},
        compiler_params=pltpu.CompilerParams(
            has_side_effects=pltpu.SideEffectType.DATAFLOW_SIDE_EFFECTING))(*operands)


def _sem_pair(n, m):
    return [pltpu.SemaphoreType.DMA((n * m,)), pltpu.SemaphoreType.DMA((n * m,))]


def _gather_start(bufs, after, tag):
    n = len(bufs)
    out = _split_call([(_gather_send_own, 0)], name="gather_start_" + tag, bufs=bufs,
                      sems_out=_sem_pair(n, 4), after=after)
    return out[0], out[1], list(out[2:2 + n]), out[-1]


def _gather_relay(state, after, tag):
    send, recv, bufs, _ = state
    n = len(bufs)
    out = _split_call([(_gather_await_own, 0), (_gather_relay_send, 2)], name="gather_relay_" + tag, bufs=bufs,
                      sems_in=[send, recv], sems_out=_sem_pair(n, 3), after=after)
    return out[0], out[1], list(out[2:2 + n]), out[-1]


def _gather_finish(state, after, tag):
    send, recv, bufs, _ = state
    out = _split_call([(_gather_relay_await, 0)], name="gather_finish_" + tag, bufs=bufs,
                      sems_in=[send, recv], after=after, token=False)
    return list(out)


def _pair_arrays(n):
    def split(step):
        return lambda bufs, send, recv: step(bufs[:n], bufs[n:], send, recv)
    return split


def _reduce_pairs_start(grads, after, tag):
    n = len(grads)
    stage = [lax.empty((4,) + g.shape[1:], g.dtype) for g in grads]
    out = _split_call([(_pair_arrays(n)(_reduce_pair_send), 0)], name="reduce_pairs_start_" + tag,
                      bufs=list(grads) + stage, sems_out=_sem_pair(n, 4), after=after)
    return out[0], out[1], list(out[2:2 + n]), list(out[2 + n:2 + 2 * n]), out[-1]


def _reduce_pairs_finish(state, after, tag):
    send, recv, grads, stage, _ = state
    n = len(grads)
    out = _split_call([(_pair_arrays(n)(_reduce_pair_await), 0)], name="reduce_pairs_finish_" + tag,
                      bufs=list(grads) + list(stage), sems_in=[send, recv], after=after, token=False)
    return list(out[:n]), list(out[n:])


def _reduce_chips_start(sums, after, tag):
    n = len(sums)
    stage = [lax.empty(p.shape, p.dtype) for p in sums]
    out = _split_call([(_pair_arrays(n)(_reduce_chip_send), 0)], name="reduce_chips_start_" + tag,
                      bufs=list(sums) + stage, sems_out=_sem_pair(n, 3), after=after)
    return out[0], out[1], list(out[2:2 + n]), list(out[2 + n:2 + 2 * n]), out[-1]


def _reduce_chips_finish(state, after, tag):
    send, recv, sums, stage, _ = state
    n = len(sums)
    out = _split_call([(_pair_arrays(n)(_reduce_chip_await), 0)], name="reduce_chips_finish_" + tag,
                      bufs=list(sums) + list(stage), sems_in=[send, recv], after=after, token=False)
    return list(out[:n]), list(out[n:])


def _pair_sum_body(c_ref, a_ref, b_ref, o_ref):
    o_ref[...] = (a_ref[...].astype(F32) + b_ref[...].astype(F32)).astype(o_ref.dtype)


def _pair_sum(grad, stage, core):
    _, r, c = grad.shape
    tr = _slab_rows(r, 1024)
    blk = pl.BlockSpec((None, tr, c), lambda q, i, c_ref: (q, i, 0))
    spec = pltpu.PrefetchScalarGridSpec(
        num_scalar_prefetch=1, grid=(4, r // tr),
        in_specs=[pl.BlockSpec((None, tr, c), lambda q, i, c_ref: (2 * q + c_ref[0], i, 0)), blk],
        out_specs=blk)
    return _call(functools.partial(_pair_sum_body), name="pair_sum", out_shape=_sds((4, r, c), grad.dtype),
                 grid_spec=spec)(core, grad, stage)


def _adamw_math(w, g, m, v):
    m = ADAM_B1 * m + (1.0 - ADAM_B1) * g
    v = ADAM_B2 * v + (1.0 - ADAM_B2) * (g * g)
    m_hat = m / (1.0 - ADAM_B1 ** ADAM_STEP)
    v_hat = v / (1.0 - ADAM_B2 ** ADAM_STEP)
    delta = -ADAM_LR * (m_hat / (jnp.sqrt(v_hat) + ADAM_EPS) + ADAM_WD * w)
    return delta, m, v


def _adamw_body(q_ref, own_ref, r_ref, w_ref, m_ref, v_ref, *rest):
    g_out, d_out, m_out, v_out = rest[-4:]
    q = q_ref[0]
    g = None
    for j in range(4):
        term = jnp.where(q == j, own_ref[...], r_ref[j]).astype(F32)
        g = term if g is None else g + term
    delta, m, v = _adamw_math(w_ref[...], g, m_ref[...], v_ref[...])
    g_out[...] = g
    d_out[...] = delta
    m_out[...] = m
    v_out[...] = v


def _adamw_layer(sums, recv, w, m, v, prev, layer, chip):
    l, r, c = w.shape
    tr = _slab_rows(r)
    lay = pl.BlockSpec((None, tr, c), lambda i, q_ref: (layer, i, 0))
    spec = pltpu.PrefetchScalarGridSpec(
        num_scalar_prefetch=1, grid=(r // tr,),
        in_specs=[pl.BlockSpec((None, tr, c), lambda i, q_ref: (q_ref[0], i, 0)),
                  pl.BlockSpec((4, tr, c), lambda i, q_ref: (0, i, 0)), lay, lay, lay] + [ANY_SPEC] * 4,
        out_specs=(lay, lay, lay, lay))
    out = _sds((l, r, c), F32)
    return _call(
        functools.partial(_adamw_body), name="adamw", out_shape=(out, out, out, out), grid_spec=spec,
        aliases={6: 0, 7: 1, 8: 2, 9: 3})(chip, sums, recv, w, m, v, *prev)


def _adamw_small_body(r_ref, w_ref, m_ref, v_ref, g_out, d_out, m_out, v_out):
    g = r_ref[0]
    for j in range(1, N_DEV):
        g = g + r_ref[j]
    delta, m, v = _adamw_math(w_ref[...], g, m_ref[...], v_ref[...])
    g_out[...] = g
    d_out[...] = delta
    m_out[...] = m
    v_out[...] = v


def _adamw_small(recv, w, m, v):
    r, c = w.shape
    tr = 8
    for cand in (1024, 512, 256, 128, 64, 32, 16, 8):
        if r % cand == 0:
            tr = cand
            break
    blk = pl.BlockSpec((tr, c), lambda i: (i, 0))
    out = _sds((r, c), F32)
    return _call(
        functools.partial(_adamw_small_body), name="adamw_small", out_shape=(out, out, out, out),
        grid=(r // tr,),
        in_specs=[pl.BlockSpec((N_DEV, tr, c), lambda i: (0, i, 0)), blk, blk, blk],
        out_specs=(blk, blk, blk, blk))(recv, w, m, v)


def _as3d(a):
    return a.reshape(a.shape[0], -1, a.shape[-1])


GROUPS = (("w_ffn1_up", "w_ffn1_down"),
          ("w_in", "pool_group_w", "w_pool_out", "w_sgu_out", "w_out"),
          ("w_ffn2_up", "w_ffn2_down"))


def _ffn_forward(x, h, g_pre, wu, wd, g_post, g_next, after=(), mid=None):
    if h is None:
        h, after = _rms_fwd(x, g_pre, after=after), ()
    gu, a = _ffn_up(h, wu, after=after)
    tokens = tuple(mid(a)) if mid else ()
    wd = wd() if callable(wd) else wd
    x_out, f, *h_next = _down_norm_res(a, wd, x, g_post, MACARON_WEIGHT, g_next, after=tokens)
    return x_out, (h_next[0] if h_next else None), (wd, (x, h, gu, a, f))


def _ffn_backward(dy, saved, g_pre, wu, wd, g_post, norm_in=None, norm_next=None, mid=None, early=None):
    x, h, gu, a, f = saved
    df, dg_post = norm_in if norm_in else _norm_res_bwd(dy, f, g_post, MACARON_WEIGHT)
    dz = _dgrad_ffn(df, wd, gu)
    tokens = tuple(mid(dz)) if mid else ()
    dwd = _wgrad_rows(a, df).reshape(N_DEV, -1, df.shape[1])
    dwu = _wgrad_cs(h, dz)
    tokens += tuple(early([dwu, dwd])) if early else (dwu, dwd)
    dy, dg_pre, *norm_out = _dgrad_cs_norm(dz, wu, x, g_pre, dy, norm_next, after=tokens)
    return dy, dg_pre, dg_post, (tuple(norm_out) if norm_out else None)


def kernel(x, g_ffn1_pre, w_ffn1_up, w_ffn1_down, g_ffn1_post, g_mix_pre, w_in, pool_group_w, pool_scale, w_pool_out, sgu_v_gain, sgu_w_s, sgu_b_s, w_sgu_out, w_out, g_mix_post, g_ffn2_pre, w_ffn2_up, w_ffn2_down, g_ffn2_post, loss_target, m_g_ffn1_pre, m_w_ffn1_up, m_w_ffn1_down, m_g_ffn1_post, m_g_mix_pre, m_w_in, m_pool_group_w, m_pool_scale, m_w_pool_out, m_sgu_v_gain, m_sgu_w_s, m_sgu_b_s, m_w_sgu_out, m_w_out, m_g_mix_post, m_g_ffn2_pre, m_w_ffn2_up, m_w_ffn2_down, m_g_ffn2_post, v_g_ffn1_pre, v_w_ffn1_up, v_w_ffn1_down, v_g_ffn1_post, v_g_mix_pre, v_w_in, v_pool_group_w, v_pool_scale, v_w_pool_out, v_sgu_v_gain, v_sgu_w_s, v_sgu_b_s, v_w_sgu_out, v_w_out, v_g_mix_post, v_g_ffn2_pre, v_w_ffn2_up, v_w_ffn2_down, v_g_ffn2_post):
    given = dict(locals())
    weights = {n: given[n] for n in ORDER}
    mom_m = {n: given["m_" + n] for n in ORDER}
    mom_v = {n: given["v_" + n] for n in ORDER}
    depth = g_ffn1_pre.shape[0]
    d_model = x.shape[-1]
    pool_w = pool_scale.shape[1]
    sgu_w = sgu_v_gain.shape[1]
    groups, grp_rows, grp_c = pool_group_w.shape[1:]
    gate_off = pool_w + 2 * sgu_w

    xs = x.reshape(x.shape[-2], d_model)
    target = loss_target.reshape(xs.shape)

    px, py, pc = _mesh_position()
    me = (4 * px + 2 * py + pc).astype(jnp.int32).reshape(1)
    core = pc.astype(jnp.int32).reshape(1)
    chip = (2 * px + py).astype(jnp.int32).reshape(1)

    n_groups = len(GROUPS) * depth
    ffn_gains = {0: ("g_ffn1_pre", "g_ffn1_post"), 2: ("g_ffn2_pre", "g_ffn2_post")}

    def start_gather(t, after):
        layer, gi = divmod(t, len(GROUPS))
        bufs = [_cast_to_slot(_as3d(weights[n]), layer, me) for n in GROUPS[gi]]
        return _gather_start(bufs, after, "g%d" % t)

    def vec(name, i):
        return weights[name][i].reshape(1, -1)

    pre_gains = ("g_ffn1_pre", "g_mix_pre", "g_ffn2_pre")

    def forward_group(t, cur, h, bufs, after, mid):
        layer, gi = divmod(t, len(GROUPS))
        gw = dict(zip(GROUPS[gi], bufs))
        nxt = divmod(t + 1, len(GROUPS))
        g_next = vec(pre_gains[nxt[1]], nxt[0]) if t + 1 < n_groups else None
        if gi != 1:
            pre, post = ffn_gains[gi]
            wu, wd = (gw[n] for n in GROUPS[gi])
            wd = wd if callable(wd) else wd.reshape(-1, d_model)
            cur, h_next, (wd, sv) = _ffn_forward(cur, h, vec(pre, layer), wu, wd, vec(post, layer), g_next,
                                                 after=after, mid=mid)
            return cur, h_next, ((wu, wd), sv)
        wgrp = gw["pool_group_w"].reshape(N_DEV, groups, grp_rows, grp_c).transpose(1, 0, 2, 3)
        wts = dict(win=gw["w_in"], wgrp=wgrp.reshape(groups, grp_c, grp_c), wpo=gw["w_pool_out"],
                   wso=gw["w_sgu_out"], wout=gw["w_out"].reshape(-1, d_model),
                   ws=weights["sgu_w_s"][layer], b3=weights["sgu_b_s"][layer][:, :, None])
        x1 = cur
        if h is None:
            h, after = _rms_fwd(x1, vec("g_mix_pre", layer), after=after), ()
        proj = _mm_cs(h, wts["win"], after=after)
        tokens = tuple(mid(proj))
        yps = _pool_fwd(proj, wts["wgrp"], vec("pool_scale", layer))
        q = _sgu_fwd(proj, pool_w, vec("sgu_v_gain", layer), wts["ws"], wts["b3"])
        m, parts = _mix_gate(yps, q, proj, gate_off, wts["wpo"], wts["wso"])
        cur, o, *h_next = _down_norm_res(m, wts["wout"], x1, vec("g_mix_post", layer), 1.0, g_next, after=tokens)
        return cur, (h_next[0] if h_next else None), (wts, (x1, h, proj, yps, q, parts, m, o))

    first = [_gather_start([_cast_to_slot(_as3d(weights[n]), 0, me)], (), "g0" + n[-2:]) for n in GROUPS[0]]
    states = {1: start_gather(1, (first[0][-1], first[1][-1]))}
    up0 = _gather_finish(_gather_relay(first[0], (states[1][-1],), "g0up"), (), "g0up")
    down0 = []
    bufs = [up0[0], lambda: down0[0][0].reshape(-1, d_model)]
    saved = []
    cur, h = xs, None
    for t in range(n_groups):
        after = []
        if t + 2 < n_groups:
            states[t + 2] = start_gather(t + 2, (bufs[0],))
            after.append(states[t + 2][-1])
        relayed = []

        def mid(result, t=t, relayed=relayed):
            if t == 0:
                down0.append(_gather_finish(_gather_relay(first[1], (result,), "g0wn"), (), "g0wn"))
            if t + 1 < n_groups:
                relayed.append(_gather_relay(states[t + 1], (result,), "g%d" % (t + 1)))
            return tuple(r[-1] for r in relayed)

        cur, h, sv = forward_group(t, cur, h, bufs, tuple(after), mid)
        saved.append(sv)
        if t + 1 < n_groups:
            bufs = _gather_finish(relayed[0], (cur,), "g%d" % (t + 1))

    dy, sq = _loss_grad(cur, target)
    loss = lax.psum(0.5 * sq[0, 0] / d_model, MESH_AXES)

    small_grads = {n: [None] * depth for n in SMALL}
    outs = {n: None for n in BIG}

    def chip_sums(pairs, after, tag):
        grads, stage = _reduce_pairs_finish(pairs, after, tag)
        sums = [_pair_sum(g, s, core) for g, s in zip(grads, stage)]
        return _reduce_chips_start(sums, (), tag)

    def update(chips, after, t):
        layer, gi = divmod(t, len(GROUPS))
        sums, recv = _reduce_chips_finish(chips, after, "g%d" % t)
        for n, p, r in zip(GROUPS[gi], sums, recv):
            w3, m3, v3 = _as3d(weights[n]), _as3d(mom_m[n]), _as3d(mom_v[n])
            prev = outs[n] if outs[n] is not None else [lax.empty(w3.shape, F32) for _ in range(4)]
            outs[n] = _adamw_layer(p, r, w3, m3, v3, prev, layer, chip)

    def sandwich_norm(t):
        layer, gi = divmod(t, len(GROUPS))
        sv = saved[t][1]
        if gi != 1:
            return sv[4], vec(ffn_gains[gi][1], layer), MACARON_WEIGHT
        return sv[7], vec("g_mix_post", layer), 1.0

    def backward_group(t, dy, sv, norm_in, mid, early):
        layer, gi = divmod(t, len(GROUPS))
        norm_next = sandwich_norm(t - 1) if t > 0 else None
        wts, sv = sv
        if gi != 1:
            pre, post = ffn_gains[gi]
            dy, dg, dgp, norm_out = _ffn_backward(dy, sv, vec(pre, layer), wts[0], wts[1], vec(post, layer),
                                                  norm_in=norm_in, norm_next=norm_next, mid=mid, early=early)
            small_grads[pre][layer], small_grads[post][layer] = dg, dgp
            return dy, norm_out
        x1, h2, proj, yps, q, parts, m, o = sv
        do, small_grads["g_mix_post"][layer] = norm_in if norm_in else _norm_res_bwd(
            dy, o, vec("g_mix_post", layer), 1.0)
        dya, dyb, dga, dgb = _dgrad_gate(do, wts["wout"], parts)
        tokens = tuple(mid(dya)) if mid else ()
        dwout = _wgrad_rows(m, do).reshape(N_DEV, -1, d_model)
        dwpo = _wgrad_cs_small(yps, dya)
        dwso = _wgrad_cs_small(q, dyb)
        dyps = _dgrad_cs(dya, wts["wpo"], after=tokens)
        dq = _dgrad_cs(dyb, wts["wso"])
        dp, dwgrp, small_grads["pool_scale"][layer] = _pool_bwd(proj, dyps, wts["wgrp"], vec("pool_scale", layer))
        duv, dws, db3, small_grads["sgu_v_gain"][layer] = _sgu_bwd(
            proj, pool_w, dq, vec("sgu_v_gain", layer), wts["ws"], wts["b3"])
        small_grads["sgu_w_s"][layer] = dws
        small_grads["sgu_b_s"][layer] = db3
        dproj = jnp.concatenate([dp, duv, dga, dgb], axis=1)[None]
        dwin = _wgrad_cs(h2, dproj)
        dwgrp8 = dwgrp.reshape(groups, N_DEV, grp_rows, grp_c).transpose(1, 0, 2, 3)
        dwgrp8 = dwgrp8.reshape(N_DEV, groups * grp_rows, grp_c).astype(BF16)
        dy, small_grads["g_mix_pre"][layer], *norm_out = _dgrad_cs_norm(
            dproj, wts["win"], x1, vec("g_mix_pre", layer), dy, norm_next,
            after=tuple(early([dwin, dwgrp8, dwpo, dwso, dwout])))
        return dy, (tuple(norm_out) if norm_out else None)

    pairs, norm_in = None, None
    for t in reversed(range(n_groups)):
        started, mine = [], []

        def mid(first, t=t, started=started):
            started.append(chip_sums(pairs, (first,), "g%d" % (t + 1)))
            return (started[0][-1],)

        def early(grads, t=t, mine=mine):
            mine.append(_reduce_pairs_start(grads, (), "g%d" % t))
            return (mine[0][-1],)

        dy, norm_in = backward_group(t, dy, saved[t], norm_in, mid if pairs else None, early)
        if started:
            update(started[0], (dy,), t + 1)
        pairs = mine[0]
    last = chip_sums(pairs, (), "g0")

    def pack(parts):
        return jnp.concatenate([p.reshape(depth, -1) for p in parts], axis=1).reshape(-1, 128)

    part = pack([jnp.stack(small_grads[n]) for n in SMALL])
    small = _gather_start([_cast_to_slot(part[None], 0, me, F32)], (last[-1],), "small")
    done = tuple(outs[n][0] for n in BIG if outs[n] is not None)
    all_parts = _gather_finish(_gather_relay(small, done, "small"), (), "small")[0]
    small_out = _adamw_small(all_parts, pack([weights[n] for n in SMALL]), pack([mom_m[n] for n in SMALL]),
                             pack([mom_v[n] for n in SMALL]))
    update(last, (small_out[0],), 0)

    results = {}
    sizes = [weights[n][0].size for n in SMALL]
    for kind, packed in zip(("grad", "delta", "new_m", "new_v"), small_out):
        flat = packed.reshape(depth, -1)
        off = 0
        for n, sz in zip(SMALL, sizes):
            results[(kind, n)] = flat[:, off:off + sz].reshape(weights[n].shape)
            off += sz
    for n in BIG:
        for kind, arr in zip(("grad", "delta", "new_m", "new_v"), outs[n]):
            results[(kind, n)] = arr.reshape(weights[n].shape)

    grad_x = dy.reshape(x.shape)
    return (loss, grad_x, *[results[("grad", n)] for n in ORDER], *[results[("delta", n)] for n in ORDER],
            *[results[("new_m", n)] for n in ORDER], *[results[("new_v", n)] for n in ORDER])
```
